```python
import math
import numpy as np
import jax
import jax.numpy as jnp
from jax import lax

D_MODEL = 1024
BATCH = 16
SEQ = 2048
DEPTH = 2

N_MIXERS = 4
GROUP_W = D_MODEL // N_MIXERS
EPS = 1e-6
ROPE_THETA = 500000.0
Q_BLOCK = 128
NEG = -1e30
MAX_START = 4096

MLA_HEADS = 4
MLA_ROPE = 32
MLA_NOPE = 64
MLA_V = GROUP_W // MLA_HEADS
MLA_QK = MLA_ROPE + MLA_NOPE
MLA_Q_RANK = 3 * GROUP_W // 4
MLA_KV_RANK = GROUP_W // 2

LRU_W = GROUP_W
LRU_BLOCKS = 4
LRU_BW = LRU_W // LRU_BLOCKS
CONV_W = 4
LRU_C = 8.0

S5_W = GROUP_W
S5_CH = 16
S5_GROUPS = S5_W // S5_CH
S5_P = 64

NSA_HEADS = 4
NSA_DK = GROUP_W // NSA_HEADS
ROT_DIM = NSA_DK // 4
CMP_LEN = 32
CMP_STRIDE = 16
CMP_HID = 128
SEL_LEN = 64
SEL_TOPK = 5
WIN = 512

N_GROUPS = 4
EXP_PER_GROUP = 4
N_EXPERTS = N_GROUPS * EXP_PER_GROUP
D_EXPERT = 256
TOPK_IN_GROUP = 2

IN_COLS = (MLA_Q_RANK, MLA_KV_RANK, MLA_ROPE, LRU_W, LRU_W, S5_W, NSA_HEADS * NSA_DK, 6 * NSA_DK, 3 * NSA_HEADS)
IN_SPLITS = tuple(sum(IN_COLS[:i + 1]) for i in range(len(IN_COLS) - 1))
D_IN = sum(IN_COLS)

kernel_name = 'hybrid_mla_rglru_s5_nsa_hmoe'


def rms_norm(x, g):
    xf = x.astype(jnp.float32)
    y = xf * lax.rsqrt(jnp.mean(xf * xf, axis=-1, keepdims=True) + EPS) * g.astype(jnp.float32)
    return y.astype(x.dtype)


def rope(x, pos, rot_dim):
    half = rot_dim // 2
    inv = ROPE_THETA ** (-jnp.arange(half, dtype=jnp.float32) * 2.0 / rot_dim)
    ang = pos.astype(jnp.float32)[..., None] * inv
    c = jnp.cos(ang)[:, :, None, :]
    s = jnp.sin(ang)[:, :, None, :]
    xf = x.astype(jnp.float32)
    x1 = xf[..., :half]
    x2 = xf[..., half:rot_dim]
    out = jnp.concatenate([x1 * c - x2 * s, x1 * s + x2 * c, xf[..., rot_dim:]], axis=-1)
    return out.astype(x.dtype)


def masked_softmax(s, mask):
    s = jnp.where(mask, s.astype(jnp.float32), NEG)
    p = jax.nn.softmax(s, axis=-1)
    return jnp.where(mask, p, 0.0)


def linear_scan(a, b):
    def comb(l, r):
        return (r[0] * l[0], r[0] * l[1] + r[1])
    return lax.associative_scan(comb, (a, b), axis=1)[1]


def causal_attention(q, k, v, scale):
    B, S, H, Dq = q.shape
    nq = S // Q_BLOCK
    qb = q.reshape(B, nq, Q_BLOCK, H, Dq).transpose(1, 0, 2, 3, 4)
    kpos = jnp.arange(S)

    def block(args):
        qc, c = args
        s = jnp.einsum('bqhd,bkhd->bhqk', qc, k) * scale
        qpos = c * Q_BLOCK + jnp.arange(Q_BLOCK)
        p = masked_softmax(s, kpos[None, :] <= qpos[:, None])
        return jnp.einsum('bhqk,bkhd->bqhd', p.astype(v.dtype), v)

    o = lax.map(block, (qb, jnp.arange(nq)))
    return o.transpose(1, 0, 2, 3, 4).reshape(B, S, H, v.shape[-1])


def mla(c_q, c_kv, k_pe, pos, g_cq, g_ckv, w_uq, w_ukv, g_q, g_k):
    B, S, _ = c_q.shape
    H = MLA_HEADS
    q = (rms_norm(c_q, g_cq) @ w_uq).reshape(B, S, H, MLA_QK)
    kv = (rms_norm(c_kv, g_ckv) @ w_ukv).reshape(B, S, H, MLA_NOPE + MLA_V)
    k_nope, v = kv[..., :MLA_NOPE], kv[..., MLA_NOPE:]
    k_rope = jnp.broadcast_to(k_pe[:, :, None, :], (B, S, H, MLA_ROPE))
    k = jnp.concatenate([k_rope, k_nope], axis=-1)
    q = rope(rms_norm(q, g_q), pos, MLA_ROPE)
    k = rope(rms_norm(k, g_k), pos, MLA_ROPE)
    o = causal_attention(q, k, v, MLA_QK ** -0.5)
    return o.reshape(B, S, H * MLA_V)


def rglru(xb, gb, w_conv, b_conv, w_a, b_a, w_i, b_i, lam):
    B, S, C = xb.shape
    xp = jnp.pad(xb, ((0, 0), (CONV_W - 1, 0), (0, 0)))
    u = b_conv
    for j in range(CONV_W):
        u = u + xp[:, j:j + S] * w_conv[j]
    ub = u.reshape(B, S, LRU_BLOCKS, LRU_BW)
    r = jax.nn.sigmoid(jnp.einsum('bshi,hij->bshj', ub, w_a) + b_a).reshape(B, S, C).astype(jnp.float32)
    gi = jax.nn.sigmoid(jnp.einsum('bshi,hij->bshj', ub, w_i) + b_i).reshape(B, S, C).astype(jnp.float32)
    log_a = -LRU_C * r * jax.nn.softplus(-lam.astype(jnp.float32))
    a = jnp.exp(log_a)
    mult = jnp.sqrt(jnp.maximum(-jnp.expm1(2.0 * log_a), 0.0))
    mult = jnp.where(jnp.arange(S)[None, :, None] == 0, 1.0, mult)
    h = linear_scan(a, mult * gi * u.astype(jnp.float32))
    y = h * jax.nn.gelu(gb.astype(jnp.float32))
    return y.astype(xb.dtype)


def s5(u, a_re, a_im, log_dt, b_re, b_im, c_re, c_im, d, w_glu, b_glu):
    B, S, _ = u.shape
    f32 = jnp.float32
    uf = u.astype(f32).reshape(B, S, S5_GROUPS, S5_CH)
    a_re = a_re.astype(f32)
    a_im = a_im.astype(f32)
    b_re = b_re.astype(f32)
    b_im = b_im.astype(f32)
    dt = jnp.exp(log_dt.astype(f32))[:, None]
    mag = jnp.exp(dt * a_re)
    ab_re = mag * jnp.cos(dt * a_im)
    ab_im = mag * jnp.sin(dt * a_im)
    den = a_re * a_re + a_im * a_im
    n_re = ab_re - 1.0
    g_re = (n_re * a_re + ab_im * a_im) / den
    g_im = (ab_im * a_re - n_re * a_im) / den
    bb_re = g_re[..., None] * b_re - g_im[..., None] * b_im
    bb_im = g_re[..., None] * b_im + g_im[..., None] * b_re
    bu_re = jnp.einsum('bsgh,gph->bsgp', uf, bb_re)
    bu_im = jnp.einsum('bsgh,gph->bsgp', uf, bb_im)
    ar = jnp.broadcast_to(ab_re, bu_re.shape)
    ai = jnp.broadcast_to(ab_im, bu_re.shape)

    def comb(l, r):
        ar1, ai1, br1, bi1 = l
        ar2, ai2, br2, bi2 = r
        return (ar2 * ar1 - ai2 * ai1, ar2 * ai1 + ai2 * ar1,
                ar2 * br1 - ai2 * bi1 + br2, ar2 * bi1 + ai2 * br1 + bi2)

    _, _, h_re, h_im = lax.associative_scan(comb, (ar, ai, bu_re, bu_im), axis=1)
    y = (jnp.einsum('bsgp,ghp->bsgh', h_re, c_re.astype(f32))
         - jnp.einsum('bsgp,ghp->bsgh', h_im, c_im.astype(f32)))
    y = y.reshape(B, S, S5_W) + d.astype(f32) * u.astype(f32)
    y = jax.nn.gelu(y)
    y = y * jax.nn.sigmoid(y @ w_glu.astype(f32) + b_glu.astype(f32))
    return y.astype(u.dtype)


def nsa(q, kv, gate_logits, pos, g_q, g_k, pe_k, w1_k, w2_k, pe_v, w1_v, w2_v):
    B, S, _ = q.shape
    H, Dh = NSA_HEADS, NSA_DK
    scale = Dh ** -0.5
    k_cmp, v_cmp, k_slc, v_slc, k_win, v_win = jnp.split(kv, 6, axis=-1)
    qh = rope(rms_norm(q.reshape(B, S, H, Dh), g_q), pos, ROT_DIM)
    qidx = jnp.arange(S)

    def key_prep(k, g, p):
        return rope(rms_norm(k, g)[:, :, None, :], p, ROT_DIM)[:, :, 0, :]

    nc = (S - CMP_LEN) // CMP_STRIDE + 1
    blk = np.arange(nc)[:, None] * CMP_STRIDE + np.arange(CMP_LEN)[None, :]
    blk_end = blk[:, -1]

    def compress(t, pe, w1, w2):
        tb = (t[:, blk] + pe).reshape(B, nc, CMP_LEN * Dh)
        return jax.nn.gelu(tb @ w1) @ w2

    kc = key_prep(compress(k_cmp, pe_k, w1_k, w2_k), g_k[0], pos[:, blk_end])
    vc = compress(v_cmp, pe_v, w1_v, w2_v)
    s_c = jnp.einsum('bshd,bcd->bhsc', qh, kc) * scale
    p_c = masked_softmax(s_c, blk_end[None, :] <= qidx[:, None])
    o_cmp = jnp.einsum('bhsc,bcd->bshd', p_c.astype(vc.dtype), vc)

    nsb = S // SEL_LEN
    cs = np.arange(nc) * CMP_STRIDE
    ss = np.arange(nsb) * SEL_LEN
    ov = np.clip(np.minimum(cs[:, None] + CMP_LEN, ss[None, :] + SEL_LEN)
                 - np.maximum(cs[:, None], ss[None, :]), 0, None) / CMP_STRIDE
    imp = jnp.einsum('bhsc,cj->bsj', p_c, jnp.asarray(ov, jnp.float32))
    cur = qidx // SEL_LEN
    sb = jnp.arange(nsb)
    forced = (sb[None, :] == 0) | (sb[None, :] == cur[:, None]) | (sb[None, :] == cur[:, None] - 1)
    future = sb[None, :] > cur[:, None]
    score = jnp.where(future, -jnp.inf, jnp.where(forced, jnp.inf, imp))
    ksel = min(SEL_TOPK, nsb)
    _, sel = lax.top_k(score, ksel)
    k_s = key_prep(k_slc, g_k[1], pos)
    nq = S // Q_BLOCK
    gather = jax.vmap(lambda kb, ib: kb[ib])

    def sel_block(args):
        qc, sc, qp = args
        tok = (sc[..., None] * SEL_LEN + jnp.arange(SEL_LEN)).reshape(B, Q_BLOCK, ksel * SEL_LEN)
        kg = gather(k_s, tok)
        vg = gather(v_slc, tok)
        s = jnp.einsum('bqhd,bqnd->bhqn', qc, kg) * scale
        p = masked_softmax(s, (tok <= qp[None, :, None])[:, None])
        return jnp.einsum('bhqn,bqnd->bqhd', p.astype(vg.dtype), vg)

    q_ch = qh.reshape(B, nq, Q_BLOCK, H, Dh).transpose(1, 0, 2, 3, 4)
    sel_ch = sel.reshape(B, nq, Q_BLOCK, ksel).transpose(1, 0, 2, 3)
    o_slc = lax.map(sel_block, (q_ch, sel_ch, qidx.reshape(nq, Q_BLOCK)))
    o_slc = o_slc.transpose(1, 0, 2, 3, 4).reshape(B, S, H, Dh)

    k_w = key_prep(k_win, g_k[2], pos)
    span = WIN + Q_BLOCK
    kidx = np.arange(nq)[:, None] * Q_BLOCK + np.arange(span)[None, :]
    kpos = kidx - WIN
    kw = jnp.pad(k_w, ((0, 0), (WIN, 0), (0, 0)))[:, kidx]
    vw = jnp.pad(v_win, ((0, 0), (WIN, 0), (0, 0)))[:, kidx]
    qb = qh.reshape(B, nq, Q_BLOCK, H, Dh)
    s_w = jnp.einsum('bcqhd,bckd->bhcqk', qb, kw) * scale
    qpos = np.arange(nq)[:, None] * Q_BLOCK + np.arange(Q_BLOCK)[None, :]
    mask_w = ((kpos[:, None, :] <= qpos[:, :, None]) & (qpos[:, :, None] - kpos[:, None, :] < WIN)
              & (kpos[:, None, :] >= 0))
    p_w = masked_softmax(s_w, mask_w)
    o_win = jnp.einsum('bhcqk,bckd->bcqhd', p_w.astype(vw.dtype), vw).reshape(B, S, H, Dh)

    g = jax.nn.sigmoid(gate_logits.astype(jnp.float32)).reshape(B, S, H, 3)
    o = g[..., 0:1] * o_cmp + g[..., 1:2] * o_slc + g[..., 2:3] * o_win
    return o.reshape(B, S, H * Dh).astype(q.dtype)


def hier_moe(h, w_rg, b_rg, w_re, b_re, w_gate, w_up, w_down):
    B, S, D = h.shape
    f32 = jnp.float32
    t = h.reshape(B * S, D)
    lg = (t @ w_rg).astype(f32) + b_rg.astype(f32)
    pg = jax.nn.softmax(lg, axis=-1)
    gi = jnp.argmax(lg, axis=-1)
    pg_top = jnp.take_along_axis(pg, gi[:, None], axis=1)
    le = ((t @ w_re).astype(f32) + b_re.astype(f32)).reshape(-1, N_GROUPS, EXP_PER_GROUP)
    le_g = jnp.take_along_axis(le, gi[:, None, None], axis=1)[:, 0]
    top_v, top_i = lax.top_k(jax.nn.softmax(le_g, axis=-1), TOPK_IN_GROUP)
    w = pg_top * top_v / jnp.sum(top_v, axis=-1, keepdims=True)
    eid = gi[:, None] * EXP_PER_GROUP + top_i
    comb = jnp.sum(jax.nn.one_hot(eid, N_EXPERTS, dtype=f32) * w[..., None], axis=1)
    y = jnp.zeros((B * S, D), f32)
    for e in range(N_EXPERTS):
        he = jax.nn.silu(t @ w_gate[e]) * (t @ w_up[e])
        y = y + comb[:, e:e + 1] * (he @ w_down[e]).astype(f32)
    return y.reshape(B, S, D).astype(h.dtype)


def setup_inputs(seed: int = 0) -> dict:
    key = jax.random.key(seed)
    ks = iter(jax.random.split(key, 64))
    f32 = jnp.float32
    L = DEPTH

    def rn(shape, scale):
        return scale * jax.random.normal(next(ks), shape, f32)

    def gain(shape):
        return 1.0 + rn(shape, 0.02)

    x = rn((BATCH, SEQ, D_MODEL), 1.0)
    start = jax.random.randint(next(ks), (BATCH, 1), 0, MAX_START, dtype=jnp.int32)
    positions = start + jnp.arange(SEQ, dtype=jnp.int32)[None, :]
    u_lam = jax.random.uniform(next(ks), (L, LRU_W), f32, 0.9, 0.999)
    s_lam = u_lam ** (1.0 / LRU_C)
    lru_lambda = jnp.log(s_lam) - jnp.log1p(-s_lam)
    s5_log_dt = jax.random.uniform(next(ks), (L, S5_GROUPS), f32, math.log(1e-3), math.log(1e-1))
    return {
        'x': x,
        'positions': positions,
        'mix_norm': gain((L, D_MODEL)),
        'w_in': rn((L, D_MODEL, D_IN), D_MODEL ** -0.5),
        'mla_g_cq': gain((L, MLA_Q_RANK)),
        'mla_g_ckv': gain((L, MLA_KV_RANK)),
        'mla_w_uq': rn((L, MLA_Q_RANK, MLA_HEADS * MLA_QK), MLA_Q_RANK ** -0.5),
        'mla_w_ukv': rn((L, MLA_KV_RANK, MLA_HEADS * (MLA_NOPE + MLA_V)), MLA_KV_RANK ** -0.5),
        'mla_g_q': gain((L, MLA_QK)),
        'mla_g_k': gain((L, MLA_QK)),
        'lru_conv_w': rn((L, CONV_W, LRU_W), CONV_W ** -0.5),
        'lru_conv_b': rn((L, LRU_W), 0.02),
        'lru_w_a': rn((L, LRU_BLOCKS, LRU_BW, LRU_BW), LRU_BW ** -0.5),
        'lru_b_a': rn((L, LRU_BLOCKS, LRU_BW), 0.1),
        'lru_w_i': rn((L, LRU_BLOCKS, LRU_BW, LRU_BW), LRU_BW ** -0.5),
        'lru_b_i': rn((L, LRU_BLOCKS, LRU_BW), 0.1),
        'lru_lambda': lru_lambda,
        's5_a_re': -0.5 + rn((L, S5_GROUPS, S5_P), 0.01),
        's5_a_im': math.pi * jnp.arange(S5_P, dtype=f32) + rn((L, S5_GROUPS, S5_P), 0.01),
        's5_log_dt': s5_log_dt,
        's5_b_re': rn((L, S5_GROUPS, S5_P, S5_CH), (2.0 * S5_CH) ** -0.5),
        's5_b_im': rn((L, S5_GROUPS, S5_P, S5_CH), (2.0 * S5_CH) ** -0.5),
        's5_c_re': rn((L, S5_GROUPS, S5_CH, S5_P), (2.0 * S5_P) ** -0.5),
        's5_c_im': rn((L, S5_GROUPS, S5_CH, S5_P), (2.0 * S5_P) ** -0.5),
        's5_d': rn((L, S5_W), 1.0),
        's5_w_glu': rn((L, S5_W, S5_W), S5_W ** -0.5),
        's5_b_glu': rn((L, S5_W), 0.02),
        'nsa_g_q': gain((L, NSA_DK)),
        'nsa_g_k': gain((L, 3, NSA_DK)),
        'nsa_pe_k': rn((L, CMP_LEN, NSA_DK), 0.02),
        'nsa_w1_k': rn((L, CMP_LEN * NSA_DK, CMP_HID), (CMP_LEN * NSA_DK) ** -0.5),
        'nsa_w2_k': rn((L, CMP_HID, NSA_DK), CMP_HID ** -0.5),
        'nsa_pe_v': rn((L, CMP_LEN, NSA_DK), 0.02),
        'nsa_w1_v': rn((L, CMP_LEN * NSA_DK, CMP_HID), (CMP_LEN * NSA_DK) ** -0.5),
        'nsa_w2_v': rn((L, CMP_HID, NSA_DK), CMP_HID ** -0.5),
        'out_norm': gain((L, N_MIXERS, GROUP_W)),
        'w_out': rn((L, D_MODEL, D_MODEL), D_MODEL ** -0.5),
        'ffn_norm': gain((L, D_MODEL)),
        'moe_w_rg': rn((L, D_MODEL, N_GROUPS), D_MODEL ** -0.5),
        'moe_b_rg': rn((L, N_GROUPS), 0.01),
        'moe_w_re': rn((L, D_MODEL, N_EXPERTS), D_MODEL ** -0.5),
        'moe_b_re': rn((L, N_EXPERTS), 0.01),
        'moe_w_gate': rn((L, N_EXPERTS, D_MODEL, D_EXPERT), D_MODEL ** -0.5),
        'moe_w_up': rn((L, N_EXPERTS, D_MODEL, D_EXPERT), D_MODEL ** -0.5),
        'moe_w_down': rn((L, N_EXPERTS, D_EXPERT, D_MODEL), D_EXPERT ** -0.5),
    }


def reference(x, positions, mix_norm, w_in, mla_g_cq, mla_g_ckv, mla_w_uq, mla_w_ukv, mla_g_q, mla_g_k,
              lru_conv_w, lru_conv_b, lru_w_a, lru_b_a, lru_w_i, lru_b_i, lru_lambda,
              s5_a_re, s5_a_im, s5_log_dt, s5_b_re, s5_b_im, s5_c_re, s5_c_im, s5_d, s5_w_glu, s5_b_glu,
              nsa_g_q, nsa_g_k, nsa_pe_k, nsa_w1_k, nsa_w2_k, nsa_pe_v, nsa_w1_v, nsa_w2_v,
              out_norm, w_out, ffn_norm, moe_w_rg, moe_b_rg, moe_w_re, moe_b_re,
              moe_w_gate, moe_w_up, moe_w_down):
    for l in range(DEPTH):
        h = rms_norm(x, mix_norm[l])
        c_q, c_kv, k_pe, lru_x, lru_gate, s5_u, nsa_q, nsa_kv, nsa_gate = jnp.split(
            h @ w_in[l], IN_SPLITS, axis=-1)
        y_a = mla(c_q, c_kv, k_pe, positions, mla_g_cq[l], mla_g_ckv[l], mla_w_uq[l], mla_w_ukv[l],
                  mla_g_q[l], mla_g_k[l])
        y_b = rglru(lru_x, lru_gate, lru_conv_w[l], lru_conv_b[l], lru_w_a[l], lru_b_a[l],
                    lru_w_i[l], lru_b_i[l], lru_lambda[l])
        y_c = s5(s5_u, s5_a_re[l], s5_a_im[l], s5_log_dt[l], s5_b_re[l], s5_b_im[l], s5_c_re[l],
                 s5_c_im[l], s5_d[l], s5_w_glu[l], s5_b_glu[l])
        y_d = nsa(nsa_q, nsa_kv, nsa_gate, positions, nsa_g_q[l], nsa_g_k[l], nsa_pe_k[l], nsa_w1_k[l],
                  nsa_w2_k[l], nsa_pe_v[l], nsa_w1_v[l], nsa_w2_v[l])
        y = jnp.stack([y_a, y_b, y_c, y_d], axis=2)
        y = rms_norm(y, out_norm[l]).reshape(x.shape)
        x = x + y @ w_out[l]
        x = x + hier_moe(rms_norm(x, ffn_norm[l]), moe_w_rg[l], moe_b_rg[l], moe_w_re[l], moe_b_re[l],
                         moe_w_gate[l], moe_w_up[l], moe_w_down[l])
    return x
```

```python
import functools
import math

import numpy as np
import jax
import jax.numpy as jnp
from jax import lax
from jax.experimental import pallas as pl
from jax.experimental.pallas import tpu as pltpu

F32 = jnp.float32
BF16 = jnp.bfloat16

D_MODEL = 1024
DEPTH = 2
GROUP_W = 256
EPS = 1e-6
ROPE_THETA = 500000.0
NEG = -1e30

MLA_HEADS = 4
MLA_ROPE = 32
MLA_NOPE = 64
MLA_V = 64
MLA_QK = 96
MLA_Q_RANK = 192
MLA_KV_RANK = 128

LRU_W = 256
LRU_BLOCKS = 4
LRU_BW = 64
CONV_W = 4
LRU_C = 8.0

S5_W = 256
S5_CH = 16
S5_GROUPS = 16
S5_P = 64
S5_STATE = S5_GROUPS * S5_P

NSA_HEADS = 4
NSA_DK = 64
NSA_ROT = 16
CMP_LEN = 32
CMP_STRIDE = 16
CMP_HID = 128
SEL_LEN = 64
SEL_TOPK = 5
WIN = 512

N_GROUPS = 4
EXP_PER_GROUP = 4
N_EXPERTS = 16
D_EXPERT = 256

D_IN = 1772

LANES = 128
VMEM_LIMIT = 48 * 1024 * 1024

MLA_IN_W = 512
LRU_IN_W = 512
S5_IN_W = 256
NQ_IN_W = NSA_HEADS * LANES
NKV_IN_W = 6 * LANES
IN_W = MLA_IN_W + LRU_IN_W + S5_IN_W + NQ_IN_W + NKV_IN_W


def _cparams(sem):
    return pltpu.CompilerParams(dimension_semantics=sem, vmem_limit_bytes=VMEM_LIMIT)


def _dot(a, b):
    return jnp.dot(a.astype(BF16), b.astype(BF16), preferred_element_type=F32)


def _dot_nt(a, b):
    return lax.dot_general(a.astype(BF16), b.astype(BF16), (((1,), (1,)), ((), ())),
                           preferred_element_type=F32)


def _split3(x):
    hi = x.astype(BF16)
    r = x - hi.astype(F32)
    mid = r.astype(BF16)
    lo = (r - mid.astype(F32)).astype(BF16)
    return hi, mid, lo


def _dot_f32_by_exact(x, w_bf16):
    hi, mid, lo = _split3(x)
    return (jnp.dot(hi, w_bf16, preferred_element_type=F32)
            + jnp.dot(mid, w_bf16, preferred_element_type=F32)
            + jnp.dot(lo, w_bf16, preferred_element_type=F32))


def _rms(x, g, n):
    return x * lax.rsqrt(jnp.sum(x * x, axis=-1, keepdims=True) * (1.0 / n) + EPS) * g


def _gelu(x):
    return 0.5 * x * (1.0 + jnp.tanh(math.sqrt(2.0 / math.pi) * (x + 0.044715 * (x * x * x))))


def _rope(x, cos, sin_lo, sin_hi, half):
    return (x * cos + pltpu.roll(x, LANES - half, axis=1) * sin_lo
            + pltpu.roll(x, half, axis=1) * sin_hi)


def _rope_tables(pos_f32, inv, m_lo, m_hi):
    ang = pos_f32 * inv
    s = jnp.sin(ang)
    return jnp.cos(ang), s * m_lo, s * m_hi


def _rope_consts(rot_dim):
    half = rot_dim // 2
    inv = ROPE_THETA ** (-jnp.arange(half, dtype=F32) * 2.0 / rot_dim)
    z = jnp.zeros((LANES - rot_dim,), F32)
    inv_l = jnp.concatenate([inv, inv, z])[None, :]
    m_lo = jnp.concatenate([-jnp.ones((half,), F32), jnp.zeros((LANES - half,), F32)])[None, :]
    m_hi = jnp.concatenate([jnp.zeros((half,), F32), jnp.ones((half,), F32), z])[None, :]
    return inv_l, m_lo, m_hi


def _pad_last(a, n):
    return jnp.pad(a, [(0, 0)] * (a.ndim - 1) + [(0, n - a.shape[-1])])


def _full(shape):
    nd = len(shape)
    return pl.BlockSpec(shape, lambda *_: (0,) * nd)


def _inproj_kernel(x_ref, g_ref, w_ref, o_mla, o_lru, o_s5, o_nq, o_nkv):
    x = x_ref[...]
    h = x * lax.rsqrt(jnp.mean(x * x, axis=-1, keepdims=True) + EPS) * g_ref[...]
    y = jnp.dot(h.astype(BF16), w_ref[...], preferred_element_type=F32)
    c0 = 0
    for o in (o_mla, o_lru, o_s5, o_nq, o_nkv):
        w = o.shape[-1]
        o[...] = y[:, c0:c0 + w]
        c0 += w


def _inproj_cols():
    src = -np.ones((IN_W,), np.int64)
    o = 0
    src[o:o + 192] = np.arange(0, 192)
    src[o + 256:o + 384] = np.arange(192, 320)
    src[o + 384:o + 416] = np.arange(320, 352)
    o += MLA_IN_W
    src[o:o + 512] = np.arange(352, 864)
    o += LRU_IN_W
    src[o:o + 256] = np.arange(864, 1120)
    o += S5_IN_W
    for h in range(NSA_HEADS):
        src[o + LANES * h:o + LANES * h + 64] = np.arange(1120 + 64 * h, 1120 + 64 * h + 64)
    o += NQ_IN_W
    kv0 = 1376
    src[o:o + 64] = np.arange(kv0, kv0 + 64)
    src[o + 128:o + 192] = np.arange(kv0 + 128, kv0 + 192)
    src[o + 256:o + 320] = np.arange(kv0 + 256, kv0 + 320)
    src[o + 384:o + 448] = np.arange(kv0 + 64, kv0 + 128)
    src[o + 448:o + 512] = np.arange(kv0 + 192, kv0 + 256)
    src[o + 512:o + 576] = np.arange(kv0 + 320, kv0 + 384)
    src[o + 640:o + 652] = np.arange(1760, 1772)
    return src


_INPROJ_SRC = _inproj_cols()


def _prep_w_in(w_in):
    idx = jnp.asarray(np.maximum(_INPROJ_SRC, 0), jnp.int32)
    keep = jnp.asarray(_INPROJ_SRC >= 0)
    return jnp.where(keep[None, :], jnp.take(w_in, idx, axis=1), 0.0).astype(BF16)


def _inproj(x2, g, w_pad, tm=512):
    t = x2.shape[0]
    widths = (MLA_IN_W, LRU_IN_W, S5_IN_W, NQ_IN_W, NKV_IN_W)
    return pl.pallas_call(
        _inproj_kernel,
        grid=(t // tm,),
        in_specs=[pl.BlockSpec((tm, D_MODEL), lambda i: (i, 0)),
                  _full((1, D_MODEL)),
                  _full((D_MODEL, IN_W))],
        out_specs=[pl.BlockSpec((tm, w), lambda i: (i, 0)) for w in widths],
        out_shape=[jax.ShapeDtypeStruct((t, w), F32) for w in widths],
        compiler_params=_cparams(("parallel",)),
        name="inproj",
    )(x2, g[None, :], w_pad)


def _mla_prep_kernel(in_ref, pos_ref, gcq_ref, wuq_ref, gckv_ref, wk_ref, wv_ref, gq_ref, gk_ref,
                     inv_ref, mlo_ref, mhi_ref, q_out, k_out, v_out):
    xin = in_ref[0]
    cq = _rms(xin[:, 0:256], gcq_ref[...], MLA_Q_RANK)
    ckv = _rms(xin[:, 256:384], gckv_ref[...], MLA_KV_RANK)
    kpe = xin[:, 384:512]
    q = _dot(cq, wuq_ref[...])
    kn = _dot(ckv, wk_ref[...])
    v = _dot(ckv, wv_ref[...])
    cos, s_lo, s_hi = _rope_tables(pos_ref[0].astype(F32), inv_ref[...], mlo_ref[...], mhi_ref[...])
    scale = MLA_QK ** -0.5
    for h in range(MLA_HEADS):
        qh = _rms(q[:, LANES * h:LANES * (h + 1)], gq_ref[...], MLA_QK)
        q_out[0, h] = (_rope(qh, cos, s_lo, s_hi, MLA_ROPE // 2) * scale).astype(BF16)
        kh = _rms(kn[:, LANES * h:LANES * (h + 1)] + kpe, gk_ref[...], MLA_QK)
        k_out[0, h] = _rope(kh, cos, s_lo, s_hi, MLA_ROPE // 2).astype(BF16)
        v_out[0, h] = v[:, MLA_V * h:MLA_V * (h + 1)].astype(BF16)


def _flash_update(s, valid, v, m, l, acc):
    if valid is not None:
        s = jnp.where(valid, s, NEG)
    m_new = jnp.maximum(m, jnp.max(s, axis=-1, keepdims=True))
    alpha = jnp.exp(m - m_new)
    p = jnp.exp(s - m_new)
    if valid is not None:
        p = jnp.where(valid, p, 0.0)
    l = alpha * l + jnp.sum(p, axis=-1, keepdims=True)
    pv = jnp.dot(p.astype(BF16).reshape(-1, p.shape[-1]), v, preferred_element_type=F32)
    acc = alpha * acc + pv.reshape(acc.shape)
    return m_new, l, acc


def _mla_attn_kernel(q_ref, k_ref, v_ref, o_ref, *, tq):
    qi = pl.program_id(1)
    row = lax.broadcasted_iota(jnp.int32, (tq, tq), 0)
    col = lax.broadcasted_iota(jnp.int32, (tq, tq), 1)
    diag = col <= row
    for h in range(MLA_HEADS):
        q = q_ref[0, h]

        def body(kj, carry, q=q, h=h):
            k0 = pl.multiple_of(kj * tq, tq)
            s = _dot_nt(q, k_ref[0, h, pl.ds(k0, tq), :])
            return _flash_update(s, None, v_ref[0, h, pl.ds(k0, tq), :], *carry)

        init = (jnp.full((tq, 1), NEG, F32), jnp.zeros((tq, 1), F32), jnp.zeros((tq, MLA_V), F32))
        carry = lax.fori_loop(0, qi, body, init)
        k0 = pl.multiple_of(qi * tq, tq)
        s = _dot_nt(q, k_ref[0, h, pl.ds(k0, tq), :])
        _, l, acc = _flash_update(s, diag, v_ref[0, h, pl.ds(k0, tq), :], *carry)
        o_ref[0, :, MLA_V * h:MLA_V * (h + 1)] = acc / l


def _mla(mla_in, pos3, p, ts=512, tq=256):
    b, s, _ = mla_in.shape
    inv, m_lo, m_hi = _rope_consts(MLA_ROPE)
    hq = (b, MLA_HEADS, s, LANES)
    q, k, v = pl.pallas_call(
        _mla_prep_kernel,
        grid=(b, s // ts),
        in_specs=[pl.BlockSpec((1, ts, MLA_IN_W), lambda i, j: (i, j, 0)),
                  pl.BlockSpec((1, ts, 1), lambda i, j: (i, j, 0)),
                  _full((1, 256)), _full((256, 512)), _full((1, 128)), _full((128, 512)),
                  _full((128, 256)), _full((1, 128)), _full((1, 128)),
                  _full((1, 128)), _full((1, 128)), _full((1, 128))],
        out_specs=[pl.BlockSpec((1, MLA_HEADS, ts, LANES), lambda i, j: (i, 0, j, 0)),
                   pl.BlockSpec((1, MLA_HEADS, ts, LANES), lambda i, j: (i, 0, j, 0)),
                   pl.BlockSpec((1, MLA_HEADS, ts, MLA_V), lambda i, j: (i, 0, j, 0))],
        out_shape=[jax.ShapeDtypeStruct(hq, BF16), jax.ShapeDtypeStruct(hq, BF16),
                   jax.ShapeDtypeStruct((b, MLA_HEADS, s, MLA_V), BF16)],
        compiler_params=_cparams(("parallel", "parallel")),
        name="mla_prep",
    )(mla_in, pos3, p["g_cq"], p["w_uq"], p["g_ckv"], p["w_k"], p["w_v"], p["g_q"], p["g_k"],
      inv, m_lo, m_hi)
    return pl.pallas_call(
        functools.partial(_mla_attn_kernel, tq=tq),
        grid=(b, s // tq),
        in_specs=[pl.BlockSpec((1, MLA_HEADS, tq, LANES), lambda i, j: (i, 0, j, 0)),
                  pl.BlockSpec((1, MLA_HEADS, s, LANES), lambda i, j: (i, 0, 0, 0)),
                  pl.BlockSpec((1, MLA_HEADS, s, MLA_V), lambda i, j: (i, 0, 0, 0))],
        out_specs=pl.BlockSpec((1, tq, GROUP_W), lambda i, j: (i, j, 0)),
        out_shape=jax.ShapeDtypeStruct((b, s, GROUP_W), F32),
        compiler_params=_cparams(("parallel", "arbitrary")),
        name="mla_attn",
    )(q, k, v)


def _prep_mla(g_cq, g_ckv, w_uq, w_ukv, g_q, g_k):
    wq = w_uq.reshape(MLA_Q_RANK, MLA_HEADS, MLA_QK)
    wq = _pad_last(wq, LANES).reshape(MLA_Q_RANK, MLA_HEADS * LANES)
    wq = jnp.pad(wq, ((0, 256 - MLA_Q_RANK), (0, 0)))
    wkv = w_ukv.reshape(MLA_KV_RANK, MLA_HEADS, MLA_NOPE + MLA_V)
    wk = jnp.pad(wkv[:, :, :MLA_NOPE], ((0, 0), (0, 0), (MLA_ROPE, LANES - MLA_QK)))
    wk = wk.reshape(MLA_KV_RANK, MLA_HEADS * LANES)
    wv = wkv[:, :, MLA_NOPE:].reshape(MLA_KV_RANK, MLA_HEADS * MLA_V)
    return dict(g_cq=_pad_last(g_cq, 256)[None, :], g_ckv=g_ckv[None, :], w_uq=wq.astype(BF16),
                w_k=wk.astype(BF16), w_v=wv.astype(BF16), g_q=_pad_last(g_q, LANES)[None, :],
                g_k=_pad_last(g_k, LANES)[None, :])


def _shift_rows(x, k, row, fill):
    return jnp.where(row >= k, pltpu.roll(x, k, axis=0), fill)


def _lru_kernel(in_ref, cw_ref, cb_ref, wa_ref, ba_ref, wi_ref, bi_ref, lam_ref, o_ref):
    xin = in_ref[0]
    s = xin.shape[0]
    xb = xin[:, :LRU_W]
    gb = xin[:, LRU_W:]
    row = lax.broadcasted_iota(jnp.int32, (s, LRU_W), 0)
    u = cb_ref[...] + xb * cw_ref[CONV_W - 1:CONV_W, :]
    for j in range(CONV_W - 1):
        u = u + _shift_rows(xb, CONV_W - 1 - j, row, 0.0) * cw_ref[j:j + 1, :]
    r = jax.nn.sigmoid(_dot(u, wa_ref[...]) + ba_ref[...])
    gi = jax.nn.sigmoid(_dot(u, wi_ref[...]) + bi_ref[...])
    nlam = -lam_ref[...]
    softplus = jnp.maximum(nlam, 0.0) + jnp.log1p(jnp.exp(-jnp.abs(nlam)))
    log_a = (-LRU_C) * r * softplus
    a = jnp.exp(log_a)
    mult = jnp.sqrt(jnp.maximum(-jnp.tanh(log_a) * (a * a + 1.0), 0.0))
    mult = jnp.where(row == 0, 1.0, mult)
    bt = mult * gi * u
    k = 1
    while k < s:
        bt = a * _shift_rows(bt, k, row, 0.0) + bt
        a = a * _shift_rows(a, k, row, 1.0)
        k *= 2
    o_ref[0] = bt * _gelu(gb)


def _block_diag(w):
    n, i, j = w.shape
    eye = jnp.eye(n, dtype=w.dtype)
    return (eye[:, None, :, None] * w[:, :, None, :]).reshape(n * i, n * j)


def _lru(lru_in, p):
    b, s, _ = lru_in.shape
    return pl.pallas_call(
        _lru_kernel,
        grid=(b,),
        in_specs=[pl.BlockSpec((1, s, LRU_IN_W), lambda i: (i, 0, 0)),
                  _full((CONV_W, LRU_W)), _full((1, LRU_W)), _full((LRU_W, LRU_W)), _full((1, LRU_W)),
                  _full((LRU_W, LRU_W)), _full((1, LRU_W)), _full((1, LRU_W))],
        out_specs=pl.BlockSpec((1, s, LRU_W), lambda i: (i, 0, 0)),
        out_shape=jax.ShapeDtypeStruct((b, s, LRU_W), F32),
        compiler_params=_cparams(("parallel",)),
        name="rglru",
    )(lru_in, p["cw"], p["cb"], p["wa"], p["ba"], p["wi"], p["bi"], p["lam"])


def _prep_lru(conv_w, conv_b, w_a, b_a, w_i, b_i, lam):
    return dict(cw=conv_w, cb=conv_b[None, :], wa=_block_diag(w_a).astype(BF16),
                ba=b_a.reshape(1, LRU_W), wi=_block_diag(w_i).astype(BF16), bi=b_i.reshape(1, LRU_W),
                lam=lam[None, :])


def _s5_disc_kernel(are_ref, aim_ref, ldt_ref, arer_ref, aimr_ref, bre_ref, bim_ref,
                    abre_ref, abim_ref, bbre_ref, bbim_ref):
    dt = jnp.exp(ldt_ref[...])

    def disc(a_re, a_im):
        mag = jnp.exp(dt * a_re)
        ab_re = mag * jnp.cos(dt * a_im)
        ab_im = mag * jnp.sin(dt * a_im)
        den = a_re * a_re + a_im * a_im
        n_re = ab_re - 1.0
        g_re = (n_re * a_re + ab_im * a_im) / den
        g_im = (ab_im * a_re - n_re * a_im) / den
        return ab_re, ab_im, g_re, g_im

    ab_re, ab_im, _, _ = disc(are_ref[...], aim_ref[...])
    abre_ref[...] = ab_re
    abim_ref[...] = ab_im
    _, _, g_re, g_im = disc(arer_ref[...], aimr_ref[...])
    bbre_ref[...] = g_re * bre_ref[...] - g_im * bim_ref[...]
    bbim_ref[...] = g_re * bim_ref[...] + g_im * bre_ref[...]


def _s5_kernel(u_ref, are_ref, aim_ref, bre_ref, bim_ref, cre_ref, cim_ref, d_ref, wg_ref, bg_ref,
               o_ref, hre, him, st_re, st_im, *, tc, nb, cw):
    @pl.when(pl.program_id(0) == 0)
    def _():
        st_re[...] = jnp.zeros_like(st_re)
        st_im[...] = jnp.zeros_like(st_im)

    u = u_ref[...].reshape(tc * nb, S5_W)
    ub = u.astype(BF16)
    hre[...] = jnp.dot(ub, bre_ref[...], preferred_element_type=F32)
    him[...] = jnp.dot(ub, bim_ref[...], preferred_element_type=F32)
    for c in range(S5_STATE // cw):
        cs = slice(c * cw, (c + 1) * cw)
        ar = jnp.broadcast_to(are_ref[:, cs], (nb, cw))
        ai = jnp.broadcast_to(aim_ref[:, cs], (nb, cw))

        def body(t, carry, cs=cs, ar=ar, ai=ai):
            hr, hi = carry
            r0 = pl.multiple_of(t * nb, nb)
            nr = ar * hr - ai * hi + hre[pl.ds(r0, nb), cs]
            ni = ar * hi + ai * hr + him[pl.ds(r0, nb), cs]
            hre[pl.ds(r0, nb), cs] = nr
            him[pl.ds(r0, nb), cs] = ni
            return nr, ni

        hr, hi = lax.fori_loop(0, tc, body, (st_re[:, cs], st_im[:, cs]), unroll=4)
        st_re[:, cs] = hr
        st_im[:, cs] = hi
    y = (jnp.dot(hre[...].astype(BF16), cre_ref[...], preferred_element_type=F32)
         - jnp.dot(him[...].astype(BF16), cim_ref[...], preferred_element_type=F32))
    y = _gelu(y + d_ref[...] * u)
    z = _dot(y, wg_ref[...]) + bg_ref[...]
    o_ref[...] = (y * jax.nn.sigmoid(z)).reshape(tc, nb, S5_W)


def _prep_s5(a_re, a_im, log_dt, b_re, b_im, c_re, c_im, d, w_glu, b_glu):
    g, p, ch = S5_GROUPS, S5_P, S5_CH
    ab_re, ab_im, bb_re, bb_im = pl.pallas_call(
        _s5_disc_kernel,
        out_shape=[jax.ShapeDtypeStruct((g, p), F32), jax.ShapeDtypeStruct((g, p), F32),
                   jax.ShapeDtypeStruct((g, p * ch), F32), jax.ShapeDtypeStruct((g, p * ch), F32)],
        name="s5_discretize",
    )(a_re, a_im, log_dt[:, None], jnp.repeat(a_re, ch, axis=1), jnp.repeat(a_im, ch, axis=1),
      b_re.reshape(g, p * ch), b_im.reshape(g, p * ch))

    def b_dense(bb):
        return _block_diag(bb.reshape(g, p, ch).transpose(0, 2, 1)).astype(BF16)

    def c_dense(c):
        return _block_diag(c.transpose(0, 2, 1)).astype(BF16)

    return dict(a_re=ab_re.reshape(1, S5_STATE), a_im=ab_im.reshape(1, S5_STATE),
                b_re=b_dense(bb_re), b_im=b_dense(bb_im), c_re=c_dense(c_re), c_im=c_dense(c_im),
                d=d[None, :], w_glu=w_glu.astype(BF16), b_glu=b_glu[None, :])


def _s5(u_tm, p, tc=64, cw=256):
    s, nb, _ = u_tm.shape
    return pl.pallas_call(
        functools.partial(_s5_kernel, tc=tc, nb=nb, cw=cw),
        grid=(s // tc,),
        in_specs=[pl.BlockSpec((tc, nb, S5_W), lambda i: (i, 0, 0)),
                  _full((1, S5_STATE)), _full((1, S5_STATE)),
                  _full((S5_W, S5_STATE)), _full((S5_W, S5_STATE)),
                  _full((S5_STATE, S5_W)), _full((S5_STATE, S5_W)),
                  _full((1, S5_W)), _full((S5_W, S5_W)), _full((1, S5_W))],
        out_specs=pl.BlockSpec((tc, nb, S5_W), lambda i: (i, 0, 0)),
        out_shape=jax.ShapeDtypeStruct((s, nb, S5_W), F32),
        scratch_shapes=[pltpu.VMEM((tc * nb, S5_STATE), F32), pltpu.VMEM((tc * nb, S5_STATE), F32),
                        pltpu.VMEM((nb, S5_STATE), F32), pltpu.VMEM((nb, S5_STATE), F32)],
        compiler_params=_cparams(("arbitrary",)),
        name="s5",
    )(u_tm, p["a_re"], p["a_im"], p["b_re"], p["b_im"], p["c_re"], p["c_im"], p["d"], p["w_glu"],
      p["b_glu"])


def _nsa_prep_kernel(nq_ref, nkv_ref, pos_ref, gq_ref, gk_ref, inv_ref, mlo_ref, mhi_ref,
                     q_out, ks_out, kw_out, vs_out, vw_out):
    cos, s_lo, s_hi = _rope_tables(pos_ref[0].astype(F32), inv_ref[...], mlo_ref[...], mhi_ref[...])
    half = NSA_ROT // 2
    nq = nq_ref[0]
    scale = NSA_DK ** -0.5
    for h in range(NSA_HEADS):
        qh = _rms(nq[:, LANES * h:LANES * (h + 1)], gq_ref[...], NSA_DK)
        q_out[0, h] = (_rope(qh, cos, s_lo, s_hi, half) * scale).astype(BF16)
    nkv = nkv_ref[0]
    ks = _rms(nkv[:, 128:256], gk_ref[1:2, :], NSA_DK)
    ks_out[0] = _rope(ks, cos, s_lo, s_hi, half).astype(BF16)
    kw = _rms(nkv[:, 256:384], gk_ref[2:3, :], NSA_DK)
    kw_out[0] = _rope(kw, cos, s_lo, s_hi, half).astype(BF16)
    vs_out[0] = nkv[:, 448:512].astype(BF16)
    vw_out[0] = nkv[:, 512:576].astype(BF16)


def _nsa_cmp_kernel(gk_in, gv_in, posc_ref, pek_ref, w1k_ref, w2k_ref, pev_ref, w1v_ref, w2v_ref,
                    g_ref, inv_ref, mlo_ref, mhi_ref, kc_out, vc_out):
    half_w = CMP_STRIDE * NSA_DK
    nc = gk_in.shape[1]

    def compress(g, pe_ref, w1_ref, w2_ref):
        lo = _dot(g + pe_ref[:, :half_w], w1_ref[:half_w, :])
        hi = _dot(g + pe_ref[:, half_w:], w1_ref[half_w:, :])
        hid = lo + pltpu.roll(hi, nc - 1, axis=0)
        return _dot(_gelu(hid), w2_ref[...])

    kc = compress(gk_in[0], pek_ref, w1k_ref, w2k_ref)
    cos, s_lo, s_hi = _rope_tables(posc_ref[0].astype(F32), inv_ref[...], mlo_ref[...], mhi_ref[...])
    kc = _rope(_rms(kc, g_ref[0:1, :], NSA_DK), cos, s_lo, s_hi, NSA_ROT // 2)
    kc_out[0] = kc.astype(BF16)
    vc_out[0] = compress(gv_in[0], pev_ref, w1v_ref, w2v_ref).astype(BF16)


def _nsa_attn_kernel(q_ref, kc_ref, vc_ref, ks_ref, vs_ref, kw_ref, vw_ref, gate_ref, ov_ref, ex_ref,
                     o_ref, msk, *, tq):
    qi = pl.program_id(1)
    nh = NSA_HEADS
    nkt = msk.shape[0]
    q = q_ref[0].reshape(nh * tq, LANES)
    lane = lax.broadcasted_iota(jnp.int32, (tq, LANES), 1)
    qpos = qi * tq + lax.broadcasted_iota(jnp.int32, (tq, LANES), 0)

    s = _dot_nt(q, kc_ref[0]).reshape(nh, tq, LANES)
    valid = (lane * CMP_STRIDE + (CMP_LEN - 1)) <= qpos
    s = jnp.where(valid, s, NEG)
    e = jnp.exp(s - jnp.max(s, axis=-1, keepdims=True))
    p_c = jnp.where(valid, e / jnp.sum(e, axis=-1, keepdims=True), 0.0)
    o_cmp = jnp.dot(p_c.astype(BF16).reshape(nh * tq, LANES), vc_ref[0],
                    preferred_element_type=F32).reshape(nh, tq, NSA_DK)

    imp = _dot_f32_by_exact(jnp.sum(p_c, axis=0), ov_ref[...])
    cur = qpos // SEL_LEN
    lane_f = lane.astype(F32)
    forced = (lane == 0) | (lane == cur) | (lane == cur - 1)
    cand = (lane < cur - 1) & (lane > 0)
    sel = forced
    for _ in range(SEL_TOPK - 3):
        sc = jnp.where(cand, imp, -jnp.inf)
        best = jnp.max(sc, axis=-1, keepdims=True)
        pick = jnp.min(jnp.where(cand & (sc == best), lane_f, float(LANES)), axis=-1, keepdims=True)
        hit = lane_f == pick
        sel = sel | hit
        cand = cand & jnp.logical_not(hit)
    em = jnp.dot(jnp.where(sel, 1.0, 0.0).astype(BF16), ex_ref[...], preferred_element_type=F32)
    for t in range(nkt):
        msk[t] = em[:, tq * t:tq * (t + 1)]

    kcol = lax.broadcasted_iota(jnp.int32, (tq, tq), 1)
    qrow = qi * tq + lax.broadcasted_iota(jnp.int32, (tq, tq), 0)
    init = (jnp.full((nh, tq, 1), NEG, F32), jnp.zeros((nh, tq, 1), F32),
            jnp.zeros((nh, tq, NSA_DK), F32))

    def slc_body(kt, carry):
        k0 = pl.multiple_of(kt * tq, tq)
        s = _dot_nt(q, ks_ref[0, pl.ds(k0, tq), :]).reshape(nh, tq, tq)
        ok = (msk[kt] > 0.5) & (kt * tq + kcol <= qrow)
        return _flash_update(s, ok, vs_ref[0, pl.ds(k0, tq), :], *carry)

    _, l, acc = lax.fori_loop(0, qi + 1, slc_body, init)
    o_slc = acc / l

    def win_body(kt, carry):
        k0 = pl.multiple_of(kt * tq, tq)
        s = _dot_nt(q, kw_ref[0, pl.ds(k0, tq), :]).reshape(nh, tq, tq)
        kpos = kt * tq + kcol
        ok = (kpos <= qrow) & (qrow - kpos < WIN)
        return _flash_update(s, ok, vw_ref[0, pl.ds(k0, tq), :], *carry)

    _, l, acc = lax.fori_loop(jnp.maximum(qi - WIN // tq, 0), qi + 1, win_body, init)
    o_win = acc / l

    g = jax.nn.sigmoid(gate_ref[0])
    for h in range(nh):
        o_ref[0, :, NSA_DK * h:NSA_DK * (h + 1)] = (g[:, 3 * h:3 * h + 1] * o_cmp[h]
                                                    + g[:, 3 * h + 1:3 * h + 2] * o_slc[h]
                                                    + g[:, 3 * h + 2:3 * h + 3] * o_win[h])


def _nsa_tables(s, tq):
    nc = s // CMP_STRIDE
    nsb = s // SEL_LEN
    cs = np.arange(nc) * CMP_STRIDE
    ss = np.arange(nsb) * SEL_LEN
    ov = np.clip(np.minimum(cs[:, None] + CMP_LEN, ss[None, :] + SEL_LEN)
                 - np.maximum(cs[:, None], ss[None, :]), 0, None) / CMP_STRIDE
    ov[(s - CMP_LEN) // CMP_STRIDE + 1:] = 0.0
    ov_pad = np.zeros((nc, LANES), np.float32)
    ov_pad[:, :nsb] = ov
    ex = np.zeros((LANES, s), np.float32)
    ex[np.arange(s) // SEL_LEN, np.arange(s)] = 1.0
    return jnp.asarray(ov_pad, BF16), jnp.asarray(ex, BF16)


def _nsa(nq, nkv, pos3, p, ts=512, tq=128):
    b, s, _ = nq.shape
    nc = s // CMP_STRIDE
    assert nc == LANES and s // SEL_LEN <= LANES
    inv, m_lo, m_hi = _rope_consts(NSA_ROT)
    q, ks, kw, vs, vw = pl.pallas_call(
        _nsa_prep_kernel,
        grid=(b, s // ts),
        in_specs=[pl.BlockSpec((1, ts, NQ_IN_W), lambda i, j: (i, j, 0)),
                  pl.BlockSpec((1, ts, NKV_IN_W), lambda i, j: (i, j, 0)),
                  pl.BlockSpec((1, ts, 1), lambda i, j: (i, j, 0)),
                  _full((1, LANES)), _full((3, LANES)), _full((1, LANES)), _full((1, LANES)),
                  _full((1, LANES))],
        out_specs=[pl.BlockSpec((1, NSA_HEADS, ts, LANES), lambda i, j: (i, 0, j, 0)),
                   pl.BlockSpec((1, ts, LANES), lambda i, j: (i, j, 0)),
                   pl.BlockSpec((1, ts, LANES), lambda i, j: (i, j, 0)),
                   pl.BlockSpec((1, ts, NSA_DK), lambda i, j: (i, j, 0)),
                   pl.BlockSpec((1, ts, NSA_DK), lambda i, j: (i, j, 0))],
        out_shape=[jax.ShapeDtypeStruct((b, NSA_HEADS, s, LANES), BF16),
                   jax.ShapeDtypeStruct((b, s, LANES), BF16), jax.ShapeDtypeStruct((b, s, LANES), BF16),
                   jax.ShapeDtypeStruct((b, s, NSA_DK), BF16), jax.ShapeDtypeStruct((b, s, NSA_DK), BF16)],
        compiler_params=_cparams(("parallel", "parallel")),
        name="nsa_prep",
    )(nq, nkv, pos3, p["g_q"], p["g_k"], inv, m_lo, m_hi)

    gk = nkv[:, :, 0:NSA_DK].reshape(b, nc, CMP_STRIDE * NSA_DK)
    gv = nkv[:, :, 384:384 + NSA_DK].reshape(b, nc, CMP_STRIDE * NSA_DK)
    last = np.minimum(np.arange(nc) * CMP_STRIDE + CMP_LEN - 1, s - 1)
    posc = pos3[:, last, :]
    gw = CMP_STRIDE * NSA_DK
    kc, vc = pl.pallas_call(
        _nsa_cmp_kernel,
        grid=(b,),
        in_specs=[pl.BlockSpec((1, nc, gw), lambda i: (i, 0, 0)),
                  pl.BlockSpec((1, nc, gw), lambda i: (i, 0, 0)),
                  pl.BlockSpec((1, nc, 1), lambda i: (i, 0, 0)),
                  _full((1, 2 * gw)), _full((2 * gw, CMP_HID)), _full((CMP_HID, LANES)),
                  _full((1, 2 * gw)), _full((2 * gw, CMP_HID)), _full((CMP_HID, NSA_DK)),
                  _full((3, LANES)), _full((1, LANES)), _full((1, LANES)), _full((1, LANES))],
        out_specs=[pl.BlockSpec((1, nc, LANES), lambda i: (i, 0, 0)),
                   pl.BlockSpec((1, nc, NSA_DK), lambda i: (i, 0, 0))],
        out_shape=[jax.ShapeDtypeStruct((b, nc, LANES), BF16),
                   jax.ShapeDtypeStruct((b, nc, NSA_DK), BF16)],
        compiler_params=_cparams(("parallel",)),
        name="nsa_compress",
    )(gk, gv, posc, p["pe_k"], p["w1_k"], p["w2_k"], p["pe_v"], p["w1_v"], p["w2_v"], p["g_k"],
      inv, m_lo, m_hi)

    ov, ex = _nsa_tables(s, tq)
    return pl.pallas_call(
        functools.partial(_nsa_attn_kernel, tq=tq),
        grid=(b, s // tq),
        in_specs=[pl.BlockSpec((1, NSA_HEADS, tq, LANES), lambda i, j: (i, 0, j, 0)),
                  pl.BlockSpec((1, nc, LANES), lambda i, j: (i, 0, 0)),
                  pl.BlockSpec((1, nc, NSA_DK), lambda i, j: (i, 0, 0)),
                  pl.BlockSpec((1, s, LANES), lambda i, j: (i, 0, 0)),
                  pl.BlockSpec((1, s, NSA_DK), lambda i, j: (i, 0, 0)),
                  pl.BlockSpec((1, s, LANES), lambda i, j: (i, 0, 0)),
                  pl.BlockSpec((1, s, NSA_DK), lambda i, j: (i, 0, 0)),
                  pl.BlockSpec((1, tq, LANES), lambda i, j: (i, j, NKV_IN_W // LANES - 1)),
                  _full((nc, LANES)), _full((LANES, s))],
        out_specs=pl.BlockSpec((1, tq, GROUP_W), lambda i, j: (i, j, 0)),
        out_shape=jax.ShapeDtypeStruct((b, s, GROUP_W), F32),
        scratch_shapes=[pltpu.VMEM((s // tq, tq, tq), F32)],
        compiler_params=_cparams(("parallel", "arbitrary")),
        name="nsa_attn",
    )(q, kc, vc, ks, vs, kw, vw, nkv, ov, ex)


def _prep_nsa(g_q, g_k, pe_k, w1_k, w2_k, pe_v, w1_v, w2_v):
    return dict(g_q=_pad_last(g_q, LANES)[None, :], g_k=_pad_last(g_k, LANES),
                pe_k=pe_k.reshape(1, CMP_LEN * NSA_DK), w1_k=w1_k.astype(BF16),
                w2_k=_pad_last(w2_k, LANES).astype(BF16),
                pe_v=pe_v.reshape(1, CMP_LEN * NSA_DK), w1_v=w1_v.astype(BF16),
                w2_v=w2_v.astype(BF16))


def _outproj_kernel(ya_ref, yb_ref, yc_ref, yd_ref, x_ref, g_ref, w_ref, o_ref):
    acc = x_ref[...]
    for i, y_ref in enumerate((ya_ref, yb_ref, yc_ref, yd_ref)):
        y = _rms(y_ref[...], g_ref[i:i + 1, :], GROUP_W)
        acc = acc + _dot(y, w_ref[GROUP_W * i:GROUP_W * (i + 1), :])
    o_ref[...] = acc


def _outproj(ys, x2, g, w, tm=512):
    t = x2.shape[0]
    yspec = pl.BlockSpec((tm, GROUP_W), lambda i: (i, 0))
    return pl.pallas_call(
        _outproj_kernel,
        grid=(t // tm,),
        in_specs=[yspec, yspec, yspec, yspec, pl.BlockSpec((tm, D_MODEL), lambda i: (i, 0)),
                  _full((4, GROUP_W)), _full((D_MODEL, D_MODEL))],
        out_specs=pl.BlockSpec((tm, D_MODEL), lambda i: (i, 0)),
        out_shape=jax.ShapeDtypeStruct((t, D_MODEL), F32),
        compiler_params=_cparams(("parallel",)),
        name="outproj",
    )(*ys, x2, g, w)


def _route(logits):
    tm = logits.shape[0]
    lane = lax.broadcasted_iota(jnp.int32, (tm, LANES), 1).astype(F32)
    far = float(LANES)
    is_g = lane < N_GROUPS
    lg = jnp.where(is_g, logits, -jnp.inf)
    gmax = jnp.max(lg, axis=-1, keepdims=True)
    gi = jnp.min(jnp.where(is_g & (lg == gmax), lane, far), axis=-1, keepdims=True)
    pg_top = 1.0 / jnp.sum(jnp.exp(lg - gmax), axis=-1, keepdims=True)
    e_id = lane - N_GROUPS
    in_g = (e_id >= gi * EXP_PER_GROUP) & (e_id < (gi + 1.0) * EXP_PER_GROUP)
    le = jnp.where(in_g, logits, -jnp.inf)
    m1 = jnp.max(le, axis=-1, keepdims=True)
    i1 = jnp.min(jnp.where(in_g & (le == m1), lane, far), axis=-1, keepdims=True)
    le2 = jnp.where(lane == i1, -jnp.inf, le)
    m2 = jnp.max(le2, axis=-1, keepdims=True)
    i2 = jnp.min(jnp.where(le2 == m2, lane, far), axis=-1, keepdims=True)
    v2 = jnp.exp(m2 - m1)
    w1 = pg_top / (1.0 + v2)
    w2 = pg_top * v2 / (1.0 + v2)
    comb = jnp.where(lane == i1, w1, 0.0) + jnp.where(lane == i2, w2, 0.0)
    return pltpu.roll(comb, LANES - N_GROUPS, axis=1)


def _moe_kernel(x_ref, g_ref, wr_hi_ref, wr_lo_ref, br_ref, ex_ref, wg_ref, wu_ref, wd_ref, o_ref,
                hn, comb):
    j = pl.program_id(1)

    @pl.when(j == 0)
    def _():
        x = x_ref[...]
        h = x * lax.rsqrt(jnp.mean(x * x, axis=-1, keepdims=True) + EPS) * g_ref[...]
        hn[...] = h.astype(BF16)
        h_hi = h.astype(BF16)
        h_lo = (h - h_hi.astype(F32)).astype(BF16)
        logits = (jnp.dot(h_hi, wr_hi_ref[...], preferred_element_type=F32)
                  + jnp.dot(h_hi, wr_lo_ref[...], preferred_element_type=F32)
                  + jnp.dot(h_lo, wr_hi_ref[...], preferred_element_type=F32)) + br_ref[...]
        comb[...] = _route(logits)
        o_ref[...] = x

    hb = hn[...]
    a = jax.nn.silu(jnp.dot(hb, wg_ref[...], preferred_element_type=F32)) \
        * jnp.dot(hb, wu_ref[...], preferred_element_type=F32)
    a = a * _dot_f32_by_exact(comb[...], ex_ref[...])
    o_ref[...] += jnp.dot(a.astype(BF16), wd_ref[...], preferred_element_type=F32)


def _moe(x2, p, tm=1024, ec=2):
    t = x2.shape[0]
    cw = ec * D_EXPERT
    return pl.pallas_call(
        _moe_kernel,
        grid=(t // tm, N_EXPERTS // ec),
        in_specs=[pl.BlockSpec((tm, D_MODEL), lambda i, j: (i, 0)),
                  _full((1, D_MODEL)), _full((D_MODEL, LANES)), _full((D_MODEL, LANES)),
                  _full((1, LANES)),
                  pl.BlockSpec((LANES, cw), lambda i, j: (0, j)),
                  pl.BlockSpec((D_MODEL, cw), lambda i, j: (0, j)),
                  pl.BlockSpec((D_MODEL, cw), lambda i, j: (0, j)),
                  pl.BlockSpec((cw, D_MODEL), lambda i, j: (j, 0))],
        out_specs=pl.BlockSpec((tm, D_MODEL), lambda i, j: (i, 0)),
        out_shape=jax.ShapeDtypeStruct((t, D_MODEL), F32),
        scratch_shapes=[pltpu.VMEM((tm, D_MODEL), BF16), pltpu.VMEM((tm, LANES), F32)],
        compiler_params=_cparams(("parallel", "arbitrary")),
        name="moe",
    )(x2, p["g"], p["wr_hi"], p["wr_lo"], p["br"], p["ex"], p["wg"], p["wu"], p["wd"])


def _prep_moe(g, w_rg, b_rg, w_re, b_re, w_gate, w_up, w_down):
    wr = _pad_last(jnp.concatenate([w_rg, w_re], axis=1), LANES)
    wr_hi = wr.astype(BF16)
    wr_lo = (wr - wr_hi.astype(F32)).astype(BF16)
    br = _pad_last(jnp.concatenate([b_rg, b_re]), LANES)[None, :]
    ex = np.zeros((LANES, N_EXPERTS * D_EXPERT), np.float32)
    ex[np.arange(N_EXPERTS * D_EXPERT) // D_EXPERT, np.arange(N_EXPERTS * D_EXPERT)] = 1.0
    cat = lambda w: w.transpose(1, 0, 2).reshape(D_MODEL, N_EXPERTS * D_EXPERT).astype(BF16)
    return dict(g=g[None, :], wr_hi=wr_hi, wr_lo=wr_lo, br=br, ex=jnp.asarray(ex, BF16),
                wg=cat(w_gate), wu=cat(w_up),
                wd=w_down.reshape(N_EXPERTS * D_EXPERT, D_MODEL).astype(BF16))


def kernel(x, positions, mix_norm, w_in, mla_g_cq, mla_g_ckv, mla_w_uq, mla_w_ukv, mla_g_q, mla_g_k, lru_conv_w, lru_conv_b, lru_w_a, lru_b_a, lru_w_i, lru_b_i, lru_lambda, s5_a_re, s5_a_im, s5_log_dt, s5_b_re, s5_b_im, s5_c_re, s5_c_im, s5_d, s5_w_glu, s5_b_glu, nsa_g_q, nsa_g_k, nsa_pe_k, nsa_w1_k, nsa_w2_k, nsa_pe_v, nsa_w1_v, nsa_w2_v, out_norm, w_out, ffn_norm, moe_w_rg, moe_b_rg, moe_w_re, moe_b_re, moe_w_gate, moe_w_up, moe_w_down):
    b, s, d = x.shape
    t = b * s
    pos3 = positions.astype(jnp.int32)[:, :, None]
    x2 = x.reshape(t, d)
    for l in range(w_in.shape[0]):
        o_mla, o_lru, o_s5, o_nq, o_nkv = _inproj(x2, mix_norm[l], _prep_w_in(w_in[l]))
        y_a = _mla(o_mla.reshape(b, s, MLA_IN_W), pos3,
                   _prep_mla(mla_g_cq[l], mla_g_ckv[l], mla_w_uq[l], mla_w_ukv[l], mla_g_q[l], mla_g_k[l]))
        y_b = _lru(o_lru.reshape(b, s, LRU_IN_W),
                   _prep_lru(lru_conv_w[l], lru_conv_b[l], lru_w_a[l], lru_b_a[l], lru_w_i[l], lru_b_i[l],
                             lru_lambda[l]))
        u_tm = o_s5.reshape(b, s, S5_W).transpose(1, 0, 2)
        y_c = _s5(u_tm, _prep_s5(s5_a_re[l], s5_a_im[l], s5_log_dt[l], s5_b_re[l], s5_b_im[l], s5_c_re[l],
                                 s5_c_im[l], s5_d[l], s5_w_glu[l], s5_b_glu[l])).transpose(1, 0, 2)
        y_d = _nsa(o_nq.reshape(b, s, NQ_IN_W), o_nkv.reshape(b, s, NKV_IN_W), pos3,
                   _prep_nsa(nsa_g_q[l], nsa_g_k[l], nsa_pe_k[l], nsa_w1_k[l], nsa_w2_k[l], nsa_pe_v[l],
                             nsa_w1_v[l], nsa_w2_v[l]))
        ys = [y.reshape(t, GROUP_W) for y in (y_a, y_b, y_c, y_d)]
        x2 = _outproj(ys, x2, out_norm[l], w_out[l].astype(BF16))
        x2 = _moe(x2, _prep_moe(ffn_norm[l], moe_w_rg[l], moe_b_rg[l], moe_w_re[l], moe_b_re[l],
                                moe_w_gate[l], moe_w_up[l], moe_w_down[l]))
    return x2.reshape(b, s, d)
```

```python
import functools
import math

import numpy as np
import jax
import jax.numpy as jnp
from jax import lax
from jax.experimental import pallas as pl
from jax.experimental.pallas import tpu as pltpu

F32 = jnp.float32
BF16 = jnp.bfloat16

D_MODEL = 1024
DEPTH = 2
GROUP_W = 256
EPS = 1e-6
ROPE_THETA = 500000.0
NEG = -1e30
LOG2E = math.log2(math.e)

MLA_HEADS = 4
MLA_ROPE = 32
MLA_NOPE = 64
MLA_V = 64
MLA_QK = 96
MLA_Q_RANK = 192
MLA_KV_RANK = 128

LRU_W = 256
LRU_BLOCKS = 4
LRU_BW = 64
CONV_W = 4
LRU_C = 8.0

S5_W = 256
S5_CH = 16
S5_GROUPS = 16
S5_P = 64
S5_STATE = S5_GROUPS * S5_P

NSA_HEADS = 4
NSA_DK = 64
NSA_ROT = 16
CMP_LEN = 32
CMP_STRIDE = 16
CMP_HID = 128
SEL_LEN = 64
SEL_TOPK = 5
WIN = 512

N_GROUPS = 4
EXP_PER_GROUP = 4
N_EXPERTS = 16
D_EXPERT = 256

D_IN = 1772

LANES = 128
VMEM_LIMIT = 48 * 1024 * 1024

MLA_IN_W = 512
LRU_IN_W = 512
S5_IN_W = 256
NQ_IN_W = NSA_HEADS * LANES
NKV_IN_W = 6 * LANES
IN_W = MLA_IN_W + LRU_IN_W + S5_IN_W + NQ_IN_W + NKV_IN_W


def _cparams(sem):
    return pltpu.CompilerParams(dimension_semantics=sem, vmem_limit_bytes=VMEM_LIMIT)


def _dot(a, b):
    return jnp.dot(a.astype(BF16), b.astype(BF16), preferred_element_type=F32)


def _dot_nt(a, b):
    return lax.dot_general(a.astype(BF16), b.astype(BF16), (((1,), (1,)), ((), ())),
                           preferred_element_type=F32)


def _split3(x):
    hi = x.astype(BF16)
    r = x - hi.astype(F32)
    mid = r.astype(BF16)
    lo = (r - mid.astype(F32)).astype(BF16)
    return hi, mid, lo


def _dot_f32_by_exact(x, w_bf16):
    hi, mid, lo = _split3(x)
    return (jnp.dot(hi, w_bf16, preferred_element_type=F32)
            + jnp.dot(mid, w_bf16, preferred_element_type=F32)
            + jnp.dot(lo, w_bf16, preferred_element_type=F32))


def _rms(x, g, n):
    return x * lax.rsqrt(jnp.sum(x * x, axis=-1, keepdims=True) * (1.0 / n) + EPS) * g


def _gelu(x):
    return 0.5 * x * (1.0 + jnp.tanh(math.sqrt(2.0 / math.pi) * (x + 0.044715 * (x * x * x))))


def _rope(x, cos, sin_lo, sin_hi, half):
    return (x * cos + pltpu.roll(x, LANES - half, axis=1) * sin_lo
            + pltpu.roll(x, half, axis=1) * sin_hi)


def _with_ones(x, upper):
    if upper:
        x = pltpu.roll(x, LANES // 2, axis=1)
    lane = lax.broadcasted_iota(jnp.int32, x.shape, 1)
    return jnp.where(lane < LANES // 2, x, 1.0).astype(BF16)


def _rope_tables(pos_f32, inv, m_lo, m_hi):
    ang = pos_f32 * inv
    s = jnp.sin(ang)
    return jnp.cos(ang), s * m_lo, s * m_hi


def _rope_consts(rot_dim):
    half = rot_dim // 2
    inv = ROPE_THETA ** (-jnp.arange(half, dtype=F32) * 2.0 / rot_dim)
    z = jnp.zeros((LANES - rot_dim,), F32)
    inv_l = jnp.concatenate([inv, inv, z])[None, :]
    m_lo = jnp.concatenate([-jnp.ones((half,), F32), jnp.zeros((LANES - half,), F32)])[None, :]
    m_hi = jnp.concatenate([jnp.zeros((half,), F32), jnp.ones((half,), F32), z])[None, :]
    return inv_l, m_lo, m_hi


def _pad_last(a, n):
    return jnp.pad(a, [(0, 0)] * (a.ndim - 1) + [(0, n - a.shape[-1])])


def _full(shape):
    nd = len(shape)
    return pl.BlockSpec(shape, lambda *_: (0,) * nd)


def _inproj_kernel(x_ref, g_ref, w_ref, o_mla, o_lru, o_s5, o_nq, o_nkv):
    x = x_ref[...]
    h = x * lax.rsqrt(jnp.mean(x * x, axis=-1, keepdims=True) + EPS) * g_ref[...]
    y = jnp.dot(h.astype(BF16), w_ref[...], preferred_element_type=F32)
    c0 = 0
    for o in (o_mla, o_lru, o_s5, o_nq, o_nkv):
        w = o.shape[-1]
        o[...] = y[:, c0:c0 + w]
        c0 += w


def _inproj_cols():
    src = -np.ones((IN_W,), np.int64)
    o = 0
    src[o:o + 192] = np.arange(0, 192)
    src[o + 256:o + 384] = np.arange(192, 320)
    src[o + 384:o + 416] = np.arange(320, 352)
    o += MLA_IN_W
    src[o:o + 512] = np.arange(352, 864)
    o += LRU_IN_W
    src[o:o + 256] = np.arange(864, 1120)
    o += S5_IN_W
    for h in range(NSA_HEADS):
        src[o + LANES * h:o + LANES * h + 64] = np.arange(1120 + 64 * h, 1120 + 64 * h + 64)
    o += NQ_IN_W
    kv0 = 1376
    src[o:o + 64] = np.arange(kv0, kv0 + 64)
    src[o + 128:o + 192] = np.arange(kv0 + 128, kv0 + 192)
    src[o + 256:o + 320] = np.arange(kv0 + 256, kv0 + 320)
    src[o + 384:o + 448] = np.arange(kv0 + 64, kv0 + 128)
    src[o + 448:o + 512] = np.arange(kv0 + 192, kv0 + 256)
    src[o + 512:o + 576] = np.arange(kv0 + 320, kv0 + 384)
    src[o + 640:o + 652] = np.arange(1760, 1772)
    return src


_INPROJ_SRC = _inproj_cols()


def _prep_w_in(w_in):
    idx = jnp.asarray(np.maximum(_INPROJ_SRC, 0), jnp.int32)
    keep = jnp.asarray(_INPROJ_SRC >= 0)
    return jnp.where(keep[None, :], jnp.take(w_in, idx, axis=1), 0.0).astype(BF16)


def _inproj(x2, g, w_pad, tm=512):
    t = x2.shape[0]
    widths = (MLA_IN_W, LRU_IN_W, S5_IN_W, NQ_IN_W, NKV_IN_W)
    return pl.pallas_call(
        _inproj_kernel,
        grid=(t // tm,),
        in_specs=[pl.BlockSpec((tm, D_MODEL), lambda i: (i, 0)),
                  _full((1, D_MODEL)),
                  _full((D_MODEL, IN_W))],
        out_specs=[pl.BlockSpec((tm, w), lambda i: (i, 0)) for w in widths],
        out_shape=[jax.ShapeDtypeStruct((t, w), F32) for w in widths],
        compiler_params=_cparams(("parallel",)),
        name="inproj",
    )(x2, g[None, :], w_pad)


def _mla_prep_kernel(in_ref, pos_ref, gcq_ref, wuq_ref, gckv_ref, wk_ref, wv_ref, gq_ref, gk_ref,
                     inv_ref, mlo_ref, mhi_ref, q_out, k_out, v_out):
    xin = in_ref[0]
    cq = _rms(xin[:, 0:256], gcq_ref[...], MLA_Q_RANK)
    ckv = _rms(xin[:, 256:384], gckv_ref[...], MLA_KV_RANK)
    kpe = xin[:, 384:512]
    q = _dot(cq, wuq_ref[...])
    kn = _dot(ckv, wk_ref[...])
    v = _dot(ckv, wv_ref[...])
    cos, s_lo, s_hi = _rope_tables(pos_ref[0].astype(F32), inv_ref[...], mlo_ref[...], mhi_ref[...])
    scale = MLA_QK ** -0.5 * LOG2E
    for h in range(MLA_HEADS):
        qh = _rms(q[:, LANES * h:LANES * (h + 1)], gq_ref[...], MLA_QK)
        q_out[0, h] = (_rope(qh, cos, s_lo, s_hi, MLA_ROPE // 2) * scale).astype(BF16)
        kh = _rms(kn[:, LANES * h:LANES * (h + 1)] + kpe, gk_ref[...], MLA_QK)
        k_out[0, h] = _rope(kh, cos, s_lo, s_hi, MLA_ROPE // 2).astype(BF16)
        v_out[0, h] = _with_ones(v[:, LANES * (h // 2):LANES * (h // 2 + 1)], h % 2 == 1)


def _attend(q, segs, v1, s_scr, p_scr, m_scr, nh, chunk):
    rows = q.shape[0]
    c0 = 0
    m = None
    for k, bias in segs:
        n = k.shape[0]
        sb = _dot_nt(q, k)
        if bias is not None:
            sb = (sb.reshape(nh, rows // nh, n) + bias[None]).reshape(rows, n)
        s_scr[0:rows, c0:c0 + n] = sb
        mx = jnp.max(sb, axis=-1, keepdims=True)
        m = mx if m is None else jnp.maximum(m, mx)
        c0 += n
    m_scr[0:rows, :] = jnp.broadcast_to(m, (rows, LANES))
    for r0 in range(0, rows, chunk):
        mb = m_scr[r0:r0 + chunk, :]
        for j0 in range(0, c0, LANES):
            p = jnp.exp2(s_scr[r0:r0 + chunk, j0:j0 + LANES] - mb)
            p_scr[r0:r0 + chunk, j0:j0 + LANES] = p.astype(BF16)
    o = jnp.dot(p_scr[0:rows, 0:c0], v1, preferred_element_type=F32)
    return (o * (1.0 / pltpu.roll(o, LANES // 2, axis=1)))[:, 0:LANES // 2]


def _mla_attn_kernel(q_ref, k_ref, v_ref, o_ref, s_scr, p_scr, m_scr, o_scr, bias_scr, *, tq, chunk):
    qi = pl.program_id(1)
    row = lax.broadcasted_iota(jnp.int32, (tq, tq), 0)
    col = lax.broadcasted_iota(jnp.int32, (tq, tq), 1)
    bias_scr[...] = jnp.where(col <= row, 0.0, NEG)
    nset = s_scr.shape[0]
    for c in range(k_ref.shape[2] // tq):

        @pl.when(qi == c)
        def _(c=c):
            n = (c + 1) * tq

            def heads(i, carry):
                for j in range(nset):
                    h = i * nset + j
                    segs = [(k_ref[0, h, n - tq:n, :], bias_scr[...])]
                    if c > 0:
                        segs = [(k_ref[0, h, 0:n - tq, :], None)] + segs
                    o_scr[h] = _attend(q_ref[0, h], segs, v_ref[0, h, 0:n, :], s_scr.at[j], p_scr.at[j],
                                       m_scr.at[j], 1, chunk)
                return carry

            lax.fori_loop(0, MLA_HEADS // nset, heads, 0)

    for h in range(MLA_HEADS):
        o_ref[0, :, MLA_V * h:MLA_V * (h + 1)] = o_scr[h]


def _mla(mla_in, pos3, p, ts=512, tq=512, chunk=128, nset=2):
    b, s, _ = mla_in.shape
    inv, m_lo, m_hi = _rope_consts(MLA_ROPE)
    hq = (b, MLA_HEADS, s, LANES)
    q, k, v = pl.pallas_call(
        _mla_prep_kernel,
        grid=(b, s // ts),
        in_specs=[pl.BlockSpec((1, ts, MLA_IN_W), lambda i, j: (i, j, 0)),
                  pl.BlockSpec((1, ts, 1), lambda i, j: (i, j, 0)),
                  _full((1, 256)), _full((256, 512)), _full((1, 128)), _full((128, 512)),
                  _full((128, 256)), _full((1, 128)), _full((1, 128)),
                  _full((1, 128)), _full((1, 128)), _full((1, 128))],
        out_specs=[pl.BlockSpec((1, MLA_HEADS, ts, LANES), lambda i, j: (i, 0, j, 0)),
                   pl.BlockSpec((1, MLA_HEADS, ts, LANES), lambda i, j: (i, 0, j, 0)),
                   pl.BlockSpec((1, MLA_HEADS, ts, LANES), lambda i, j: (i, 0, j, 0))],
        out_shape=[jax.ShapeDtypeStruct(hq, BF16), jax.ShapeDtypeStruct(hq, BF16),
                   jax.ShapeDtypeStruct(hq, BF16)],
        compiler_params=_cparams(("parallel", "parallel")),
        name="mla_prep",
    )(mla_in, pos3, p["g_cq"], p["w_uq"], p["g_ckv"], p["w_k"], p["w_v"], p["g_q"], p["g_k"],
      inv, m_lo, m_hi)
    return pl.pallas_call(
        functools.partial(_mla_attn_kernel, tq=tq, chunk=chunk),
        grid=(b, s // tq),
        in_specs=[pl.BlockSpec((1, MLA_HEADS, tq, LANES), lambda i, j: (i, 0, j, 0)),
                  pl.BlockSpec((1, MLA_HEADS, s, LANES), lambda i, j: (i, 0, 0, 0)),
                  pl.BlockSpec((1, MLA_HEADS, s, LANES), lambda i, j: (i, 0, 0, 0))],
        out_specs=pl.BlockSpec((1, tq, GROUP_W), lambda i, j: (i, j, 0)),
        out_shape=jax.ShapeDtypeStruct((b, s, GROUP_W), F32),
        scratch_shapes=[pltpu.VMEM((nset, tq, s), F32), pltpu.VMEM((nset, tq, s), BF16),
                        pltpu.VMEM((nset, tq, LANES), F32),
                        pltpu.VMEM((MLA_HEADS, tq, MLA_V), F32), pltpu.VMEM((tq, tq), F32)],
        compiler_params=_cparams(("parallel", "arbitrary")),
        name="mla_attn",
    )(q, k, v)


def _prep_mla(g_cq, g_ckv, w_uq, w_ukv, g_q, g_k):
    wq = w_uq.reshape(MLA_Q_RANK, MLA_HEADS, MLA_QK)
    wq = _pad_last(wq, LANES).reshape(MLA_Q_RANK, MLA_HEADS * LANES)
    wq = jnp.pad(wq, ((0, 256 - MLA_Q_RANK), (0, 0)))
    wkv = w_ukv.reshape(MLA_KV_RANK, MLA_HEADS, MLA_NOPE + MLA_V)
    wk = jnp.pad(wkv[:, :, :MLA_NOPE], ((0, 0), (0, 0), (MLA_ROPE, LANES - MLA_QK)))
    wk = wk.reshape(MLA_KV_RANK, MLA_HEADS * LANES)
    wv = wkv[:, :, MLA_NOPE:].reshape(MLA_KV_RANK, MLA_HEADS * MLA_V)
    return dict(g_cq=_pad_last(g_cq, 256)[None, :], g_ckv=g_ckv[None, :], w_uq=wq.astype(BF16),
                w_k=wk.astype(BF16), w_v=wv.astype(BF16), g_q=_pad_last(g_q, LANES)[None, :],
                g_k=_pad_last(g_k, LANES)[None, :])


def _shift_rows(x, k, row, fill):
    return jnp.where(row >= k, pltpu.roll(x, k, axis=0), fill)


def _lru_kernel(in_ref, cw_ref, cb_ref, wa_ref, ba_ref, wi_ref, bi_ref, lam_ref, o_ref):
    xin = in_ref[0]
    s = xin.shape[0]
    xb = xin[:, :LRU_W]
    gb = xin[:, LRU_W:]
    row = lax.broadcasted_iota(jnp.int32, (s, LRU_W), 0)
    u = cb_ref[...] + xb * cw_ref[CONV_W - 1:CONV_W, :]
    for j in range(CONV_W - 1):
        u = u + _shift_rows(xb, CONV_W - 1 - j, row, 0.0) * cw_ref[j:j + 1, :]
    r = jax.nn.sigmoid(_dot(u, wa_ref[...]) + ba_ref[...])
    gi = jax.nn.sigmoid(_dot(u, wi_ref[...]) + bi_ref[...])
    nlam = -lam_ref[...]
    softplus = jnp.maximum(nlam, 0.0) + jnp.log1p(jnp.exp(-jnp.abs(nlam)))
    log_a = (-LRU_C) * r * softplus
    a = jnp.exp(log_a)
    mult = jnp.sqrt(jnp.maximum(-jnp.tanh(log_a) * (a * a + 1.0), 0.0))
    mult = jnp.where(row == 0, 1.0, mult)
    bt = mult * gi * u
    k = 1
    while k < s:
        bt = a * _shift_rows(bt, k, row, 0.0) + bt
        a = a * _shift_rows(a, k, row, 1.0)
        k *= 2
    o_ref[0] = bt * _gelu(gb)


def _block_diag(w):
    n, i, j = w.shape
    eye = jnp.eye(n, dtype=w.dtype)
    return (eye[:, None, :, None] * w[:, :, None, :]).reshape(n * i, n * j)


def _lru(lru_in, p):
    b, s, _ = lru_in.shape
    return pl.pallas_call(
        _lru_kernel,
        grid=(b,),
        in_specs=[pl.BlockSpec((1, s, LRU_IN_W), lambda i: (i, 0, 0)),
                  _full((CONV_W, LRU_W)), _full((1, LRU_W)), _full((LRU_W, LRU_W)), _full((1, LRU_W)),
                  _full((LRU_W, LRU_W)), _full((1, LRU_W)), _full((1, LRU_W))],
        out_specs=pl.BlockSpec((1, s, LRU_W), lambda i: (i, 0, 0)),
        out_shape=jax.ShapeDtypeStruct((b, s, LRU_W), F32),
        compiler_params=_cparams(("parallel",)),
        name="rglru",
    )(lru_in, p["cw"], p["cb"], p["wa"], p["ba"], p["wi"], p["bi"], p["lam"])


def _prep_lru(conv_w, conv_b, w_a, b_a, w_i, b_i, lam):
    return dict(cw=conv_w, cb=conv_b[None, :], wa=_block_diag(w_a).astype(BF16),
                ba=b_a.reshape(1, LRU_W), wi=_block_diag(w_i).astype(BF16), bi=b_i.reshape(1, LRU_W),
                lam=lam[None, :])


def _s5_disc_kernel(are_ref, aim_ref, ldt_ref, arer_ref, aimr_ref, bre_ref, bim_ref,
                    abre_ref, abim_ref, bbre_ref, bbim_ref):
    dt = jnp.exp(ldt_ref[...])

    def disc(a_re, a_im):
        mag = jnp.exp(dt * a_re)
        ab_re = mag * jnp.cos(dt * a_im)
        ab_im = mag * jnp.sin(dt * a_im)
        den = a_re * a_re + a_im * a_im
        n_re = ab_re - 1.0
        g_re = (n_re * a_re + ab_im * a_im) / den
        g_im = (ab_im * a_re - n_re * a_im) / den
        return ab_re, ab_im, g_re, g_im

    ab_re, ab_im, _, _ = disc(are_ref[...], aim_ref[...])
    abre_ref[...] = ab_re
    abim_ref[...] = ab_im
    _, _, g_re, g_im = disc(arer_ref[...], aimr_ref[...])
    bbre_ref[...] = g_re * bre_ref[...] - g_im * bim_ref[...]
    bbim_ref[...] = g_re * bim_ref[...] + g_im * bre_ref[...]


def _s5_kernel(u_ref, are_ref, aim_ref, bre_ref, bim_ref, cre_ref, cim_ref, d_ref, wg_ref, bg_ref,
               o_ref, hre, him, st_re, st_im, *, tc, nb, cw):
    @pl.when(pl.program_id(0) == 0)
    def _():
        st_re[...] = jnp.zeros_like(st_re)
        st_im[...] = jnp.zeros_like(st_im)

    u = u_ref[...].reshape(tc * nb, S5_W)
    ub = u.astype(BF16)
    hre[...] = jnp.dot(ub, bre_ref[...], preferred_element_type=F32)
    him[...] = jnp.dot(ub, bim_ref[...], preferred_element_type=F32)
    for c in range(S5_STATE // cw):
        cs = slice(c * cw, (c + 1) * cw)
        ar = jnp.broadcast_to(are_ref[:, cs], (nb, cw))
        ai = jnp.broadcast_to(aim_ref[:, cs], (nb, cw))

        def body(t, carry, cs=cs, ar=ar, ai=ai):
            hr, hi = carry
            r0 = pl.multiple_of(t * nb, nb)
            nr = ar * hr - ai * hi + hre[pl.ds(r0, nb), cs]
            ni = ar * hi + ai * hr + him[pl.ds(r0, nb), cs]
            hre[pl.ds(r0, nb), cs] = nr
            him[pl.ds(r0, nb), cs] = ni
            return nr, ni

        hr, hi = lax.fori_loop(0, tc, body, (st_re[:, cs], st_im[:, cs]), unroll=4)
        st_re[:, cs] = hr
        st_im[:, cs] = hi
    y = (jnp.dot(hre[...].astype(BF16), cre_ref[...], preferred_element_type=F32)
         - jnp.dot(him[...].astype(BF16), cim_ref[...], preferred_element_type=F32))
    y = _gelu(y + d_ref[...] * u)
    z = _dot(y, wg_ref[...]) + bg_ref[...]
    o_ref[...] = (y * jax.nn.sigmoid(z)).reshape(tc, nb, S5_W)


def _prep_s5(a_re, a_im, log_dt, b_re, b_im, c_re, c_im, d, w_glu, b_glu):
    g, p, ch = S5_GROUPS, S5_P, S5_CH
    ab_re, ab_im, bb_re, bb_im = pl.pallas_call(
        _s5_disc_kernel,
        out_shape=[jax.ShapeDtypeStruct((g, p), F32), jax.ShapeDtypeStruct((g, p), F32),
                   jax.ShapeDtypeStruct((g, p * ch), F32), jax.ShapeDtypeStruct((g, p * ch), F32)],
        name="s5_discretize",
    )(a_re, a_im, log_dt[:, None], jnp.repeat(a_re, ch, axis=1), jnp.repeat(a_im, ch, axis=1),
      b_re.reshape(g, p * ch), b_im.reshape(g, p * ch))

    def b_dense(bb):
        return _block_diag(bb.reshape(g, p, ch).transpose(0, 2, 1)).astype(BF16)

    def c_dense(c):
        return _block_diag(c.transpose(0, 2, 1)).astype(BF16)

    return dict(a_re=ab_re.reshape(1, S5_STATE), a_im=ab_im.reshape(1, S5_STATE),
                b_re=b_dense(bb_re), b_im=b_dense(bb_im), c_re=c_dense(c_re), c_im=c_dense(c_im),
                d=d[None, :], w_glu=w_glu.astype(BF16), b_glu=b_glu[None, :])


def _s5(u_tm, p, tc=64, cw=256):
    s, nb, _ = u_tm.shape
    return pl.pallas_call(
        functools.partial(_s5_kernel, tc=tc, nb=nb, cw=cw),
        grid=(s // tc,),
        in_specs=[pl.BlockSpec((tc, nb, S5_W), lambda i: (i, 0, 0)),
                  _full((1, S5_STATE)), _full((1, S5_STATE)),
                  _full((S5_W, S5_STATE)), _full((S5_W, S5_STATE)),
                  _full((S5_STATE, S5_W)), _full((S5_STATE, S5_W)),
                  _full((1, S5_W)), _full((S5_W, S5_W)), _full((1, S5_W))],
        out_specs=pl.BlockSpec((tc, nb, S5_W), lambda i: (i, 0, 0)),
        out_shape=jax.ShapeDtypeStruct((s, nb, S5_W), F32),
        scratch_shapes=[pltpu.VMEM((tc * nb, S5_STATE), F32), pltpu.VMEM((tc * nb, S5_STATE), F32),
                        pltpu.VMEM((nb, S5_STATE), F32), pltpu.VMEM((nb, S5_STATE), F32)],
        compiler_params=_cparams(("arbitrary",)),
        name="s5",
    )(u_tm, p["a_re"], p["a_im"], p["b_re"], p["b_im"], p["c_re"], p["c_im"], p["d"], p["w_glu"],
      p["b_glu"])


def _nsa_prep_kernel(nq_ref, nkv_ref, pos_ref, gq_ref, gk_ref, inv_ref, mlo_ref, mhi_ref,
                     q_out, ks_out, kw_out, vs_out, vw_out):
    cos, s_lo, s_hi = _rope_tables(pos_ref[0].astype(F32), inv_ref[...], mlo_ref[...], mhi_ref[...])
    half = NSA_ROT // 2
    nq = nq_ref[0]
    scale = NSA_DK ** -0.5 * LOG2E
    for h in range(NSA_HEADS):
        qh = _rms(nq[:, LANES * h:LANES * (h + 1)], gq_ref[...], NSA_DK)
        q_out[0, h] = (_rope(qh, cos, s_lo, s_hi, half) * scale).astype(BF16)
    nkv = nkv_ref[0]
    ks = _rms(nkv[:, 128:256], gk_ref[1:2, :], NSA_DK)
    ks_out[0] = _rope(ks, cos, s_lo, s_hi, half).astype(BF16)
    kw = _rms(nkv[:, 256:384], gk_ref[2:3, :], NSA_DK)
    kw_out[0] = _rope(kw, cos, s_lo, s_hi, half).astype(BF16)
    vs_out[0] = _with_ones(nkv[:, 384:512], True)
    vw_out[0] = _with_ones(nkv[:, 512:640], False)


def _nsa_cmp_kernel(gk_in, gv_in, posc_ref, pek_ref, w1k_ref, w2k_ref, pev_ref, w1v_ref, w2v_ref,
                    g_ref, inv_ref, mlo_ref, mhi_ref, kc_out, vc_out):
    half_w = CMP_STRIDE * NSA_DK
    nc = gk_in.shape[1]

    def compress(g, pe_ref, w1_ref, w2_ref):
        lo = _dot(g + pe_ref[:, :half_w], w1_ref[:half_w, :])
        hi = _dot(g + pe_ref[:, half_w:], w1_ref[half_w:, :])
        hid = lo + pltpu.roll(hi, nc - 1, axis=0)
        return _dot(_gelu(hid), w2_ref[...])

    kc = compress(gk_in[0], pek_ref, w1k_ref, w2k_ref)
    cos, s_lo, s_hi = _rope_tables(posc_ref[0].astype(F32), inv_ref[...], mlo_ref[...], mhi_ref[...])
    kc = _rope(_rms(kc, g_ref[0:1, :], NSA_DK), cos, s_lo, s_hi, NSA_ROT // 2)
    kc_out[0] = kc.astype(BF16)
    vc_out[0] = compress(gv_in[0], pev_ref, w1v_ref, w2v_ref).astype(BF16)


def _nsa_attn_kernel(q_ref, kc_ref, vc_ref, ks_ref, vs_ref, kw_ref, vw_ref, gate_ref, ov_ref, ex_ref,
                     o_ref, s_scr, p_scr, m_scr, obr_scr, q_scr, *, tq, chunk, kchunk):
    qi = pl.program_id(1)
    nh = NSA_HEADS
    rows = nh * tq
    s_len = ks_ref.shape[1]
    q = q_ref[0].reshape(rows, LANES)
    lane = lax.broadcasted_iota(jnp.int32, (tq, LANES), 1)
    qpos = qi * tq + lax.broadcasted_iota(jnp.int32, (tq, LANES), 0)

    s = _dot_nt(q, kc_ref[0]).reshape(nh, tq, LANES)
    valid = (lane * CMP_STRIDE + (CMP_LEN - 1)) <= qpos
    s = jnp.where(valid, s, NEG)
    e = jnp.exp2(s - jnp.max(s, axis=-1, keepdims=True))
    p_c = jnp.where(valid, e / jnp.sum(e, axis=-1, keepdims=True), 0.0)
    o_cmp = jnp.dot(p_c.astype(BF16).reshape(nh * tq, LANES), vc_ref[0], preferred_element_type=F32)

    imp = _dot_f32_by_exact(jnp.sum(p_c, axis=0), ov_ref[...])
    cur = qpos // SEL_LEN
    lane_f = lane.astype(F32)
    forced = (lane == 0) | (lane == cur) | (lane == cur - 1)
    cand = (lane < cur - 1) & (lane > 0)
    sel = forced
    for _ in range(SEL_TOPK - 3):
        sc = jnp.where(cand, imp, -jnp.inf)
        best = jnp.max(sc, axis=-1, keepdims=True)
        pick = jnp.min(jnp.where(cand & (sc == best), lane_f, float(LANES)), axis=-1, keepdims=True)
        hit = lane_f == pick
        sel = sel | hit
        cand = cand & jnp.logical_not(hit)
    sel_b = jnp.where(sel, 1.0, 0.0).astype(BF16)
    obr_scr[0] = o_cmp
    q_scr[...] = q

    def window():
        wk = WIN + tq
        start = pl.multiple_of(jnp.maximum(qi - WIN // tq, 0) * tq, tq)
        kpos = start + lax.broadcasted_iota(jnp.int32, (tq, wk), 1)
        qrow = qi * tq + lax.broadcasted_iota(jnp.int32, (tq, wk), 0)
        bias = jnp.where((kpos <= qrow) & (qrow - kpos < WIN), 0.0, NEG)
        obr_scr[2] = _attend(q_scr[...], [(kw_ref[0, pl.ds(start, wk), :], bias)],
                             vw_ref[0, pl.ds(start, wk), :], s_scr.at[1], p_scr.at[1], m_scr.at[1],
                             nh, chunk)

    for c in range(s_len // kchunk):

        @pl.when(qi // (kchunk // tq) == c)
        def _(c=c):
            n = (c + 1) * kchunk
            em = jnp.dot(sel_b, ex_ref[:, 0:n], preferred_element_type=F32)
            kpos = lax.broadcasted_iota(jnp.int32, (tq, n), 1)
            qrow = qi * tq + lax.broadcasted_iota(jnp.int32, (tq, n), 0)
            bias = jnp.where((em > 0.5) & (kpos <= qrow), 0.0, NEG)
            obr_scr[1] = _attend(q_scr[...], [(ks_ref[0, 0:n, :], bias)], vs_ref[0, 0:n, :],
                                 s_scr.at[0], p_scr.at[0], m_scr.at[0], nh, chunk)
            window()

    g = jax.nn.sigmoid(gate_ref[0])
    for h in range(nh):
        r = slice(h * tq, (h + 1) * tq)
        o_ref[0, :, NSA_DK * h:NSA_DK * (h + 1)] = (g[:, 3 * h:3 * h + 1] * obr_scr[0, r, :]
                                                    + g[:, 3 * h + 1:3 * h + 2] * obr_scr[1, r, :]
                                                    + g[:, 3 * h + 2:3 * h + 3] * obr_scr[2, r, :])


def _nsa_tables(s, tq):
    nc = s // CMP_STRIDE
    nsb = s // SEL_LEN
    cs = np.arange(nc) * CMP_STRIDE
    ss = np.arange(nsb) * SEL_LEN
    ov = np.clip(np.minimum(cs[:, None] + CMP_LEN, ss[None, :] + SEL_LEN)
                 - np.maximum(cs[:, None], ss[None, :]), 0, None) / CMP_STRIDE
    ov[(s - CMP_LEN) // CMP_STRIDE + 1:] = 0.0
    ov_pad = np.zeros((nc, LANES), np.float32)
    ov_pad[:, :nsb] = ov
    ex = np.zeros((LANES, s), np.float32)
    ex[np.arange(s) // SEL_LEN, np.arange(s)] = 1.0
    return jnp.asarray(ov_pad, BF16), jnp.asarray(ex, BF16)


def _nsa(nq, nkv, pos3, p, ts=512, tq=128, chunk=128, kchunk=512):
    b, s, _ = nq.shape
    nc = s // CMP_STRIDE
    assert nc == LANES and s // SEL_LEN <= LANES and s >= WIN + tq and s % kchunk == 0
    inv, m_lo, m_hi = _rope_consts(NSA_ROT)
    q, ks, kw, vs, vw = pl.pallas_call(
        _nsa_prep_kernel,
        grid=(b, s // ts),
        in_specs=[pl.BlockSpec((1, ts, NQ_IN_W), lambda i, j: (i, j, 0)),
                  pl.BlockSpec((1, ts, NKV_IN_W), lambda i, j: (i, j, 0)),
                  pl.BlockSpec((1, ts, 1), lambda i, j: (i, j, 0)),
                  _full((1, LANES)), _full((3, LANES)), _full((1, LANES)), _full((1, LANES)),
                  _full((1, LANES))],
        out_specs=[pl.BlockSpec((1, NSA_HEADS, ts, LANES), lambda i, j: (i, 0, j, 0)),
                   pl.BlockSpec((1, ts, LANES), lambda i, j: (i, j, 0)),
                   pl.BlockSpec((1, ts, LANES), lambda i, j: (i, j, 0)),
                   pl.BlockSpec((1, ts, LANES), lambda i, j: (i, j, 0)),
                   pl.BlockSpec((1, ts, LANES), lambda i, j: (i, j, 0))],
        out_shape=[jax.ShapeDtypeStruct((b, NSA_HEADS, s, LANES), BF16),
                   jax.ShapeDtypeStruct((b, s, LANES), BF16), jax.ShapeDtypeStruct((b, s, LANES), BF16),
                   jax.ShapeDtypeStruct((b, s, LANES), BF16), jax.ShapeDtypeStruct((b, s, LANES), BF16)],
        compiler_params=_cparams(("parallel", "parallel")),
        name="nsa_prep",
    )(nq, nkv, pos3, p["g_q"], p["g_k"], inv, m_lo, m_hi)

    gk = nkv[:, :, 0:NSA_DK].reshape(b, nc, CMP_STRIDE * NSA_DK)
    gv = nkv[:, :, 384:384 + NSA_DK].reshape(b, nc, CMP_STRIDE * NSA_DK)
    last = np.minimum(np.arange(nc) * CMP_STRIDE + CMP_LEN - 1, s - 1)
    posc = pos3[:, last, :]
    gw = CMP_STRIDE * NSA_DK
    kc, vc = pl.pallas_call(
        _nsa_cmp_kernel,
        grid=(b,),
        in_specs=[pl.BlockSpec((1, nc, gw), lambda i: (i, 0, 0)),
                  pl.BlockSpec((1, nc, gw), lambda i: (i, 0, 0)),
                  pl.BlockSpec((1, nc, 1), lambda i: (i, 0, 0)),
                  _full((1, 2 * gw)), _full((2 * gw, CMP_HID)), _full((CMP_HID, LANES)),
                  _full((1, 2 * gw)), _full((2 * gw, CMP_HID)), _full((CMP_HID, NSA_DK)),
                  _full((3, LANES)), _full((1, LANES)), _full((1, LANES)), _full((1, LANES))],
        out_specs=[pl.BlockSpec((1, nc, LANES), lambda i: (i, 0, 0)),
                   pl.BlockSpec((1, nc, NSA_DK), lambda i: (i, 0, 0))],
        out_shape=[jax.ShapeDtypeStruct((b, nc, LANES), BF16),
                   jax.ShapeDtypeStruct((b, nc, NSA_DK), BF16)],
        compiler_params=_cparams(("parallel",)),
        name="nsa_compress",
    )(gk, gv, posc, p["pe_k"], p["w1_k"], p["w2_k"], p["pe_v"], p["w1_v"], p["w2_v"], p["g_k"],
      inv, m_lo, m_hi)

    ov, ex = _nsa_tables(s, tq)
    rows = NSA_HEADS * tq
    return pl.pallas_call(
        functools.partial(_nsa_attn_kernel, tq=tq, chunk=chunk, kchunk=kchunk),
        grid=(b, s // tq),
        in_specs=[pl.BlockSpec((1, NSA_HEADS, tq, LANES), lambda i, j: (i, 0, j, 0)),
                  pl.BlockSpec((1, nc, LANES), lambda i, j: (i, 0, 0)),
                  pl.BlockSpec((1, nc, NSA_DK), lambda i, j: (i, 0, 0)),
                  pl.BlockSpec((1, s, LANES), lambda i, j: (i, 0, 0)),
                  pl.BlockSpec((1, s, LANES), lambda i, j: (i, 0, 0)),
                  pl.BlockSpec((1, s, LANES), lambda i, j: (i, 0, 0)),
                  pl.BlockSpec((1, s, LANES), lambda i, j: (i, 0, 0)),
                  pl.BlockSpec((1, tq, LANES), lambda i, j: (i, j, NKV_IN_W // LANES - 1)),
                  _full((nc, LANES)), _full((LANES, s))],
        out_specs=pl.BlockSpec((1, tq, GROUP_W), lambda i, j: (i, j, 0)),
        out_shape=jax.ShapeDtypeStruct((b, s, GROUP_W), F32),
        scratch_shapes=[pltpu.VMEM((2, rows, s), F32), pltpu.VMEM((2, rows, s), BF16),
                        pltpu.VMEM((2, rows, LANES), F32), pltpu.VMEM((3, rows, NSA_DK), F32),
                        pltpu.VMEM((rows, LANES), BF16)],
        compiler_params=_cparams(("parallel", "arbitrary")),
        name="nsa_attn",
    )(q, kc, vc, ks, vs, kw, vw, nkv, ov, ex)


def _prep_nsa(g_q, g_k, pe_k, w1_k, w2_k, pe_v, w1_v, w2_v):
    return dict(g_q=_pad_last(g_q, LANES)[None, :], g_k=_pad_last(g_k, LANES),
                pe_k=pe_k.reshape(1, CMP_LEN * NSA_DK), w1_k=w1_k.astype(BF16),
                w2_k=_pad_last(w2_k, LANES).astype(BF16),
                pe_v=pe_v.reshape(1, CMP_LEN * NSA_DK), w1_v=w1_v.astype(BF16),
                w2_v=w2_v.astype(BF16))


def _outproj_kernel(ya_ref, yb_ref, yc_ref, yd_ref, x_ref, g_ref, w_ref, o_ref):
    acc = x_ref[...]
    for i, y_ref in enumerate((ya_ref, yb_ref, yc_ref, yd_ref)):
        y = _rms(y_ref[...], g_ref[i:i + 1, :], GROUP_W)
        acc = acc + _dot(y, w_ref[GROUP_W * i:GROUP_W * (i + 1), :])
    o_ref[...] = acc


def _outproj(ys, x2, g, w, tm=512):
    t = x2.shape[0]
    yspec = pl.BlockSpec((tm, GROUP_W), lambda i: (i, 0))
    return pl.pallas_call(
        _outproj_kernel,
        grid=(t // tm,),
        in_specs=[yspec, yspec, yspec, yspec, pl.BlockSpec((tm, D_MODEL), lambda i: (i, 0)),
                  _full((4, GROUP_W)), _full((D_MODEL, D_MODEL))],
        out_specs=pl.BlockSpec((tm, D_MODEL), lambda i: (i, 0)),
        out_shape=jax.ShapeDtypeStruct((t, D_MODEL), F32),
        compiler_params=_cparams(("parallel",)),
        name="outproj",
    )(*ys, x2, g, w)


def _route(logits):
    tm = logits.shape[0]
    lane = lax.broadcasted_iota(jnp.int32, (tm, LANES), 1).astype(F32)
    far = float(LANES)
    is_g = lane < N_GROUPS
    lg = jnp.where(is_g, logits, -jnp.inf)
    gmax = jnp.max(lg, axis=-1, keepdims=True)
    gi = jnp.min(jnp.where(is_g & (lg == gmax), lane, far), axis=-1, keepdims=True)
    pg_top = 1.0 / jnp.sum(jnp.exp(lg - gmax), axis=-1, keepdims=True)
    e_id = lane - N_GROUPS
    in_g = (e_id >= gi * EXP_PER_GROUP) & (e_id < (gi + 1.0) * EXP_PER_GROUP)
    le = jnp.where(in_g, logits, -jnp.inf)
    m1 = jnp.max(le, axis=-1, keepdims=True)
    i1 = jnp.min(jnp.where(in_g & (le == m1), lane, far), axis=-1, keepdims=True)
    le2 = jnp.where(lane == i1, -jnp.inf, le)
    m2 = jnp.max(le2, axis=-1, keepdims=True)
    i2 = jnp.min(jnp.where(le2 == m2, lane, far), axis=-1, keepdims=True)
    v2 = jnp.exp(m2 - m1)
    w1 = pg_top / (1.0 + v2)
    w2 = pg_top * v2 / (1.0 + v2)
    comb = jnp.where(lane == i1, w1, 0.0) + jnp.where(lane == i2, w2, 0.0)
    return pltpu.roll(comb, LANES - N_GROUPS, axis=1)


def _moe_kernel(x_ref, g_ref, wr_hi_ref, wr_lo_ref, br_ref, ex_ref, wg_ref, wu_ref, wd_ref, o_ref,
                hn, comb):
    j = pl.program_id(1)

    @pl.when(j == 0)
    def _():
        x = x_ref[...]
        h = x * lax.rsqrt(jnp.mean(x * x, axis=-1, keepdims=True) + EPS) * g_ref[...]
        hn[...] = h.astype(BF16)
        h_hi = h.astype(BF16)
        h_lo = (h - h_hi.astype(F32)).astype(BF16)
        logits = (jnp.dot(h_hi, wr_hi_ref[...], preferred_element_type=F32)
                  + jnp.dot(h_hi, wr_lo_ref[...], preferred_element_type=F32)
                  + jnp.dot(h_lo, wr_hi_ref[...], preferred_element_type=F32)) + br_ref[...]
        comb[...] = _route(logits)
        o_ref[...] = x

    hb = hn[...]
    a = jax.nn.silu(jnp.dot(hb, wg_ref[...], preferred_element_type=F32)) \
        * jnp.dot(hb, wu_ref[...], preferred_element_type=F32)
    a = a * _dot_f32_by_exact(comb[...], ex_ref[...])
    o_ref[...] += jnp.dot(a.astype(BF16), wd_ref[...], preferred_element_type=F32)


def _moe(x2, p, tm=1024, ec=2):
    t = x2.shape[0]
    cw = ec * D_EXPERT
    return pl.pallas_call(
        _moe_kernel,
        grid=(t // tm, N_EXPERTS // ec),
        in_specs=[pl.BlockSpec((tm, D_MODEL), lambda i, j: (i, 0)),
                  _full((1, D_MODEL)), _full((D_MODEL, LANES)), _full((D_MODEL, LANES)),
                  _full((1, LANES)),
                  pl.BlockSpec((LANES, cw), lambda i, j: (0, j)),
                  pl.BlockSpec((D_MODEL, cw), lambda i, j: (0, j)),
                  pl.BlockSpec((D_MODEL, cw), lambda i, j: (0, j)),
                  pl.BlockSpec((cw, D_MODEL), lambda i, j: (j, 0))],
        out_specs=pl.BlockSpec((tm, D_MODEL), lambda i, j: (i, 0)),
        out_shape=jax.ShapeDtypeStruct((t, D_MODEL), F32),
        scratch_shapes=[pltpu.VMEM((tm, D_MODEL), BF16), pltpu.VMEM((tm, LANES), F32)],
        compiler_params=_cparams(("parallel", "arbitrary")),
        name="moe",
    )(x2, p["g"], p["wr_hi"], p["wr_lo"], p["br"], p["ex"], p["wg"], p["wu"], p["wd"])


def _prep_moe(g, w_rg, b_rg, w_re, b_re, w_gate, w_up, w_down):
    wr = _pad_last(jnp.concatenate([w_rg, w_re], axis=1), LANES)
    wr_hi = wr.astype(BF16)
    wr_lo = (wr - wr_hi.astype(F32)).astype(BF16)
    br = _pad_last(jnp.concatenate([b_rg, b_re]), LANES)[None, :]
    ex = np.zeros((LANES, N_EXPERTS * D_EXPERT), np.float32)
    ex[np.arange(N_EXPERTS * D_EXPERT) // D_EXPERT, np.arange(N_EXPERTS * D_EXPERT)] = 1.0
    cat = lambda w: w.transpose(1, 0, 2).reshape(D_MODEL, N_EXPERTS * D_EXPERT).astype(BF16)
    return dict(g=g[None, :], wr_hi=wr_hi, wr_lo=wr_lo, br=br, ex=jnp.asarray(ex, BF16),
                wg=cat(w_gate), wu=cat(w_up),
                wd=w_down.reshape(N_EXPERTS * D_EXPERT, D_MODEL).astype(BF16))


def kernel(x, positions, mix_norm, w_in, mla_g_cq, mla_g_ckv, mla_w_uq, mla_w_ukv, mla_g_q, mla_g_k, lru_conv_w, lru_conv_b, lru_w_a, lru_b_a, lru_w_i, lru_b_i, lru_lambda, s5_a_re, s5_a_im, s5_log_dt, s5_b_re, s5_b_im, s5_c_re, s5_c_im, s5_d, s5_w_glu, s5_b_glu, nsa_g_q, nsa_g_k, nsa_pe_k, nsa_w1_k, nsa_w2_k, nsa_pe_v, nsa_w1_v, nsa_w2_v, out_norm, w_out, ffn_norm, moe_w_rg, moe_b_rg, moe_w_re, moe_b_re, moe_w_gate, moe_w_up, moe_w_down):
    b, s, d = x.shape
    t = b * s
    pos3 = positions.astype(jnp.int32)[:, :, None]
    x2 = x.reshape(t, d)
    for l in range(w_in.shape[0]):
        o_mla, o_lru, o_s5, o_nq, o_nkv = _inproj(x2, mix_norm[l], _prep_w_in(w_in[l]))
        y_a = _mla(o_mla.reshape(b, s, MLA_IN_W), pos3,
                   _prep_mla(mla_g_cq[l], mla_g_ckv[l], mla_w_uq[l], mla_w_ukv[l], mla_g_q[l], mla_g_k[l]))
        y_b = _lru(o_lru.reshape(b, s, LRU_IN_W),
                   _prep_lru(lru_conv_w[l], lru_conv_b[l], lru_w_a[l], lru_b_a[l], lru_w_i[l], lru_b_i[l],
                             lru_lambda[l]))
        u_tm = o_s5.reshape(b, s, S5_W).transpose(1, 0, 2)
        y_c = _s5(u_tm, _prep_s5(s5_a_re[l], s5_a_im[l], s5_log_dt[l], s5_b_re[l], s5_b_im[l], s5_c_re[l],
                                 s5_c_im[l], s5_d[l], s5_w_glu[l], s5_b_glu[l])).transpose(1, 0, 2)
        y_d = _nsa(o_nq.reshape(b, s, NQ_IN_W), o_nkv.reshape(b, s, NKV_IN_W), pos3,
                   _prep_nsa(nsa_g_q[l], nsa_g_k[l], nsa_pe_k[l], nsa_w1_k[l], nsa_w2_k[l], nsa_pe_v[l],
                             nsa_w1_v[l], nsa_w2_v[l]))
        ys = [y.reshape(t, GROUP_W) for y in (y_a, y_b, y_c, y_d)]
        x2 = _outproj(ys, x2, out_norm[l], w_out[l].astype(BF16))
        x2 = _moe(x2, _prep_moe(ffn_norm[l], moe_w_rg[l], moe_b_rg[l], moe_w_re[l], moe_b_re[l],
                                moe_w_gate[l], moe_w_up[l], moe_w_down[l]))
    return x2.reshape(b, s, d)
```

```python
import functools
import math

import numpy as np
import jax
import jax.numpy as jnp
from jax import lax
from jax.experimental import pallas as pl
from jax.experimental.pallas import tpu as pltpu

F32 = jnp.float32
BF16 = jnp.bfloat16

D_MODEL = 1024
DEPTH = 2
GROUP_W = 256
EPS = 1e-6
ROPE_THETA = 500000.0
NEG = -1e30
LOG2E = math.log2(math.e)

MLA_HEADS = 4
MLA_ROPE = 32
MLA_NOPE = 64
MLA_V = 64
MLA_QK = 96
MLA_Q_RANK = 192
MLA_KV_RANK = 128

LRU_W = 256
LRU_BLOCKS = 4
LRU_BW = 64
CONV_W = 4
LRU_C = 8.0

S5_W = 256
S5_CH = 16
S5_GROUPS = 16
S5_P = 64
S5_STATE = S5_GROUPS * S5_P

NSA_HEADS = 4
NSA_DK = 64
NSA_ROT = 16
CMP_LEN = 32
CMP_STRIDE = 16
CMP_HID = 128
SEL_LEN = 64
SEL_TOPK = 5
WIN = 512

N_GROUPS = 4
EXP_PER_GROUP = 4
N_EXPERTS = 16
D_EXPERT = 256

D_IN = 1772

LANES = 128
VMEM_LIMIT = 48 * 1024 * 1024

MLA_IN_W = 512
LRU_IN_W = 512
S5_IN_W = 256
NQ_IN_W = NSA_HEADS * LANES
NKV_IN_W = 6 * LANES
IN_W = MLA_IN_W + LRU_IN_W + S5_IN_W + NQ_IN_W + NKV_IN_W


def _cparams(sem):
    return pltpu.CompilerParams(dimension_semantics=sem, vmem_limit_bytes=VMEM_LIMIT)


def _dot(a, b):
    return jnp.dot(a.astype(BF16), b.astype(BF16), preferred_element_type=F32)


def _dot_nt(a, b):
    return lax.dot_general(a.astype(BF16), b.astype(BF16), (((1,), (1,)), ((), ())),
                           preferred_element_type=F32)


def _split3(x):
    hi = x.astype(BF16)
    r = x - hi.astype(F32)
    mid = r.astype(BF16)
    lo = (r - mid.astype(F32)).astype(BF16)
    return hi, mid, lo


def _dot_f32_by_exact(x, w_bf16):
    hi, mid, lo = _split3(x)
    return (jnp.dot(hi, w_bf16, preferred_element_type=F32)
            + jnp.dot(mid, w_bf16, preferred_element_type=F32)
            + jnp.dot(lo, w_bf16, preferred_element_type=F32))


def _rms(x, g, n):
    return x * lax.rsqrt(jnp.sum(x * x, axis=-1, keepdims=True) * (1.0 / n) + EPS) * g


def _gelu(x):
    return 0.5 * x * (1.0 + jnp.tanh(math.sqrt(2.0 / math.pi) * (x + 0.044715 * (x * x * x))))


def _rope(x, cos, sin_lo, sin_hi, half):
    return (x * cos + pltpu.roll(x, LANES - half, axis=1) * sin_lo
            + pltpu.roll(x, half, axis=1) * sin_hi)


def _with_ones(x, upper):
    if upper:
        x = pltpu.roll(x, LANES // 2, axis=1)
    lane = lax.broadcasted_iota(jnp.int32, x.shape, 1)
    return jnp.where(lane < LANES // 2, x, 1.0).astype(BF16)


NSA_TRIG_LANE0 = MLA_ROPE


def _trig_kernel(pos_ref, inv_ref, cos_ref, sin_ref):
    ang = pos_ref[0].astype(F32) * inv_ref[...]
    cos_ref[0] = jnp.cos(ang)
    sin_ref[0] = jnp.sin(ang)


def _trig(pos3, ts=512):
    def inv(rot):
        v = ROPE_THETA ** (-jnp.arange(rot // 2, dtype=F32) * 2.0 / rot)
        return jnp.concatenate([v, v])
    inv_l = _pad_last(jnp.concatenate([inv(MLA_ROPE), inv(NSA_ROT)]), LANES)[None, :]
    b, s, _ = pos3.shape
    spec = pl.BlockSpec((1, ts, LANES), lambda i, j: (i, j, 0))
    return pl.pallas_call(
        _trig_kernel,
        grid=(b, s // ts),
        in_specs=[pl.BlockSpec((1, ts, 1), lambda i, j: (i, j, 0)), _full((1, LANES))],
        out_specs=[spec, spec],
        out_shape=[jax.ShapeDtypeStruct((b, s, LANES), F32)] * 2,
        compiler_params=_cparams(("parallel", "parallel")),
        name="rope_trig",
    )(pos3, inv_l)


def _rope_tables(cs, sn, lane0, half):
    if lane0:
        cs = pltpu.roll(cs, LANES - lane0, axis=1)
        sn = pltpu.roll(sn, LANES - lane0, axis=1)
    lane = lax.broadcasted_iota(jnp.int32, cs.shape, 1)
    cos = jnp.where(lane < 2 * half, cs, 1.0)
    s_lo = jnp.where(lane < half, -sn, 0.0)
    s_hi = jnp.where((lane >= half) & (lane < 2 * half), sn, 0.0)
    return cos, s_lo, s_hi


def _pad_last(a, n):
    return jnp.pad(a, [(0, 0)] * (a.ndim - 1) + [(0, n - a.shape[-1])])


def _full(shape):
    nd = len(shape)
    return pl.BlockSpec(shape, lambda *_: (0,) * nd)


def _inproj_kernel(x_ref, g_ref, w_ref, o_mla, o_lru, o_s5, o_nq, o_nkv):
    x = x_ref[...]
    h = x * lax.rsqrt(jnp.mean(x * x, axis=-1, keepdims=True) + EPS) * g_ref[...]
    y = jnp.dot(h.astype(BF16), w_ref[...], preferred_element_type=F32)
    c0 = 0
    for o in (o_mla, o_lru, o_s5, o_nq, o_nkv):
        w = o.shape[-1]
        o[...] = y[:, c0:c0 + w]
        c0 += w


def _inproj_cols():
    src = -np.ones((IN_W,), np.int64)
    o = 0
    src[o:o + 192] = np.arange(0, 192)
    src[o + 256:o + 384] = np.arange(192, 320)
    src[o + 384:o + 416] = np.arange(320, 352)
    o += MLA_IN_W
    src[o:o + 512] = np.arange(352, 864)
    o += LRU_IN_W
    src[o:o + 256] = np.arange(864, 1120)
    o += S5_IN_W
    for h in range(NSA_HEADS):
        src[o + LANES * h:o + LANES * h + 64] = np.arange(1120 + 64 * h, 1120 + 64 * h + 64)
    o += NQ_IN_W
    kv0 = 1376
    src[o:o + 64] = np.arange(kv0, kv0 + 64)
    src[o + 128:o + 192] = np.arange(kv0 + 128, kv0 + 192)
    src[o + 256:o + 320] = np.arange(kv0 + 256, kv0 + 320)
    src[o + 384:o + 448] = np.arange(kv0 + 64, kv0 + 128)
    src[o + 448:o + 512] = np.arange(kv0 + 192, kv0 + 256)
    src[o + 512:o + 576] = np.arange(kv0 + 320, kv0 + 384)
    src[o + 640:o + 652] = np.arange(1760, 1772)
    return src


_INPROJ_SRC = _inproj_cols()


def _prep_w_in(w_in):
    idx = jnp.asarray(np.maximum(_INPROJ_SRC, 0), jnp.int32)
    keep = jnp.asarray(_INPROJ_SRC >= 0)
    return jnp.where(keep[None, :], jnp.take(w_in, idx, axis=1), 0.0).astype(BF16)


def _inproj(x2, g, w_pad, tm=512):
    t = x2.shape[0]
    widths = (MLA_IN_W, LRU_IN_W, S5_IN_W, NQ_IN_W, NKV_IN_W)
    return pl.pallas_call(
        _inproj_kernel,
        grid=(t // tm,),
        in_specs=[pl.BlockSpec((tm, D_MODEL), lambda i: (i, 0)),
                  _full((1, D_MODEL)),
                  _full((D_MODEL, IN_W))],
        out_specs=[pl.BlockSpec((tm, w), lambda i: (i, 0)) for w in widths],
        out_shape=[jax.ShapeDtypeStruct((t, w), F32) for w in widths],
        compiler_params=_cparams(("parallel",)),
        name="inproj",
    )(x2, g[None, :], w_pad)


def _mla_prep_kernel(in_ref, cs_ref, sn_ref, gcq_ref, wuq_ref, gckv_ref, wk_ref, wv_ref, gq_ref, gk_ref,
                     q_out, k_out, v_out):
    xin = in_ref[0]
    cq = _rms(xin[:, 0:256], gcq_ref[...], MLA_Q_RANK)
    ckv = _rms(xin[:, 256:384], gckv_ref[...], MLA_KV_RANK)
    kpe = xin[:, 384:512]
    q = _dot(cq, wuq_ref[...])
    kn = _dot(ckv, wk_ref[...])
    v = _dot(ckv, wv_ref[...])
    cos, s_lo, s_hi = _rope_tables(cs_ref[0], sn_ref[0], 0, MLA_ROPE // 2)
    scale = MLA_QK ** -0.5 * LOG2E
    for h in range(MLA_HEADS):
        qh = _rms(q[:, LANES * h:LANES * (h + 1)], gq_ref[...], MLA_QK)
        q_out[0, h] = (_rope(qh, cos, s_lo, s_hi, MLA_ROPE // 2) * scale).astype(BF16)
        kh = _rms(kn[:, LANES * h:LANES * (h + 1)] + kpe, gk_ref[...], MLA_QK)
        k_out[0, h] = _rope(kh, cos, s_lo, s_hi, MLA_ROPE // 2).astype(BF16)
        v_out[0, h] = _with_ones(v[:, LANES * (h // 2):LANES * (h // 2 + 1)], h % 2 == 1)


def _attend(q, segs, v1, s_scr, p_scr, m_scr, nh, chunk):
    rows = q.shape[0]
    c0 = 0
    m = None
    for k, bias in segs:
        n = k.shape[0]
        sb = _dot_nt(q, k)
        if bias is not None:
            sb = (sb.reshape(nh, rows // nh, n) + bias[None]).reshape(rows, n)
        s_scr[0:rows, c0:c0 + n] = sb
        mx = jnp.max(sb, axis=-1, keepdims=True)
        m = mx if m is None else jnp.maximum(m, mx)
        c0 += n
    m_scr[0:rows, :] = jnp.broadcast_to(m, (rows, LANES))
    for r0 in range(0, rows, chunk):
        mb = m_scr[r0:r0 + chunk, :]
        for j0 in range(0, c0, LANES):
            p = jnp.exp2(s_scr[r0:r0 + chunk, j0:j0 + LANES] - mb)
            p_scr[r0:r0 + chunk, j0:j0 + LANES] = p.astype(BF16)
    o = jnp.dot(p_scr[0:rows, 0:c0], v1, preferred_element_type=F32)
    return (o * (1.0 / pltpu.roll(o, LANES // 2, axis=1)))[:, 0:LANES // 2]


def _mla_attn_kernel(q_ref, k_ref, v_ref, o_ref, s_scr, p_scr, m_scr, o_scr, bias_scr, *, tq, chunk):
    qi = pl.program_id(1)
    row = lax.broadcasted_iota(jnp.int32, (tq, tq), 0)
    col = lax.broadcasted_iota(jnp.int32, (tq, tq), 1)
    bias_scr[...] = jnp.where(col <= row, 0.0, NEG)
    nset = s_scr.shape[0]
    for c in range(k_ref.shape[2] // tq):

        @pl.when(qi == c)
        def _(c=c):
            n = (c + 1) * tq

            def heads(i, carry):
                for j in range(nset):
                    h = i * nset + j
                    segs = [(k_ref[0, h, n - tq:n, :], bias_scr[...])]
                    if c > 0:
                        segs = [(k_ref[0, h, 0:n - tq, :], None)] + segs
                    o_scr[h] = _attend(q_ref[0, h], segs, v_ref[0, h, 0:n, :], s_scr.at[j], p_scr.at[j],
                                       m_scr.at[j], 1, chunk)
                return carry

            lax.fori_loop(0, MLA_HEADS // nset, heads, 0)

    for h in range(MLA_HEADS):
        o_ref[0, :, MLA_V * h:MLA_V * (h + 1)] = o_scr[h]


def _mla(mla_in, trig, p, ts=512, tq=512, chunk=128, nset=2):
    b, s, _ = mla_in.shape
    hq = (b, MLA_HEADS, s, LANES)
    q, k, v = pl.pallas_call(
        _mla_prep_kernel,
        grid=(b, s // ts),
        in_specs=[pl.BlockSpec((1, ts, MLA_IN_W), lambda i, j: (i, j, 0)),
                  pl.BlockSpec((1, ts, LANES), lambda i, j: (i, j, 0)),
                  pl.BlockSpec((1, ts, LANES), lambda i, j: (i, j, 0)),
                  _full((1, 256)), _full((256, 512)), _full((1, 128)), _full((128, 512)),
                  _full((128, 256)), _full((1, 128)), _full((1, 128))],
        out_specs=[pl.BlockSpec((1, MLA_HEADS, ts, LANES), lambda i, j: (i, 0, j, 0)),
                   pl.BlockSpec((1, MLA_HEADS, ts, LANES), lambda i, j: (i, 0, j, 0)),
                   pl.BlockSpec((1, MLA_HEADS, ts, LANES), lambda i, j: (i, 0, j, 0))],
        out_shape=[jax.ShapeDtypeStruct(hq, BF16), jax.ShapeDtypeStruct(hq, BF16),
                   jax.ShapeDtypeStruct(hq, BF16)],
        compiler_params=_cparams(("parallel", "parallel")),
        name="mla_prep",
    )(mla_in, trig[0], trig[1], p["g_cq"], p["w_uq"], p["g_ckv"], p["w_k"], p["w_v"], p["g_q"], p["g_k"])
    return pl.pallas_call(
        functools.partial(_mla_attn_kernel, tq=tq, chunk=chunk),
        grid=(b, s // tq),
        in_specs=[pl.BlockSpec((1, MLA_HEADS, tq, LANES), lambda i, j: (i, 0, j, 0)),
                  pl.BlockSpec((1, MLA_HEADS, s, LANES), lambda i, j: (i, 0, 0, 0)),
                  pl.BlockSpec((1, MLA_HEADS, s, LANES), lambda i, j: (i, 0, 0, 0))],
        out_specs=pl.BlockSpec((1, tq, GROUP_W), lambda i, j: (i, j, 0)),
        out_shape=jax.ShapeDtypeStruct((b, s, GROUP_W), F32),
        scratch_shapes=[pltpu.VMEM((nset, tq, s), F32), pltpu.VMEM((nset, tq, s), BF16),
                        pltpu.VMEM((nset, tq, LANES), F32),
                        pltpu.VMEM((MLA_HEADS, tq, MLA_V), F32), pltpu.VMEM((tq, tq), F32)],
        compiler_params=_cparams(("parallel", "arbitrary")),
        name="mla_attn",
    )(q, k, v)


def _prep_mla(g_cq, g_ckv, w_uq, w_ukv, g_q, g_k):
    wq = w_uq.reshape(MLA_Q_RANK, MLA_HEADS, MLA_QK)
    wq = _pad_last(wq, LANES).reshape(MLA_Q_RANK, MLA_HEADS * LANES)
    wq = jnp.pad(wq, ((0, 256 - MLA_Q_RANK), (0, 0)))
    wkv = w_ukv.reshape(MLA_KV_RANK, MLA_HEADS, MLA_NOPE + MLA_V)
    wk = jnp.pad(wkv[:, :, :MLA_NOPE], ((0, 0), (0, 0), (MLA_ROPE, LANES - MLA_QK)))
    wk = wk.reshape(MLA_KV_RANK, MLA_HEADS * LANES)
    wv = wkv[:, :, MLA_NOPE:].reshape(MLA_KV_RANK, MLA_HEADS * MLA_V)
    return dict(g_cq=_pad_last(g_cq, 256)[None, :], g_ckv=g_ckv[None, :], w_uq=wq.astype(BF16),
                w_k=wk.astype(BF16), w_v=wv.astype(BF16), g_q=_pad_last(g_q, LANES)[None, :],
                g_k=_pad_last(g_k, LANES)[None, :])


def _shift_rows(x, k, row, fill):
    return jnp.where(row >= k, pltpu.roll(x, k, axis=0), fill)


def _lru_kernel(in_ref, cw_ref, cb_ref, wa_ref, ba_ref, wi_ref, bi_ref, lam_ref, o_ref, a_scr, b_scr, *,
                chunk):
    xin = in_ref[0]
    s = xin.shape[0]
    xb = xin[:, :LRU_W]
    row = lax.broadcasted_iota(jnp.int32, (s, LRU_W), 0)
    u = cb_ref[...] + xb * cw_ref[CONV_W - 1:CONV_W, :]
    for j in range(CONV_W - 1):
        u = u + _shift_rows(xb, CONV_W - 1 - j, row, 0.0) * cw_ref[j:j + 1, :]
    r = jax.nn.sigmoid(_dot(u, wa_ref[...]) + ba_ref[...])
    gi = jax.nn.sigmoid(_dot(u, wi_ref[...]) + bi_ref[...])
    nlam = -lam_ref[...]
    softplus = jnp.maximum(nlam, 0.0) + jnp.log1p(jnp.exp(-jnp.abs(nlam)))
    log_a = (-LRU_C) * r * softplus
    a = jnp.exp(log_a)
    y = jnp.maximum(-jnp.tanh(log_a) * (a * a + 1.0), 0.0)
    mult = jnp.where(y > 0.0, y * lax.rsqrt(y), 0.0)
    mult = jnp.where(row == 0, 1.0, mult)
    bt = mult * gi * u
    rin = jnp.bitwise_and(row, chunk - 1)
    k = 1
    while k < chunk:
        bt = a * _shift_rows(bt, k, rin, 0.0) + bt
        a = a * _shift_rows(a, k, rin, 1.0)
        k *= 2
    a_scr[...] = a
    b_scr[...] = bt

    def carry_chunk(j, h_prev):
        r0 = pl.multiple_of(j * chunk, chunk)
        h = b_scr[pl.ds(r0, chunk), :] + a_scr[pl.ds(r0, chunk), :] * h_prev
        o_ref[0, pl.ds(r0, chunk), :] = h * _gelu(in_ref[0, pl.ds(r0, chunk), LRU_W:])
        return h[chunk - 1:chunk, :]

    lax.fori_loop(0, s // chunk, carry_chunk, jnp.zeros((1, LRU_W), F32))


def _block_diag(w):
    n, i, j = w.shape
    eye = jnp.eye(n, dtype=w.dtype)
    return (eye[:, None, :, None] * w[:, :, None, :]).reshape(n * i, n * j)


def _lru(lru_in, p, chunk=64):
    b, s, _ = lru_in.shape
    return pl.pallas_call(
        functools.partial(_lru_kernel, chunk=chunk),
        grid=(b,),
        in_specs=[pl.BlockSpec((1, s, LRU_IN_W), lambda i: (i, 0, 0)),
                  _full((CONV_W, LRU_W)), _full((1, LRU_W)), _full((LRU_W, LRU_W)), _full((1, LRU_W)),
                  _full((LRU_W, LRU_W)), _full((1, LRU_W)), _full((1, LRU_W))],
        out_specs=pl.BlockSpec((1, s, LRU_W), lambda i: (i, 0, 0)),
        out_shape=jax.ShapeDtypeStruct((b, s, LRU_W), F32),
        scratch_shapes=[pltpu.VMEM((s, LRU_W), F32), pltpu.VMEM((s, LRU_W), F32)],
        compiler_params=_cparams(("parallel",)),
        name="rglru",
    )(lru_in, p["cw"], p["cb"], p["wa"], p["ba"], p["wi"], p["bi"], p["lam"])


def _prep_lru(conv_w, conv_b, w_a, b_a, w_i, b_i, lam):
    return dict(cw=conv_w, cb=conv_b[None, :], wa=_block_diag(w_a).astype(BF16),
                ba=b_a.reshape(1, LRU_W), wi=_block_diag(w_i).astype(BF16), bi=b_i.reshape(1, LRU_W),
                lam=lam[None, :])


def _s5_disc_kernel(are_ref, aim_ref, ldt_ref, arer_ref, aimr_ref, bre_ref, bim_ref,
                    abre_ref, abim_ref, bbre_ref, bbim_ref):
    dt = jnp.exp(ldt_ref[...])

    def disc(a_re, a_im):
        mag = jnp.exp(dt * a_re)
        ab_re = mag * jnp.cos(dt * a_im)
        ab_im = mag * jnp.sin(dt * a_im)
        den = a_re * a_re + a_im * a_im
        n_re = ab_re - 1.0
        g_re = (n_re * a_re + ab_im * a_im) / den
        g_im = (ab_im * a_re - n_re * a_im) / den
        return ab_re, ab_im, g_re, g_im

    ab_re, ab_im, _, _ = disc(are_ref[...], aim_ref[...])
    abre_ref[...] = ab_re
    abim_ref[...] = ab_im
    _, _, g_re, g_im = disc(arer_ref[...], aimr_ref[...])
    bbre_ref[...] = g_re * bre_ref[...] - g_im * bim_ref[...]
    bbim_ref[...] = g_re * bim_ref[...] + g_im * bre_ref[...]


def _s5_kernel(u_ref, are_ref, aim_ref, bre_ref, bim_ref, cre_ref, cim_ref, d_ref, wg_ref, bg_ref,
               o_ref, hre, him, st_re, st_im, *, tc, nb, cw):
    @pl.when(pl.program_id(0) == 0)
    def _():
        st_re[...] = jnp.zeros_like(st_re)
        st_im[...] = jnp.zeros_like(st_im)

    u = u_ref[...].reshape(tc * nb, S5_W)
    ub = u.astype(BF16)
    hre[...] = jnp.dot(ub, bre_ref[...], preferred_element_type=F32)
    him[...] = jnp.dot(ub, bim_ref[...], preferred_element_type=F32)
    for c in range(S5_STATE // cw):
        cs = slice(c * cw, (c + 1) * cw)
        ar = jnp.broadcast_to(are_ref[:, cs], (nb, cw))
        ai = jnp.broadcast_to(aim_ref[:, cs], (nb, cw))

        def body(t, carry, cs=cs, ar=ar, ai=ai):
            hr, hi = carry
            r0 = pl.multiple_of(t * nb, nb)
            nr = ar * hr - ai * hi + hre[pl.ds(r0, nb), cs]
            ni = ar * hi + ai * hr + him[pl.ds(r0, nb), cs]
            hre[pl.ds(r0, nb), cs] = nr
            him[pl.ds(r0, nb), cs] = ni
            return nr, ni

        hr, hi = lax.fori_loop(0, tc, body, (st_re[:, cs], st_im[:, cs]), unroll=4)
        st_re[:, cs] = hr
        st_im[:, cs] = hi
    y = (jnp.dot(hre[...].astype(BF16), cre_ref[...], preferred_element_type=F32)
         - jnp.dot(him[...].astype(BF16), cim_ref[...], preferred_element_type=F32))
    y = _gelu(y + d_ref[...] * u)
    z = _dot(y, wg_ref[...]) + bg_ref[...]
    o_ref[...] = (y * jax.nn.sigmoid(z)).reshape(tc, nb, S5_W)


def _prep_s5(a_re, a_im, log_dt, b_re, b_im, c_re, c_im, d, w_glu, b_glu):
    g, p, ch = S5_GROUPS, S5_P, S5_CH
    ab_re, ab_im, bb_re, bb_im = pl.pallas_call(
        _s5_disc_kernel,
        out_shape=[jax.ShapeDtypeStruct((g, p), F32), jax.ShapeDtypeStruct((g, p), F32),
                   jax.ShapeDtypeStruct((g, p * ch), F32), jax.ShapeDtypeStruct((g, p * ch), F32)],
        name="s5_discretize",
    )(a_re, a_im, log_dt[:, None], jnp.repeat(a_re, ch, axis=1), jnp.repeat(a_im, ch, axis=1),
      b_re.reshape(g, p * ch), b_im.reshape(g, p * ch))

    def b_dense(bb):
        return _block_diag(bb.reshape(g, p, ch).transpose(0, 2, 1)).astype(BF16)

    def c_dense(c):
        return _block_diag(c.transpose(0, 2, 1)).astype(BF16)

    return dict(a_re=ab_re.reshape(1, S5_STATE), a_im=ab_im.reshape(1, S5_STATE),
                b_re=b_dense(bb_re), b_im=b_dense(bb_im), c_re=c_dense(c_re), c_im=c_dense(c_im),
                d=d[None, :], w_glu=w_glu.astype(BF16), b_glu=b_glu[None, :])


def _s5(u_tm, p, tc=64, cw=256):
    s, nb, _ = u_tm.shape
    return pl.pallas_call(
        functools.partial(_s5_kernel, tc=tc, nb=nb, cw=cw),
        grid=(s // tc,),
        in_specs=[pl.BlockSpec((tc, nb, S5_W), lambda i: (i, 0, 0)),
                  _full((1, S5_STATE)), _full((1, S5_STATE)),
                  _full((S5_W, S5_STATE)), _full((S5_W, S5_STATE)),
                  _full((S5_STATE, S5_W)), _full((S5_STATE, S5_W)),
                  _full((1, S5_W)), _full((S5_W, S5_W)), _full((1, S5_W))],
        out_specs=pl.BlockSpec((tc, nb, S5_W), lambda i: (i, 0, 0)),
        out_shape=jax.ShapeDtypeStruct((s, nb, S5_W), F32),
        scratch_shapes=[pltpu.VMEM((tc * nb, S5_STATE), F32), pltpu.VMEM((tc * nb, S5_STATE), F32),
                        pltpu.VMEM((nb, S5_STATE), F32), pltpu.VMEM((nb, S5_STATE), F32)],
        compiler_params=_cparams(("arbitrary",)),
        name="s5",
    )(u_tm, p["a_re"], p["a_im"], p["b_re"], p["b_im"], p["c_re"], p["c_im"], p["d"], p["w_glu"],
      p["b_glu"])


def _nsa_prep_kernel(nq_ref, nkv_ref, cs_ref, sn_ref, gq_ref, gk_ref,
                     q_out, ks_out, kw_out, vs_out, vw_out):
    half = NSA_ROT // 2
    cos, s_lo, s_hi = _rope_tables(cs_ref[0], sn_ref[0], NSA_TRIG_LANE0, half)
    nq = nq_ref[0]
    scale = NSA_DK ** -0.5 * LOG2E
    for h in range(NSA_HEADS):
        qh = _rms(nq[:, LANES * h:LANES * (h + 1)], gq_ref[...], NSA_DK)
        q_out[0, h] = (_rope(qh, cos, s_lo, s_hi, half) * scale).astype(BF16)
    nkv = nkv_ref[0]
    ks = _rms(nkv[:, 128:256], gk_ref[1:2, :], NSA_DK)
    ks_out[0] = _rope(ks, cos, s_lo, s_hi, half).astype(BF16)
    kw = _rms(nkv[:, 256:384], gk_ref[2:3, :], NSA_DK)
    kw_out[0] = _rope(kw, cos, s_lo, s_hi, half).astype(BF16)
    vs_out[0] = _with_ones(nkv[:, 384:512], True)
    vw_out[0] = _with_ones(nkv[:, 512:640], False)


def _nsa_cmp_kernel(gk_in, gv_in, cs_ref, sn_ref, pek_ref, w1k_ref, w2k_ref, pev_ref, w1v_ref, w2v_ref,
                    g_ref, kc_out, vc_out):
    half_w = CMP_STRIDE * NSA_DK
    nc = gk_in.shape[1]

    def compress(g, pe_ref, w1_ref, w2_ref):
        lo = _dot(g + pe_ref[:, :half_w], w1_ref[:half_w, :])
        hi = _dot(g + pe_ref[:, half_w:], w1_ref[half_w:, :])
        hid = lo + pltpu.roll(hi, nc - 1, axis=0)
        return _dot(_gelu(hid), w2_ref[...])

    kc = compress(gk_in[0], pek_ref, w1k_ref, w2k_ref)
    cos, s_lo, s_hi = _rope_tables(cs_ref[0], sn_ref[0], NSA_TRIG_LANE0, NSA_ROT // 2)
    kc = _rope(_rms(kc, g_ref[0:1, :], NSA_DK), cos, s_lo, s_hi, NSA_ROT // 2)
    kc_out[0] = kc.astype(BF16)
    vc_out[0] = compress(gv_in[0], pev_ref, w1v_ref, w2v_ref).astype(BF16)


def _nsa_cmpsel_kernel(q_ref, kc_ref, vc_ref, ov_ref, ocmp_ref, sel_ref, *, tc):
    qi = pl.program_id(1)
    nh = NSA_HEADS
    q = q_ref[0].reshape(nh * tc, LANES)
    lane = lax.broadcasted_iota(jnp.int32, (tc, LANES), 1)
    qpos = qi * tc + lax.broadcasted_iota(jnp.int32, (tc, LANES), 0)

    s = _dot_nt(q, kc_ref[0]).reshape(nh, tc, LANES)
    valid = (lane * CMP_STRIDE + (CMP_LEN - 1)) <= qpos
    s = jnp.where(valid, s, NEG)
    e = jnp.exp2(s - jnp.max(s, axis=-1, keepdims=True))
    p_c = jnp.where(valid, e / jnp.sum(e, axis=-1, keepdims=True), 0.0)
    ocmp_ref[0] = jnp.dot(p_c.astype(BF16).reshape(nh * tc, LANES), vc_ref[0],
                          preferred_element_type=F32).reshape(nh, tc, NSA_DK)

    imp = _dot_f32_by_exact(jnp.sum(p_c, axis=0), ov_ref[...])
    cur = qpos // SEL_LEN
    lane_f = lane.astype(F32)
    sel = (lane == 0) | (lane == cur) | (lane == cur - 1)
    cand = (lane < cur - 1) & (lane > 0)
    for _ in range(SEL_TOPK - 3):
        sc = jnp.where(cand, imp, -jnp.inf)
        best = jnp.max(sc, axis=-1, keepdims=True)
        pick = jnp.min(jnp.where(cand & (sc == best), lane_f, float(LANES)), axis=-1, keepdims=True)
        hit = lane_f == pick
        sel = sel | hit
        cand = cand & jnp.logical_not(hit)
    sel_ref[0] = jnp.where(sel, 1.0, 0.0).astype(BF16)


def _nsa_attn_kernel(q_ref, sel_ref, ocmp_ref, ks_ref, vs_ref, kw_ref, vw_ref, gate_ref, ex_ref,
                     o_ref, s_scr, p_scr, m_scr, obr_scr, *, tq, chunk, kchunk):
    qi = pl.program_id(1)
    nh = NSA_HEADS
    rows = nh * tq
    s_len = ks_ref.shape[1]

    def window():
        wk = WIN + tq
        start = pl.multiple_of(jnp.maximum(qi - WIN // tq, 0) * tq, tq)
        kpos = start + lax.broadcasted_iota(jnp.int32, (tq, wk), 1)
        qrow = qi * tq + lax.broadcasted_iota(jnp.int32, (tq, wk), 0)
        bias = jnp.where((kpos <= qrow) & (qrow - kpos < WIN), 0.0, NEG)
        obr_scr[1] = _attend(q_ref[0].reshape(rows, LANES), [(kw_ref[0, pl.ds(start, wk), :], bias)],
                             vw_ref[0, pl.ds(start, wk), :], s_scr.at[1], p_scr.at[1], m_scr.at[1],
                             nh, chunk)

    for c in range(s_len // kchunk):

        @pl.when(qi // (kchunk // tq) == c)
        def _(c=c):
            n = (c + 1) * kchunk
            em = jnp.dot(sel_ref[0], ex_ref[:, 0:n], preferred_element_type=F32)
            kpos = lax.broadcasted_iota(jnp.int32, (tq, n), 1)
            qrow = qi * tq + lax.broadcasted_iota(jnp.int32, (tq, n), 0)
            bias = jnp.where((em > 0.5) & (kpos <= qrow), 0.0, NEG)
            obr_scr[0] = _attend(q_ref[0].reshape(rows, LANES), [(ks_ref[0, 0:n, :], bias)],
                                 vs_ref[0, 0:n, :], s_scr.at[0], p_scr.at[0], m_scr.at[0], nh, chunk)
            window()

    g = jax.nn.sigmoid(gate_ref[0])
    for h in range(nh):
        r = slice(h * tq, (h + 1) * tq)
        o_ref[0, :, NSA_DK * h:NSA_DK * (h + 1)] = (g[:, 3 * h:3 * h + 1] * ocmp_ref[0, h]
                                                    + g[:, 3 * h + 1:3 * h + 2] * obr_scr[0, r, :]
                                                    + g[:, 3 * h + 2:3 * h + 3] * obr_scr[1, r, :])


def _nsa_tables(s, tq):
    nc = s // CMP_STRIDE
    nsb = s // SEL_LEN
    cs = np.arange(nc) * CMP_STRIDE
    ss = np.arange(nsb) * SEL_LEN
    ov = np.clip(np.minimum(cs[:, None] + CMP_LEN, ss[None, :] + SEL_LEN)
                 - np.maximum(cs[:, None], ss[None, :]), 0, None) / CMP_STRIDE
    ov[(s - CMP_LEN) // CMP_STRIDE + 1:] = 0.0
    ov_pad = np.zeros((nc, LANES), np.float32)
    ov_pad[:, :nsb] = ov
    ex = np.zeros((LANES, s), np.float32)
    ex[np.arange(s) // SEL_LEN, np.arange(s)] = 1.0
    return jnp.asarray(ov_pad, BF16), jnp.asarray(ex, BF16)


def _nsa(nq, nkv, trig, p, ts=512, tq=256, tc=512, chunk=128, kchunk=512):
    b, s, _ = nq.shape
    nc = s // CMP_STRIDE
    assert nc == LANES and s // SEL_LEN <= LANES and s >= WIN + tq and s % kchunk == 0
    tspec = pl.BlockSpec((1, ts, LANES), lambda i, j: (i, j, 0))
    q, ks, kw, vs, vw = pl.pallas_call(
        _nsa_prep_kernel,
        grid=(b, s // ts),
        in_specs=[pl.BlockSpec((1, ts, NQ_IN_W), lambda i, j: (i, j, 0)),
                  pl.BlockSpec((1, ts, NKV_IN_W), lambda i, j: (i, j, 0)),
                  tspec, tspec, _full((1, LANES)), _full((3, LANES))],
        out_specs=[pl.BlockSpec((1, NSA_HEADS, ts, LANES), lambda i, j: (i, 0, j, 0)),
                   tspec, tspec, tspec, tspec],
        out_shape=[jax.ShapeDtypeStruct((b, NSA_HEADS, s, LANES), BF16)]
        + [jax.ShapeDtypeStruct((b, s, LANES), BF16)] * 4,
        compiler_params=_cparams(("parallel", "parallel")),
        name="nsa_prep",
    )(nq, nkv, trig[0], trig[1], p["g_q"], p["g_k"])

    gk = nkv[:, :, 0:NSA_DK].reshape(b, nc, CMP_STRIDE * NSA_DK)
    gv = nkv[:, :, 384:384 + NSA_DK].reshape(b, nc, CMP_STRIDE * NSA_DK)
    last = np.minimum(np.arange(nc) * CMP_STRIDE + CMP_LEN - 1, s - 1)
    gw = CMP_STRIDE * NSA_DK
    cspec = pl.BlockSpec((1, nc, LANES), lambda i: (i, 0, 0))
    kc, vc = pl.pallas_call(
        _nsa_cmp_kernel,
        grid=(b,),
        in_specs=[pl.BlockSpec((1, nc, gw), lambda i: (i, 0, 0)),
                  pl.BlockSpec((1, nc, gw), lambda i: (i, 0, 0)),
                  cspec, cspec,
                  _full((1, 2 * gw)), _full((2 * gw, CMP_HID)), _full((CMP_HID, LANES)),
                  _full((1, 2 * gw)), _full((2 * gw, CMP_HID)), _full((CMP_HID, NSA_DK)),
                  _full((3, LANES))],
        out_specs=[cspec, pl.BlockSpec((1, nc, NSA_DK), lambda i: (i, 0, 0))],
        out_shape=[jax.ShapeDtypeStruct((b, nc, LANES), BF16),
                   jax.ShapeDtypeStruct((b, nc, NSA_DK), BF16)],
        compiler_params=_cparams(("parallel",)),
        name="nsa_compress",
    )(gk, gv, trig[0][:, last, :], trig[1][:, last, :], p["pe_k"], p["w1_k"], p["w2_k"], p["pe_v"],
      p["w1_v"], p["w2_v"], p["g_k"])

    ov, ex = _nsa_tables(s, tq)
    o_cmp, sel = pl.pallas_call(
        functools.partial(_nsa_cmpsel_kernel, tc=tc),
        grid=(b, s // tc),
        in_specs=[pl.BlockSpec((1, NSA_HEADS, tc, LANES), lambda i, j: (i, 0, j, 0)),
                  pl.BlockSpec((1, nc, LANES), lambda i, j: (i, 0, 0)),
                  pl.BlockSpec((1, nc, NSA_DK), lambda i, j: (i, 0, 0)),
                  _full((nc, LANES))],
        out_specs=[pl.BlockSpec((1, NSA_HEADS, tc, NSA_DK), lambda i, j: (i, 0, j, 0)),
                   pl.BlockSpec((1, tc, LANES), lambda i, j: (i, j, 0))],
        out_shape=[jax.ShapeDtypeStruct((b, NSA_HEADS, s, NSA_DK), F32),
                   jax.ShapeDtypeStruct((b, s, LANES), BF16)],
        compiler_params=_cparams(("parallel", "parallel")),
        name="nsa_cmpsel",
    )(q, kc, vc, ov)

    rows = NSA_HEADS * tq
    kvspec = pl.BlockSpec((1, s, LANES), lambda i, j: (i, 0, 0))
    return pl.pallas_call(
        functools.partial(_nsa_attn_kernel, tq=tq, chunk=chunk, kchunk=kchunk),
        grid=(b, s // tq),
        in_specs=[pl.BlockSpec((1, NSA_HEADS, tq, LANES), lambda i, j: (i, 0, j, 0)),
                  pl.BlockSpec((1, tq, LANES), lambda i, j: (i, j, 0)),
                  pl.BlockSpec((1, NSA_HEADS, tq, NSA_DK), lambda i, j: (i, 0, j, 0)),
                  kvspec, kvspec, kvspec, kvspec,
                  pl.BlockSpec((1, tq, LANES), lambda i, j: (i, j, NKV_IN_W // LANES - 1)),
                  _full((LANES, s))],
        out_specs=pl.BlockSpec((1, tq, GROUP_W), lambda i, j: (i, j, 0)),
        out_shape=jax.ShapeDtypeStruct((b, s, GROUP_W), F32),
        scratch_shapes=[pltpu.VMEM((2, rows, s), F32), pltpu.VMEM((2, rows, s), BF16),
                        pltpu.VMEM((2, rows, LANES), F32), pltpu.VMEM((2, rows, NSA_DK), F32)],
        compiler_params=_cparams(("parallel", "arbitrary")),
        name="nsa_attn",
    )(q, sel, o_cmp, ks, vs, kw, vw, nkv, ex)


def _prep_nsa(g_q, g_k, pe_k, w1_k, w2_k, pe_v, w1_v, w2_v):
    return dict(g_q=_pad_last(g_q, LANES)[None, :], g_k=_pad_last(g_k, LANES),
                pe_k=pe_k.reshape(1, CMP_LEN * NSA_DK), w1_k=w1_k.astype(BF16),
                w2_k=_pad_last(w2_k, LANES).astype(BF16),
                pe_v=pe_v.reshape(1, CMP_LEN * NSA_DK), w1_v=w1_v.astype(BF16),
                w2_v=w2_v.astype(BF16))


def _route(logits):
    tm = logits.shape[0]
    lane = lax.broadcasted_iota(jnp.int32, (tm, LANES), 1).astype(F32)
    far = float(LANES)
    is_g = lane < N_GROUPS
    lg = jnp.where(is_g, logits, -jnp.inf)
    gmax = jnp.max(lg, axis=-1, keepdims=True)
    gi = jnp.min(jnp.where(is_g & (lg == gmax), lane, far), axis=-1, keepdims=True)
    pg_top = 1.0 / jnp.sum(jnp.exp(lg - gmax), axis=-1, keepdims=True)
    e_id = lane - N_GROUPS
    in_g = (e_id >= gi * EXP_PER_GROUP) & (e_id < (gi + 1.0) * EXP_PER_GROUP)
    le = jnp.where(in_g, logits, -jnp.inf)
    m1 = jnp.max(le, axis=-1, keepdims=True)
    i1 = jnp.min(jnp.where(in_g & (le == m1), lane, far), axis=-1, keepdims=True)
    le2 = jnp.where(lane == i1, -jnp.inf, le)
    m2 = jnp.max(le2, axis=-1, keepdims=True)
    i2 = jnp.min(jnp.where(le2 == m2, lane, far), axis=-1, keepdims=True)
    v2 = jnp.exp(m2 - m1)
    w1 = pg_top / (1.0 + v2)
    w2 = pg_top * v2 / (1.0 + v2)
    comb = jnp.where(lane == i1, w1, 0.0) + jnp.where(lane == i2, w2, 0.0)
    return pltpu.roll(comb, LANES - N_GROUPS, axis=1)


def _moe_kernel(ya_ref, yb_ref, yc_ref, yd_ref, x_ref, gout_ref, wout_ref, g_ref, wr_hi_ref, wr_lo_ref,
                br_ref, wg_ref, wu_ref, wd_ref, o_ref, hn, comb, *, epi):
    x = x_ref[...]
    for i, y_ref in enumerate((ya_ref, yb_ref, yc_ref, yd_ref)):
        y = _rms(y_ref[...], gout_ref[i:i + 1, :], GROUP_W)
        x = x + _dot(y, wout_ref[GROUP_W * i:GROUP_W * (i + 1), :])
    h = x * lax.rsqrt(jnp.mean(x * x, axis=-1, keepdims=True) + EPS) * g_ref[...]
    h_hi = h.astype(BF16)
    hn[...] = h_hi
    h_lo = (h - h_hi.astype(F32)).astype(BF16)
    logits = (jnp.dot(h_hi, wr_hi_ref[...], preferred_element_type=F32)
              + jnp.dot(h_hi, wr_lo_ref[...], preferred_element_type=F32)
              + jnp.dot(h_lo, wr_hi_ref[...], preferred_element_type=F32)) + br_ref[...]
    comb[...] = _route(logits)
    o_ref[...] = x
    lane = lax.broadcasted_iota(jnp.int32, comb.shape, 1)

    def experts(i, carry):
        hb = hn[...]
        cmb = comb[...]
        acc = None
        for k in range(epi):
            e = i * epi + k
            a = jax.nn.silu(jnp.dot(hb, wg_ref[e], preferred_element_type=F32)) \
                * jnp.dot(hb, wu_ref[e], preferred_element_type=F32)
            a = a * jnp.sum(jnp.where(lane == e, cmb, 0.0), axis=-1, keepdims=True)
            d = jnp.dot(a.astype(BF16), wd_ref[e], preferred_element_type=F32)
            acc = d if acc is None else acc + d
        o_ref[...] += acc
        return carry

    lax.fori_loop(0, N_EXPERTS // epi, experts, 0)


def _outproj_moe(ys, x2, g_out, w_out, p, tm=512, epi=2):
    t = x2.shape[0]
    yspec = pl.BlockSpec((tm, GROUP_W), lambda i: (i, 0))

    def resident(shape):
        nd = len(shape)
        return pl.BlockSpec(shape, lambda i: (0,) * nd, pipeline_mode=pl.Buffered(1))

    return pl.pallas_call(
        functools.partial(_moe_kernel, epi=epi),
        grid=(t // tm,),
        in_specs=[yspec, yspec, yspec, yspec, pl.BlockSpec((tm, D_MODEL), lambda i: (i, 0)),
                  _full((4, GROUP_W)), resident((D_MODEL, D_MODEL)),
                  _full((1, D_MODEL)), _full((D_MODEL, LANES)), _full((D_MODEL, LANES)),
                  _full((1, LANES)),
                  resident((N_EXPERTS, D_MODEL, D_EXPERT)), resident((N_EXPERTS, D_MODEL, D_EXPERT)),
                  resident((N_EXPERTS, D_EXPERT, D_MODEL))],
        out_specs=pl.BlockSpec((tm, D_MODEL), lambda i: (i, 0)),
        out_shape=jax.ShapeDtypeStruct((t, D_MODEL), F32),
        scratch_shapes=[pltpu.VMEM((tm, D_MODEL), BF16), pltpu.VMEM((tm, LANES), F32)],
        compiler_params=_cparams(("parallel",)),
        name="outproj_moe",
    )(*ys, x2, g_out, w_out, p["g"], p["wr_hi"], p["wr_lo"], p["br"], p["wg"], p["wu"], p["wd"])


def _prep_moe(g, w_rg, b_rg, w_re, b_re, w_gate, w_up, w_down):
    wr = _pad_last(jnp.concatenate([w_rg, w_re], axis=1), LANES)
    wr_hi = wr.astype(BF16)
    wr_lo = (wr - wr_hi.astype(F32)).astype(BF16)
    br = _pad_last(jnp.concatenate([b_rg, b_re]), LANES)[None, :]
    return dict(g=g[None, :], wr_hi=wr_hi, wr_lo=wr_lo, br=br, wg=w_gate.astype(BF16),
                wu=w_up.astype(BF16), wd=w_down.astype(BF16))


def kernel(x, positions, mix_norm, w_in, mla_g_cq, mla_g_ckv, mla_w_uq, mla_w_ukv, mla_g_q, mla_g_k, lru_conv_w, lru_conv_b, lru_w_a, lru_b_a, lru_w_i, lru_b_i, lru_lambda, s5_a_re, s5_a_im, s5_log_dt, s5_b_re, s5_b_im, s5_c_re, s5_c_im, s5_d, s5_w_glu, s5_b_glu, nsa_g_q, nsa_g_k, nsa_pe_k, nsa_w1_k, nsa_w2_k, nsa_pe_v, nsa_w1_v, nsa_w2_v, out_norm, w_out, ffn_norm, moe_w_rg, moe_b_rg, moe_w_re, moe_b_re, moe_w_gate, moe_w_up, moe_w_down):
    b, s, d = x.shape
    t = b * s
    trig = _trig(positions.astype(jnp.int32)[:, :, None])
    x2 = x.reshape(t, d)
    for l in range(w_in.shape[0]):
        o_mla, o_lru, o_s5, o_nq, o_nkv = _inproj(x2, mix_norm[l], _prep_w_in(w_in[l]))
        y_a = _mla(o_mla.reshape(b, s, MLA_IN_W), trig,
                   _prep_mla(mla_g_cq[l], mla_g_ckv[l], mla_w_uq[l], mla_w_ukv[l], mla_g_q[l], mla_g_k[l]))
        y_b = _lru(o_lru.reshape(b, s, LRU_IN_W),
                   _prep_lru(lru_conv_w[l], lru_conv_b[l], lru_w_a[l], lru_b_a[l], lru_w_i[l], lru_b_i[l],
                             lru_lambda[l]))
        u_tm = o_s5.reshape(b, s, S5_W).transpose(1, 0, 2)
        y_c = _s5(u_tm, _prep_s5(s5_a_re[l], s5_a_im[l], s5_log_dt[l], s5_b_re[l], s5_b_im[l], s5_c_re[l],
                                 s5_c_im[l], s5_d[l], s5_w_glu[l], s5_b_glu[l])).transpose(1, 0, 2)
        y_d = _nsa(o_nq.reshape(b, s, NQ_IN_W), o_nkv.reshape(b, s, NKV_IN_W), trig,
                   _prep_nsa(nsa_g_q[l], nsa_g_k[l], nsa_pe_k[l], nsa_w1_k[l], nsa_w2_k[l], nsa_pe_v[l],
                             nsa_w1_v[l], nsa_w2_v[l]))
        ys = [y.reshape(t, GROUP_W) for y in (y_a, y_b, y_c, y_d)]
        x2 = _outproj_moe(ys, x2, out_norm[l], w_out[l].astype(BF16),
                          _prep_moe(ffn_norm[l], moe_w_rg[l], moe_b_rg[l], moe_w_re[l], moe_b_re[l],
                                    moe_w_gate[l], moe_w_up[l], moe_w_down[l]))
    return x2.reshape(b, s, d)
```

```python
import functools
import math

import numpy as np
import jax
import jax.numpy as jnp
from jax import lax
from jax.experimental import pallas as pl
from jax.experimental.pallas import tpu as pltpu

F32 = jnp.float32
BF16 = jnp.bfloat16

D_MODEL = 1024
DEPTH = 2
GROUP_W = 256
EPS = 1e-6
ROPE_THETA = 500000.0
NEG = -1e30
LOG2E = math.log2(math.e)

MLA_HEADS = 4
MLA_ROPE = 32
MLA_NOPE = 64
MLA_V = 64
MLA_QK = 96
MLA_Q_RANK = 192
MLA_KV_RANK = 128

LRU_W = 256
LRU_BLOCKS = 4
LRU_BW = 64
CONV_W = 4
LRU_C = 8.0

S5_W = 256
S5_CH = 16
S5_GROUPS = 16
S5_P = 64
S5_STATE = S5_GROUPS * S5_P

NSA_HEADS = 4
NSA_DK = 64
NSA_ROT = 16
CMP_LEN = 32
CMP_STRIDE = 16
CMP_HID = 128
SEL_LEN = 64
SEL_TOPK = 5
WIN = 512

N_GROUPS = 4
EXP_PER_GROUP = 4
N_EXPERTS = 16
D_EXPERT = 256

D_IN = 1772

LANES = 128
VMEM_LIMIT = 48 * 1024 * 1024

MLA_IN_W = 512
LRU_IN_W = 512
S5_IN_W = 256
NQ_IN_W = NSA_HEADS * LANES
NKV_IN_W = 6 * LANES
IN_W = MLA_IN_W + LRU_IN_W + S5_IN_W + NQ_IN_W + NKV_IN_W


def _cparams(sem):
    return pltpu.CompilerParams(dimension_semantics=sem, vmem_limit_bytes=VMEM_LIMIT)


def _dot(a, b):
    return jnp.dot(a.astype(BF16), b.astype(BF16), preferred_element_type=F32)


def _dot_nt(a, b):
    return lax.dot_general(a.astype(BF16), b.astype(BF16), (((1,), (1,)), ((), ())),
                           preferred_element_type=F32)


def _split3(x):
    hi = x.astype(BF16)
    r = x - hi.astype(F32)
    mid = r.astype(BF16)
    lo = (r - mid.astype(F32)).astype(BF16)
    return hi, mid, lo


def _dot_f32_by_exact(x, w_bf16):
    hi, mid, lo = _split3(x)
    return (jnp.dot(hi, w_bf16, preferred_element_type=F32)
            + jnp.dot(mid, w_bf16, preferred_element_type=F32)
            + jnp.dot(lo, w_bf16, preferred_element_type=F32))


def _rms(x, g, n):
    return x * lax.rsqrt(jnp.sum(x * x, axis=-1, keepdims=True) * (1.0 / n) + EPS) * g


def _gelu(x):
    return 0.5 * x * (1.0 + jnp.tanh(math.sqrt(2.0 / math.pi) * (x + 0.044715 * (x * x * x))))


def _rope(x, cos, sin_lo, sin_hi, half):
    return (x * cos + pltpu.roll(x, LANES - half, axis=1) * sin_lo
            + pltpu.roll(x, half, axis=1) * sin_hi)


def _with_ones(x, upper):
    if upper:
        x = pltpu.roll(x, LANES // 2, axis=1)
    lane = lax.broadcasted_iota(jnp.int32, x.shape, 1)
    return jnp.where(lane < LANES // 2, x, 1.0).astype(BF16)


NSA_TRIG_LANE0 = MLA_ROPE


def _trig_kernel(pos_ref, inv_ref, cos_ref, sin_ref):
    ang = pos_ref[0].astype(F32) * inv_ref[...]
    cos_ref[0] = jnp.cos(ang)
    sin_ref[0] = jnp.sin(ang)


def _trig(pos3, ts=512):
    def inv(rot):
        v = ROPE_THETA ** (-jnp.arange(rot // 2, dtype=F32) * 2.0 / rot)
        return jnp.concatenate([v, v])
    inv_l = _pad_last(jnp.concatenate([inv(MLA_ROPE), inv(NSA_ROT)]), LANES)[None, :]
    b, s, _ = pos3.shape
    spec = pl.BlockSpec((1, ts, LANES), lambda i, j: (i, j, 0))
    return pl.pallas_call(
        _trig_kernel,
        grid=(b, s // ts),
        in_specs=[pl.BlockSpec((1, ts, 1), lambda i, j: (i, j, 0)), _full((1, LANES))],
        out_specs=[spec, spec],
        out_shape=[jax.ShapeDtypeStruct((b, s, LANES), F32)] * 2,
        compiler_params=_cparams(("parallel", "parallel")),
        name="rope_trig",
    )(pos3, inv_l)


def _rope_tables(cs, sn, lane0, half):
    if lane0:
        cs = pltpu.roll(cs, LANES - lane0, axis=1)
        sn = pltpu.roll(sn, LANES - lane0, axis=1)
    lane = lax.broadcasted_iota(jnp.int32, cs.shape, 1)
    cos = jnp.where(lane < 2 * half, cs, 1.0)
    s_lo = jnp.where(lane < half, -sn, 0.0)
    s_hi = jnp.where((lane >= half) & (lane < 2 * half), sn, 0.0)
    return cos, s_lo, s_hi


def _pad_last(a, n):
    return jnp.pad(a, [(0, 0)] * (a.ndim - 1) + [(0, n - a.shape[-1])])


def _full(shape):
    nd = len(shape)
    return pl.BlockSpec(shape, lambda *_: (0,) * nd)


def _inproj_kernel(x_ref, g_ref, w_ref, o_mla, o_lru, o_s5, o_nq, o_nkv):
    x = x_ref[...]
    h = x * lax.rsqrt(jnp.mean(x * x, axis=-1, keepdims=True) + EPS) * g_ref[...]
    y = jnp.dot(h.astype(BF16), w_ref[...], preferred_element_type=F32)
    c0 = 0
    for o in (o_mla, o_lru, o_s5, o_nq, o_nkv):
        w = o.shape[-1]
        o[...] = y[:, c0:c0 + w].astype(o.dtype)
        c0 += w


def _inproj_cols():
    src = -np.ones((IN_W,), np.int64)
    o = 0
    src[o:o + 192] = np.arange(0, 192)
    src[o + 256:o + 384] = np.arange(192, 320)
    src[o + 384:o + 416] = np.arange(320, 352)
    o += MLA_IN_W
    src[o:o + 512] = np.arange(352, 864)
    o += LRU_IN_W
    src[o:o + 256] = np.arange(864, 1120)
    o += S5_IN_W
    for h in range(NSA_HEADS):
        src[o + LANES * h:o + LANES * h + 64] = np.arange(1120 + 64 * h, 1120 + 64 * h + 64)
    o += NQ_IN_W
    kv0 = 1376
    src[o:o + 64] = np.arange(kv0, kv0 + 64)
    src[o + 128:o + 192] = np.arange(kv0 + 128, kv0 + 192)
    src[o + 256:o + 320] = np.arange(kv0 + 256, kv0 + 320)
    src[o + 384:o + 448] = np.arange(kv0 + 64, kv0 + 128)
    src[o + 448:o + 512] = np.arange(kv0 + 192, kv0 + 256)
    src[o + 512:o + 576] = np.arange(kv0 + 320, kv0 + 384)
    src[o + 640:o + 652] = np.arange(1760, 1772)
    return src


_INPROJ_SRC = _inproj_cols()


def _prep_w_in(w_in):
    idx = jnp.asarray(np.maximum(_INPROJ_SRC, 0), jnp.int32)
    keep = jnp.asarray(_INPROJ_SRC >= 0)
    return jnp.where(keep[None, :], jnp.take(w_in, idx, axis=1), 0.0).astype(BF16)


def _inproj(x2, g, w_pad, tm=512):
    t = x2.shape[0]
    widths = (MLA_IN_W, LRU_IN_W, S5_IN_W, NQ_IN_W, NKV_IN_W)
    return pl.pallas_call(
        _inproj_kernel,
        grid=(t // tm,),
        in_specs=[pl.BlockSpec((tm, D_MODEL), lambda i: (i, 0)),
                  _full((1, D_MODEL)),
                  _full((D_MODEL, IN_W))],
        out_specs=[pl.BlockSpec((tm, w), lambda i: (i, 0)) for w in widths],
        out_shape=[jax.ShapeDtypeStruct((t, w), BF16 if w == S5_IN_W else F32) for w in widths],
        compiler_params=_cparams(("parallel",)),
        name="inproj",
    )(x2, g[None, :], w_pad)


def _mla_prep_kernel(in_ref, cs_ref, sn_ref, gcq_ref, wuq_ref, gckv_ref, wk_ref, wv_ref, gq_ref, gk_ref,
                     q_out, k_out, v_out):
    xin = in_ref[0]
    cq = _rms(xin[:, 0:256], gcq_ref[...], MLA_Q_RANK)
    ckv = _rms(xin[:, 256:384], gckv_ref[...], MLA_KV_RANK)
    kpe = xin[:, 384:512]
    q = _dot(cq, wuq_ref[...])
    kn = _dot(ckv, wk_ref[...])
    v = _dot(ckv, wv_ref[...])
    cos, s_lo, s_hi = _rope_tables(cs_ref[0], sn_ref[0], 0, MLA_ROPE // 2)
    scale = MLA_QK ** -0.5 * LOG2E
    for h in range(MLA_HEADS):
        qh = _rms(q[:, LANES * h:LANES * (h + 1)], gq_ref[...], MLA_QK)
        q_out[0, h] = (_rope(qh, cos, s_lo, s_hi, MLA_ROPE // 2) * scale).astype(BF16)
        kh = _rms(kn[:, LANES * h:LANES * (h + 1)] + kpe, gk_ref[...], MLA_QK)
        k_out[0, h] = _rope(kh, cos, s_lo, s_hi, MLA_ROPE // 2).astype(BF16)
        v_out[0, h] = _with_ones(v[:, LANES * (h // 2):LANES * (h // 2 + 1)], h % 2 == 1)


def _attend(q, segs, v1, s_scr, p_scr, m_scr, nh, chunk):
    rows = q.shape[0]
    c0 = 0
    m = None
    for k, bias in segs:
        n = k.shape[0]
        sb = _dot_nt(q, k)
        if bias is not None:
            sb = (sb.reshape(nh, rows // nh, n) + bias[None]).reshape(rows, n)
        s_scr[0:rows, c0:c0 + n] = sb
        mx = jnp.max(sb, axis=-1, keepdims=True)
        m = mx if m is None else jnp.maximum(m, mx)
        c0 += n
    m_scr[0:rows, :] = jnp.broadcast_to(m, (rows, LANES))
    for r0 in range(0, rows, chunk):
        mb = m_scr[r0:r0 + chunk, :]
        for j0 in range(0, c0, LANES):
            p = jnp.exp2(s_scr[r0:r0 + chunk, j0:j0 + LANES] - mb)
            p_scr[r0:r0 + chunk, j0:j0 + LANES] = p.astype(BF16)
    o = jnp.dot(p_scr[0:rows, 0:c0], v1, preferred_element_type=F32)
    return (o * (1.0 / pltpu.roll(o, LANES // 2, axis=1)))[:, 0:LANES // 2]


def _mla_attn_kernel(q_ref, k_ref, v_ref, o_ref, s_scr, p_scr, m_scr, o_scr, bias_scr, *, tq, chunk):
    qi = pl.program_id(1)
    row = lax.broadcasted_iota(jnp.int32, (tq, tq), 0)
    col = lax.broadcasted_iota(jnp.int32, (tq, tq), 1)
    bias_scr[...] = jnp.where(col <= row, 0.0, NEG)
    nset = s_scr.shape[0]
    for c in range(k_ref.shape[2] // tq):

        @pl.when(qi == c)
        def _(c=c):
            n = (c + 1) * tq

            def heads(i, carry):
                for j in range(nset):
                    h = i * nset + j
                    segs = [(k_ref[0, h, n - tq:n, :], bias_scr[...])]
                    if c > 0:
                        segs = [(k_ref[0, h, 0:n - tq, :], None)] + segs
                    o_scr[h] = _attend(q_ref[0, h], segs, v_ref[0, h, 0:n, :], s_scr.at[j], p_scr.at[j],
                                       m_scr.at[j], 1, chunk)
                return carry

            lax.fori_loop(0, MLA_HEADS // nset, heads, 0)

    for h in range(MLA_HEADS):
        o_ref[0, :, MLA_V * h:MLA_V * (h + 1)] = o_scr[h].astype(o_ref.dtype)


def _mla(mla_in, trig, p, ts=512, tq=512, chunk=128, nset=2):
    b, s, _ = mla_in.shape
    hq = (b, MLA_HEADS, s, LANES)
    q, k, v = pl.pallas_call(
        _mla_prep_kernel,
        grid=(b, s // ts),
        in_specs=[pl.BlockSpec((1, ts, MLA_IN_W), lambda i, j: (i, j, 0)),
                  pl.BlockSpec((1, ts, LANES), lambda i, j: (i, j, 0)),
                  pl.BlockSpec((1, ts, LANES), lambda i, j: (i, j, 0)),
                  _full((1, 256)), _full((256, 512)), _full((1, 128)), _full((128, 512)),
                  _full((128, 256)), _full((1, 128)), _full((1, 128))],
        out_specs=[pl.BlockSpec((1, MLA_HEADS, ts, LANES), lambda i, j: (i, 0, j, 0)),
                   pl.BlockSpec((1, MLA_HEADS, ts, LANES), lambda i, j: (i, 0, j, 0)),
                   pl.BlockSpec((1, MLA_HEADS, ts, LANES), lambda i, j: (i, 0, j, 0))],
        out_shape=[jax.ShapeDtypeStruct(hq, BF16), jax.ShapeDtypeStruct(hq, BF16),
                   jax.ShapeDtypeStruct(hq, BF16)],
        compiler_params=_cparams(("parallel", "parallel")),
        name="mla_prep",
    )(mla_in, trig[0], trig[1], p["g_cq"], p["w_uq"], p["g_ckv"], p["w_k"], p["w_v"], p["g_q"], p["g_k"])
    return pl.pallas_call(
        functools.partial(_mla_attn_kernel, tq=tq, chunk=chunk),
        grid=(b, s // tq),
        in_specs=[pl.BlockSpec((1, MLA_HEADS, tq, LANES), lambda i, j: (i, 0, j, 0)),
                  pl.BlockSpec((1, MLA_HEADS, s, LANES), lambda i, j: (i, 0, 0, 0)),
                  pl.BlockSpec((1, MLA_HEADS, s, LANES), lambda i, j: (i, 0, 0, 0))],
        out_specs=pl.BlockSpec((1, tq, GROUP_W), lambda i, j: (i, j, 0)),
        out_shape=jax.ShapeDtypeStruct((b, s, GROUP_W), BF16),
        scratch_shapes=[pltpu.VMEM((nset, tq, s), F32), pltpu.VMEM((nset, tq, s), BF16),
                        pltpu.VMEM((nset, tq, LANES), F32),
                        pltpu.VMEM((MLA_HEADS, tq, MLA_V), F32), pltpu.VMEM((tq, tq), F32)],
        compiler_params=_cparams(("parallel", "arbitrary")),
        name="mla_attn",
    )(q, k, v)


def _prep_mla(g_cq, g_ckv, w_uq, w_ukv, g_q, g_k):
    wq = w_uq.reshape(MLA_Q_RANK, MLA_HEADS, MLA_QK)
    wq = _pad_last(wq, LANES).reshape(MLA_Q_RANK, MLA_HEADS * LANES)
    wq = jnp.pad(wq, ((0, 256 - MLA_Q_RANK), (0, 0)))
    wkv = w_ukv.reshape(MLA_KV_RANK, MLA_HEADS, MLA_NOPE + MLA_V)
    wk = jnp.pad(wkv[:, :, :MLA_NOPE], ((0, 0), (0, 0), (MLA_ROPE, LANES - MLA_QK)))
    wk = wk.reshape(MLA_KV_RANK, MLA_HEADS * LANES)
    wv = wkv[:, :, MLA_NOPE:].reshape(MLA_KV_RANK, MLA_HEADS * MLA_V)
    return dict(g_cq=_pad_last(g_cq, 256)[None, :], g_ckv=g_ckv[None, :], w_uq=wq.astype(BF16),
                w_k=wk.astype(BF16), w_v=wv.astype(BF16), g_q=_pad_last(g_q, LANES)[None, :],
                g_k=_pad_last(g_k, LANES)[None, :])


def _shift_rows(x, k, row, fill):
    return jnp.where(row >= k, pltpu.roll(x, k, axis=0), fill)


def _lru_kernel(in_ref, cw_ref, cb_ref, wa_ref, ba_ref, wi_ref, bi_ref, lam_ref, o_ref, a_scr, b_scr, *,
                chunk):
    xin = in_ref[0]
    s = xin.shape[0]
    xb = xin[:, :LRU_W]
    row = lax.broadcasted_iota(jnp.int32, (s, LRU_W), 0)
    u = cb_ref[...] + xb * cw_ref[CONV_W - 1:CONV_W, :]
    for j in range(CONV_W - 1):
        u = u + _shift_rows(xb, CONV_W - 1 - j, row, 0.0) * cw_ref[j:j + 1, :]
    r = jax.nn.sigmoid(_dot(u, wa_ref[...]) + ba_ref[...])
    gi = jax.nn.sigmoid(_dot(u, wi_ref[...]) + bi_ref[...])
    nlam = -lam_ref[...]
    softplus = jnp.maximum(nlam, 0.0) + jnp.log1p(jnp.exp(-jnp.abs(nlam)))
    log_a = (-LRU_C) * r * softplus
    a = jnp.exp(log_a)
    y = jnp.maximum(-jnp.tanh(log_a) * (a * a + 1.0), 0.0)
    mult = jnp.where(y > 0.0, y * lax.rsqrt(y), 0.0)
    mult = jnp.where(row == 0, 1.0, mult)
    bt = mult * gi * u
    rin = jnp.bitwise_and(row, chunk - 1)
    k = 1
    while k < chunk:
        bt = a * _shift_rows(bt, k, rin, 0.0) + bt
        a = a * _shift_rows(a, k, rin, 1.0)
        k *= 2
    a_scr[...] = a
    b_scr[...] = bt

    def carry_chunk(j, h_prev):
        r0 = pl.multiple_of(j * chunk, chunk)
        h = b_scr[pl.ds(r0, chunk), :] + a_scr[pl.ds(r0, chunk), :] * h_prev
        o_ref[0, pl.ds(r0, chunk), :] = (h * _gelu(in_ref[0, pl.ds(r0, chunk), LRU_W:])).astype(o_ref.dtype)
        return h[chunk - 1:chunk, :]

    lax.fori_loop(0, s // chunk, carry_chunk, jnp.zeros((1, LRU_W), F32))


def _block_diag(w):
    n, i, j = w.shape
    eye = jnp.eye(n, dtype=w.dtype)
    return (eye[:, None, :, None] * w[:, :, None, :]).reshape(n * i, n * j)


def _lru(lru_in, p, chunk=64):
    b, s, _ = lru_in.shape
    return pl.pallas_call(
        functools.partial(_lru_kernel, chunk=chunk),
        grid=(b,),
        in_specs=[pl.BlockSpec((1, s, LRU_IN_W), lambda i: (i, 0, 0)),
                  _full((CONV_W, LRU_W)), _full((1, LRU_W)), _full((LRU_W, LRU_W)), _full((1, LRU_W)),
                  _full((LRU_W, LRU_W)), _full((1, LRU_W)), _full((1, LRU_W))],
        out_specs=pl.BlockSpec((1, s, LRU_W), lambda i: (i, 0, 0)),
        out_shape=jax.ShapeDtypeStruct((b, s, LRU_W), BF16),
        scratch_shapes=[pltpu.VMEM((s, LRU_W), F32), pltpu.VMEM((s, LRU_W), F32)],
        compiler_params=_cparams(("parallel",)),
        name="rglru",
    )(lru_in, p["cw"], p["cb"], p["wa"], p["ba"], p["wi"], p["bi"], p["lam"])


def _prep_lru(conv_w, conv_b, w_a, b_a, w_i, b_i, lam):
    return dict(cw=conv_w, cb=conv_b[None, :], wa=_block_diag(w_a).astype(BF16),
                ba=b_a.reshape(1, LRU_W), wi=_block_diag(w_i).astype(BF16), bi=b_i.reshape(1, LRU_W),
                lam=lam[None, :])


def _s5_disc_kernel(are_ref, aim_ref, ldt_ref, arer_ref, aimr_ref, bre_ref, bim_ref,
                    abre_ref, abim_ref, bbre_ref, bbim_ref):
    dt = jnp.exp(ldt_ref[...])

    def disc(a_re, a_im):
        mag = jnp.exp(dt * a_re)
        ab_re = mag * jnp.cos(dt * a_im)
        ab_im = mag * jnp.sin(dt * a_im)
        den = a_re * a_re + a_im * a_im
        n_re = ab_re - 1.0
        g_re = (n_re * a_re + ab_im * a_im) / den
        g_im = (ab_im * a_re - n_re * a_im) / den
        return ab_re, ab_im, g_re, g_im

    ab_re, ab_im, _, _ = disc(are_ref[...], aim_ref[...])
    abre_ref[...] = ab_re
    abim_ref[...] = ab_im
    _, _, g_re, g_im = disc(arer_ref[...], aimr_ref[...])
    bbre_ref[...] = g_re * bre_ref[...] - g_im * bim_ref[...]
    bbim_ref[...] = g_re * bim_ref[...] + g_im * bre_ref[...]


def _s5_kernel(u_ref, are_ref, aim_ref, bre_ref, bim_ref, cre_ref, cim_ref, d_ref, wg_ref, bg_ref,
               o_ref, hre, him, st_re, st_im, *, tc, nb, cw):
    @pl.when(pl.program_id(0) == 0)
    def _():
        st_re[...] = jnp.zeros_like(st_re)
        st_im[...] = jnp.zeros_like(st_im)

    ub = u_ref[...].reshape(tc * nb, S5_W)
    u = ub.astype(F32)
    hre[...] = jnp.dot(ub, bre_ref[...], preferred_element_type=F32)
    him[...] = jnp.dot(ub, bim_ref[...], preferred_element_type=F32)
    for c in range(S5_STATE // cw):
        cs = slice(c * cw, (c + 1) * cw)
        ar = jnp.broadcast_to(are_ref[:, cs], (nb, cw))
        ai = jnp.broadcast_to(aim_ref[:, cs], (nb, cw))

        def body(t, carry, cs=cs, ar=ar, ai=ai):
            hr, hi = carry
            r0 = pl.multiple_of(t * nb, nb)
            nr = ar * hr - ai * hi + hre[pl.ds(r0, nb), cs]
            ni = ar * hi + ai * hr + him[pl.ds(r0, nb), cs]
            hre[pl.ds(r0, nb), cs] = nr
            him[pl.ds(r0, nb), cs] = ni
            return nr, ni

        hr, hi = lax.fori_loop(0, tc, body, (st_re[:, cs], st_im[:, cs]), unroll=4)
        st_re[:, cs] = hr
        st_im[:, cs] = hi
    y = (jnp.dot(hre[...].astype(BF16), cre_ref[...], preferred_element_type=F32)
         - jnp.dot(him[...].astype(BF16), cim_ref[...], preferred_element_type=F32))
    y = _gelu(y + d_ref[...] * u)
    z = _dot(y, wg_ref[...]) + bg_ref[...]
    o_ref[...] = (y * jax.nn.sigmoid(z)).astype(o_ref.dtype).reshape(tc, nb, S5_W)


def _prep_s5(a_re, a_im, log_dt, b_re, b_im, c_re, c_im, d, w_glu, b_glu):
    g, p, ch = S5_GROUPS, S5_P, S5_CH
    ab_re, ab_im, bb_re, bb_im = pl.pallas_call(
        _s5_disc_kernel,
        out_shape=[jax.ShapeDtypeStruct((g, p), F32), jax.ShapeDtypeStruct((g, p), F32),
                   jax.ShapeDtypeStruct((g, p * ch), F32), jax.ShapeDtypeStruct((g, p * ch), F32)],
        name="s5_discretize",
    )(a_re, a_im, log_dt[:, None], jnp.repeat(a_re, ch, axis=1), jnp.repeat(a_im, ch, axis=1),
      b_re.reshape(g, p * ch), b_im.reshape(g, p * ch))

    def b_dense(bb):
        return _block_diag(bb.reshape(g, p, ch).transpose(0, 2, 1)).astype(BF16)

    def c_dense(c):
        return _block_diag(c.transpose(0, 2, 1)).astype(BF16)

    return dict(a_re=ab_re.reshape(1, S5_STATE), a_im=ab_im.reshape(1, S5_STATE),
                b_re=b_dense(bb_re), b_im=b_dense(bb_im), c_re=c_dense(c_re), c_im=c_dense(c_im),
                d=d[None, :], w_glu=w_glu.astype(BF16), b_glu=b_glu[None, :])


def _s5(u_tm, p, tc=64, cw=256):
    s, nb, _ = u_tm.shape
    return pl.pallas_call(
        functools.partial(_s5_kernel, tc=tc, nb=nb, cw=cw),
        grid=(s // tc,),
        in_specs=[pl.BlockSpec((tc, nb, S5_W), lambda i: (i, 0, 0)),
                  _full((1, S5_STATE)), _full((1, S5_STATE)),
                  _full((S5_W, S5_STATE)), _full((S5_W, S5_STATE)),
                  _full((S5_STATE, S5_W)), _full((S5_STATE, S5_W)),
                  _full((1, S5_W)), _full((S5_W, S5_W)), _full((1, S5_W))],
        out_specs=pl.BlockSpec((tc, nb, S5_W), lambda i: (i, 0, 0)),
        out_shape=jax.ShapeDtypeStruct((s, nb, S5_W), BF16),
        scratch_shapes=[pltpu.VMEM((tc * nb, S5_STATE), F32), pltpu.VMEM((tc * nb, S5_STATE), F32),
                        pltpu.VMEM((nb, S5_STATE), F32), pltpu.VMEM((nb, S5_STATE), F32)],
        compiler_params=_cparams(("arbitrary",)),
        name="s5",
    )(u_tm, p["a_re"], p["a_im"], p["b_re"], p["b_im"], p["c_re"], p["c_im"], p["d"], p["w_glu"],
      p["b_glu"])


def _nsa_prep_kernel(nq_ref, nkv_ref, cs_ref, sn_ref, gq_ref, gk_ref,
                     q_out, ks_out, kw_out, vs_out, vw_out):
    half = NSA_ROT // 2
    cos, s_lo, s_hi = _rope_tables(cs_ref[0], sn_ref[0], NSA_TRIG_LANE0, half)
    nq = nq_ref[0]
    scale = NSA_DK ** -0.5 * LOG2E
    for h in range(NSA_HEADS):
        qh = _rms(nq[:, LANES * h:LANES * (h + 1)], gq_ref[...], NSA_DK)
        q_out[0, h] = (_rope(qh, cos, s_lo, s_hi, half) * scale).astype(BF16)
    nkv = nkv_ref[0]
    ks = _rms(nkv[:, 128:256], gk_ref[1:2, :], NSA_DK)
    ks_out[0] = _rope(ks, cos, s_lo, s_hi, half).astype(BF16)
    kw = _rms(nkv[:, 256:384], gk_ref[2:3, :], NSA_DK)
    kw_out[0] = _rope(kw, cos, s_lo, s_hi, half).astype(BF16)
    vs_out[0] = _with_ones(nkv[:, 384:512], True)
    vw_out[0] = _with_ones(nkv[:, 512:640], False)


def _nsa_cmp_kernel(k_in, v_in, cs_ref, sn_ref, pek_ref, w1k_ref, w2k_ref, pev_ref, w1v_ref, w2v_ref,
                    g_ref, kc_out, vc_out):
    nc = kc_out.shape[1]

    def compress(x_ref, pe_ref, w1_ref, w2_ref):
        lo = hi = None
        for j in range(CMP_STRIDE):
            xj = x_ref[0, pl.ds(j, nc, stride=CMP_STRIDE), :]
            dl = _dot(xj + pe_ref[j:j + 1, :], w1_ref[j])
            dh = _dot(xj + pe_ref[CMP_STRIDE + j:CMP_STRIDE + j + 1, :], w1_ref[CMP_STRIDE + j])
            lo = dl if lo is None else lo + dl
            hi = dh if hi is None else hi + dh
        hid = lo + pltpu.roll(hi, nc - 1, axis=0)
        return _dot(_gelu(hid), w2_ref[...])

    kc = compress(k_in, pek_ref, w1k_ref, w2k_ref)
    cos, s_lo, s_hi = _rope_tables(cs_ref[0], sn_ref[0], NSA_TRIG_LANE0, NSA_ROT // 2)
    kc = _rope(_rms(kc, g_ref[0:1, :], NSA_DK), cos, s_lo, s_hi, NSA_ROT // 2)
    kc_out[0] = kc.astype(BF16)
    vc_out[0] = compress(v_in, pev_ref, w1v_ref, w2v_ref).astype(BF16)


def _nsa_cmpsel_kernel(q_ref, kc_ref, vc_ref, ov_ref, ocmp_ref, sel_ref, *, tc):
    qi = pl.program_id(1)
    nh = NSA_HEADS
    q = q_ref[0].reshape(nh * tc, LANES)
    lane = lax.broadcasted_iota(jnp.int32, (tc, LANES), 1)
    qpos = qi * tc + lax.broadcasted_iota(jnp.int32, (tc, LANES), 0)

    s = _dot_nt(q, kc_ref[0]).reshape(nh, tc, LANES)
    valid = (lane * CMP_STRIDE + (CMP_LEN - 1)) <= qpos
    s = jnp.where(valid, s, NEG)
    e = jnp.exp2(s - jnp.max(s, axis=-1, keepdims=True))
    p_c = jnp.where(valid, e / jnp.sum(e, axis=-1, keepdims=True), 0.0)
    ocmp_ref[0] = jnp.dot(p_c.astype(BF16).reshape(nh * tc, LANES), vc_ref[0],
                          preferred_element_type=F32).reshape(nh, tc, NSA_DK)

    imp = _dot_f32_by_exact(jnp.sum(p_c, axis=0), ov_ref[...])
    cur = qpos // SEL_LEN
    lane_f = lane.astype(F32)
    sel = (lane == 0) | (lane == cur) | (lane == cur - 1)
    cand = (lane < cur - 1) & (lane > 0)
    for _ in range(SEL_TOPK - 3):
        sc = jnp.where(cand, imp, -jnp.inf)
        best = jnp.max(sc, axis=-1, keepdims=True)
        pick = jnp.min(jnp.where(cand & (sc == best), lane_f, float(LANES)), axis=-1, keepdims=True)
        hit = lane_f == pick
        sel = sel | hit
        cand = cand & jnp.logical_not(hit)
    sel_ref[0] = jnp.where(sel, 1.0, 0.0).astype(BF16)


def _nsa_attn_kernel(q_ref, sel_ref, ocmp_ref, ks_ref, vs_ref, kw_ref, vw_ref, gate_ref, ex_ref,
                     o_ref, s_scr, p_scr, m_scr, obr_scr, *, tq, chunk, kchunk):
    qi = pl.program_id(1)
    nh = NSA_HEADS
    rows = nh * tq
    s_len = ks_ref.shape[1]

    def window():
        wk = WIN + tq
        start = pl.multiple_of(jnp.maximum(qi - WIN // tq, 0) * tq, tq)
        kpos = start + lax.broadcasted_iota(jnp.int32, (tq, wk), 1)
        qrow = qi * tq + lax.broadcasted_iota(jnp.int32, (tq, wk), 0)
        bias = jnp.where((kpos <= qrow) & (qrow - kpos < WIN), 0.0, NEG)
        obr_scr[1] = _attend(q_ref[0].reshape(rows, LANES), [(kw_ref[0, pl.ds(start, wk), :], bias)],
                             vw_ref[0, pl.ds(start, wk), :], s_scr.at[1], p_scr.at[1], m_scr.at[1],
                             nh, chunk)

    for c in range(s_len // kchunk):

        @pl.when(qi // (kchunk // tq) == c)
        def _(c=c):
            n = (c + 1) * kchunk
            em = jnp.dot(sel_ref[0], ex_ref[:, 0:n], preferred_element_type=F32)
            kpos = lax.broadcasted_iota(jnp.int32, (tq, n), 1)
            qrow = qi * tq + lax.broadcasted_iota(jnp.int32, (tq, n), 0)
            bias = jnp.where((em > 0.5) & (kpos <= qrow), 0.0, NEG)
            obr_scr[0] = _attend(q_ref[0].reshape(rows, LANES), [(ks_ref[0, 0:n, :], bias)],
                                 vs_ref[0, 0:n, :], s_scr.at[0], p_scr.at[0], m_scr.at[0], nh, chunk)
            window()

    g = jax.nn.sigmoid(gate_ref[0])
    for h in range(nh):
        r = slice(h * tq, (h + 1) * tq)
        o = (g[:, 3 * h:3 * h + 1] * ocmp_ref[0, h] + g[:, 3 * h + 1:3 * h + 2] * obr_scr[0, r, :]
             + g[:, 3 * h + 2:3 * h + 3] * obr_scr[1, r, :])
        o_ref[0, :, NSA_DK * h:NSA_DK * (h + 1)] = o.astype(o_ref.dtype)


def _nsa_tables(s, tq):
    nc = s // CMP_STRIDE
    nsb = s // SEL_LEN
    cs = np.arange(nc) * CMP_STRIDE
    ss = np.arange(nsb) * SEL_LEN
    ov = np.clip(np.minimum(cs[:, None] + CMP_LEN, ss[None, :] + SEL_LEN)
                 - np.maximum(cs[:, None], ss[None, :]), 0, None) / CMP_STRIDE
    ov[(s - CMP_LEN) // CMP_STRIDE + 1:] = 0.0
    ov_pad = np.zeros((nc, LANES), np.float32)
    ov_pad[:, :nsb] = ov
    ex = np.zeros((LANES, s), np.float32)
    ex[np.arange(s) // SEL_LEN, np.arange(s)] = 1.0
    return jnp.asarray(ov_pad, BF16), jnp.asarray(ex, BF16)


def _nsa(nq, nkv, trig, p, ts=512, tq=256, tc=512, chunk=128, kchunk=512):
    b, s, _ = nq.shape
    nc = s // CMP_STRIDE
    assert nc == LANES and s // SEL_LEN <= LANES and s >= WIN + tq and s % kchunk == 0
    tspec = pl.BlockSpec((1, ts, LANES), lambda i, j: (i, j, 0))
    q, ks, kw, vs, vw = pl.pallas_call(
        _nsa_prep_kernel,
        grid=(b, s // ts),
        in_specs=[pl.BlockSpec((1, ts, NQ_IN_W), lambda i, j: (i, j, 0)),
                  pl.BlockSpec((1, ts, NKV_IN_W), lambda i, j: (i, j, 0)),
                  tspec, tspec, _full((1, LANES)), _full((3, LANES))],
        out_specs=[pl.BlockSpec((1, NSA_HEADS, ts, LANES), lambda i, j: (i, 0, j, 0)),
                   tspec, tspec, tspec, tspec],
        out_shape=[jax.ShapeDtypeStruct((b, NSA_HEADS, s, LANES), BF16)]
        + [jax.ShapeDtypeStruct((b, s, LANES), BF16)] * 4,
        compiler_params=_cparams(("parallel", "parallel")),
        name="nsa_prep",
    )(nq, nkv, trig[0], trig[1], p["g_q"], p["g_k"])

    last = np.minimum(np.arange(nc) * CMP_STRIDE + CMP_LEN - 1, s - 1)
    cspec = pl.BlockSpec((1, nc, LANES), lambda i: (i, 0, 0))
    kc, vc = pl.pallas_call(
        _nsa_cmp_kernel,
        grid=(b,),
        in_specs=[pl.BlockSpec((1, s, LANES), lambda i: (i, 0, 0)),
                  pl.BlockSpec((1, s, LANES), lambda i: (i, 0, 3)),
                  cspec, cspec,
                  _full((CMP_LEN, LANES)), _full((CMP_LEN, LANES, CMP_HID)), _full((CMP_HID, LANES)),
                  _full((CMP_LEN, LANES)), _full((CMP_LEN, LANES, CMP_HID)), _full((CMP_HID, NSA_DK)),
                  _full((3, LANES))],
        out_specs=[cspec, pl.BlockSpec((1, nc, NSA_DK), lambda i: (i, 0, 0))],
        out_shape=[jax.ShapeDtypeStruct((b, nc, LANES), BF16),
                   jax.ShapeDtypeStruct((b, nc, NSA_DK), BF16)],
        compiler_params=_cparams(("parallel",)),
        name="nsa_compress",
    )(nkv, nkv, trig[0][:, last, :], trig[1][:, last, :], p["pe_k"], p["w1_k"], p["w2_k"], p["pe_v"],
      p["w1_v"], p["w2_v"], p["g_k"])

    ov, ex = _nsa_tables(s, tq)
    o_cmp, sel = pl.pallas_call(
        functools.partial(_nsa_cmpsel_kernel, tc=tc),
        grid=(b, s // tc),
        in_specs=[pl.BlockSpec((1, NSA_HEADS, tc, LANES), lambda i, j: (i, 0, j, 0)),
                  pl.BlockSpec((1, nc, LANES), lambda i, j: (i, 0, 0)),
                  pl.BlockSpec((1, nc, NSA_DK), lambda i, j: (i, 0, 0)),
                  _full((nc, LANES))],
        out_specs=[pl.BlockSpec((1, NSA_HEADS, tc, NSA_DK), lambda i, j: (i, 0, j, 0)),
                   pl.BlockSpec((1, tc, LANES), lambda i, j: (i, j, 0))],
        out_shape=[jax.ShapeDtypeStruct((b, NSA_HEADS, s, NSA_DK), F32),
                   jax.ShapeDtypeStruct((b, s, LANES), BF16)],
        compiler_params=_cparams(("parallel", "parallel")),
        name="nsa_cmpsel",
    )(q, kc, vc, ov)

    rows = NSA_HEADS * tq
    kvspec = pl.BlockSpec((1, s, LANES), lambda i, j: (i, 0, 0))
    return pl.pallas_call(
        functools.partial(_nsa_attn_kernel, tq=tq, chunk=chunk, kchunk=kchunk),
        grid=(b, s // tq),
        in_specs=[pl.BlockSpec((1, NSA_HEADS, tq, LANES), lambda i, j: (i, 0, j, 0)),
                  pl.BlockSpec((1, tq, LANES), lambda i, j: (i, j, 0)),
                  pl.BlockSpec((1, NSA_HEADS, tq, NSA_DK), lambda i, j: (i, 0, j, 0)),
                  kvspec, kvspec, kvspec, kvspec,
                  pl.BlockSpec((1, tq, LANES), lambda i, j: (i, j, NKV_IN_W // LANES - 1)),
                  _full((LANES, s))],
        out_specs=pl.BlockSpec((1, tq, GROUP_W), lambda i, j: (i, j, 0)),
        out_shape=jax.ShapeDtypeStruct((b, s, GROUP_W), BF16),
        scratch_shapes=[pltpu.VMEM((2, rows, s), F32), pltpu.VMEM((2, rows, s), BF16),
                        pltpu.VMEM((2, rows, LANES), F32), pltpu.VMEM((2, rows, NSA_DK), F32)],
        compiler_params=_cparams(("parallel", "arbitrary")),
        name="nsa_attn",
    )(q, sel, o_cmp, ks, vs, kw, vw, nkv, ex)


def _prep_nsa(g_q, g_k, pe_k, w1_k, w2_k, pe_v, w1_v, w2_v):
    def per_token(w1):
        w = w1.reshape(CMP_LEN, NSA_DK, CMP_HID)
        return jnp.pad(w, ((0, 0), (0, LANES - NSA_DK), (0, 0))).astype(BF16)

    return dict(g_q=_pad_last(g_q, LANES)[None, :], g_k=_pad_last(g_k, LANES),
                pe_k=_pad_last(pe_k, LANES), w1_k=per_token(w1_k),
                w2_k=_pad_last(w2_k, LANES).astype(BF16),
                pe_v=_pad_last(pe_v, LANES), w1_v=per_token(w1_v), w2_v=w2_v.astype(BF16))


MOE_BLOCK = 128


def _route_t(lt):
    row = lambda r: lt[r:r + 1, :]
    lg = [row(i) for i in range(N_GROUPS)]
    gmax = functools.reduce(jnp.maximum, lg)
    pg_top = 1.0 / functools.reduce(lambda a, b: a + b, [jnp.exp(v - gmax) for v in lg])
    taken = jnp.zeros_like(gmax) > 1.0
    oh = []
    for v in lg:
        hit = (v == gmax) & jnp.logical_not(taken)
        oh.append(hit)
        taken = taken | hit
    le = []
    for k in range(EXP_PER_GROUP):
        v = row(N_GROUPS + k)
        for i in range(1, N_GROUPS):
            v = jnp.where(oh[i], row(N_GROUPS + EXP_PER_GROUP * i + k), v)
        le.append(v)
    m1 = functools.reduce(jnp.maximum, le)
    taken = jnp.zeros_like(m1) > 1.0
    first = []
    for v in le:
        hit = (v == m1) & jnp.logical_not(taken)
        first.append(hit)
        taken = taken | hit
    le2 = [jnp.where(f, -jnp.inf, v) for f, v in zip(first, le)]
    m2 = functools.reduce(jnp.maximum, le2)
    taken = jnp.zeros_like(m1) > 1.0
    second = []
    for v in le2:
        hit = (v == m2) & jnp.logical_not(taken)
        second.append(hit)
        taken = taken | hit
    v2 = jnp.exp(m2 - m1)
    w1 = pg_top / (1.0 + v2)
    w2 = pg_top * v2 / (1.0 + v2)
    comb = []
    for i in range(N_GROUPS):
        for k in range(EXP_PER_GROUP):
            w = jnp.where(first[k], w1, 0.0) + jnp.where(second[k], w2, 0.0)
            comb.append(jnp.where(oh[i], w, 0.0))
    return [jnp.where(o, 1.0, 0.0) for o in oh], comb


def _moe_kernel(ya_ref, yb_ref, yc_ref, yd_ref, x_ref, gout_ref, wout_ref, g_ref, wr_ref,
                br_ref, tri_ref, wg_ref, wu_ref, wd_ref, o_ref, hs, cs, ys):
    tm = x_ref.shape[0]
    rs = hs.shape[0]
    x = x_ref[...]
    for i, y_ref in enumerate((ya_ref, yb_ref, yc_ref, yd_ref)):
        y = _rms(y_ref[...].astype(F32), gout_ref[i:i + 1, :], GROUP_W)
        x = x + _dot(y, wout_ref[GROUP_W * i:GROUP_W * (i + 1), :])
    o_ref[...] = x
    h = x * lax.rsqrt(jnp.mean(x * x, axis=-1, keepdims=True) + EPS) * g_ref[...]
    h_hi = h.astype(BF16)
    h_lo = (h - h_hi.astype(F32)).astype(BF16)
    lhl = jnp.dot(h_hi, wr_ref[...], preferred_element_type=F32)
    logits = (lhl[:, 0:LANES] + lhl[:, LANES:2 * LANES]
              + jnp.dot(h_lo, wr_ref[:, 0:LANES], preferred_element_type=F32)) + br_ref[...]
    oh, comb = _route_t(logits.T)

    oh8 = jnp.concatenate(oh + [jnp.zeros((8 - N_GROUPS, tm), F32)], axis=0)
    cum = jnp.dot(oh8.astype(BF16), tri_ref[...], preferred_element_type=F32)
    cnt = jnp.sum(oh8, axis=1, keepdims=True)
    padded = jnp.floor((cnt + (MOE_BLOCK - 1)) * (1.0 / MOE_BLOCK)) * MOE_BLOCK
    starts, acc0 = [], jnp.zeros((1, 1), F32)
    for i in range(N_GROUPS):
        starts.append(acc0)
        acc0 = acc0 + padded[i:i + 1, :]
    dest = functools.reduce(lambda a, b: a + b,
                            [oh[i] * (starts[i] + cum[i:i + 1, :] - 1.0) for i in range(N_GROUPS)])
    perm = jnp.where(lax.broadcasted_iota(jnp.int32, (rs, tm), 0).astype(F32) == dest, 1.0, 0.0)
    perm = perm.astype(BF16)

    stack = jnp.concatenate(comb + [dest] + [jnp.zeros((LANES - N_EXPERTS - 1, tm), F32)], axis=0)
    stack_t = stack.T
    c_hi = stack_t.astype(BF16)
    c_lo = (stack_t - c_hi.astype(F32)).astype(BF16)
    hs[...] = jnp.dot(perm, h_hi, preferred_element_type=F32).astype(BF16)
    cs[...] = jnp.dot(perm, jnp.concatenate([c_hi, c_lo], axis=1), preferred_element_type=F32)
    ys[...] = jnp.zeros_like(ys)

    def experts(i, r0, nrows):
        hb = hs[pl.ds(r0, nrows), :]
        cb = cs[pl.ds(r0, nrows), :]
        cb = cb[:, 0:LANES] + cb[:, LANES:2 * LANES]
        acc = None
        for k in range(EXP_PER_GROUP):
            e = EXP_PER_GROUP * i + k
            a = jax.nn.silu(jnp.dot(hb, wg_ref[e], preferred_element_type=F32)) \
                * jnp.dot(hb, wu_ref[e], preferred_element_type=F32)
            d = jnp.dot((a * cb[:, e:e + 1]).astype(BF16), wd_ref[e], preferred_element_type=F32)
            acc = d if acc is None else acc + d
        ys[pl.ds(r0, nrows), :] = acc.astype(BF16)

    for i in range(N_GROUPS):
        base = starts[i][0, 0].astype(jnp.int32)
        nblk = (padded[i:i + 1, :][0, 0] * (1.0 / MOE_BLOCK)).astype(jnp.int32)

        def pair(j, carry, i=i, base=base):
            experts(i, pl.multiple_of(base + j * (2 * MOE_BLOCK), MOE_BLOCK), 2 * MOE_BLOCK)
            return carry

        lax.fori_loop(0, nblk // 2, pair, 0)

        @pl.when(nblk % 2 == 1)
        def _(i=i, base=base, nblk=nblk):
            experts(i, pl.multiple_of(base + (nblk - 1) * MOE_BLOCK, MOE_BLOCK), MOE_BLOCK)

    dest_t = stack_t[:, N_EXPERTS:N_EXPERTS + 1]
    unperm = jnp.where(lax.broadcasted_iota(jnp.int32, (tm, rs), 1).astype(F32) == dest_t, 1.0, 0.0)
    o_ref[...] += jnp.dot(unperm.astype(BF16), ys[...], preferred_element_type=F32)


def _outproj_moe(ys, x2, g_out, w_out, p, tm=512):
    t = x2.shape[0]
    rs = tm + N_GROUPS * MOE_BLOCK
    yspec = pl.BlockSpec((tm, GROUP_W), lambda i: (i, 0))
    tri = jnp.asarray(np.triu(np.ones((tm, tm), np.float32)), BF16)

    def resident(shape):
        nd = len(shape)
        return pl.BlockSpec(shape, lambda i: (0,) * nd, pipeline_mode=pl.Buffered(1))

    return pl.pallas_call(
        _moe_kernel,
        grid=(t // tm,),
        in_specs=[yspec, yspec, yspec, yspec, pl.BlockSpec((tm, D_MODEL), lambda i: (i, 0)),
                  _full((4, GROUP_W)), resident((D_MODEL, D_MODEL)),
                  _full((1, D_MODEL)), _full((D_MODEL, 2 * LANES)),
                  _full((1, LANES)), resident((tm, tm)),
                  resident((N_EXPERTS, D_MODEL, D_EXPERT)), resident((N_EXPERTS, D_MODEL, D_EXPERT)),
                  resident((N_EXPERTS, D_EXPERT, D_MODEL))],
        out_specs=pl.BlockSpec((tm, D_MODEL), lambda i: (i, 0)),
        out_shape=jax.ShapeDtypeStruct((t, D_MODEL), F32),
        scratch_shapes=[pltpu.VMEM((rs, D_MODEL), BF16), pltpu.VMEM((rs, 2 * LANES), F32),
                        pltpu.VMEM((rs, D_MODEL), BF16)],
        compiler_params=_cparams(("parallel",)),
        name="outproj_moe",
    )(*ys, x2, g_out, w_out, p["g"], p["wr"], p["br"], tri, p["wg"], p["wu"], p["wd"])


def _prep_moe(g, w_rg, b_rg, w_re, b_re, w_gate, w_up, w_down):
    wr = _pad_last(jnp.concatenate([w_rg, w_re], axis=1), LANES)
    wr_hi = wr.astype(BF16)
    wr_lo = (wr - wr_hi.astype(F32)).astype(BF16)
    br = _pad_last(jnp.concatenate([b_rg, b_re]), LANES)[None, :]
    return dict(g=g[None, :], wr=jnp.concatenate([wr_hi, wr_lo], axis=1), br=br, wg=w_gate.astype(BF16),
                wu=w_up.astype(BF16), wd=w_down.astype(BF16))


def kernel(x, positions, mix_norm, w_in, mla_g_cq, mla_g_ckv, mla_w_uq, mla_w_ukv, mla_g_q, mla_g_k, lru_conv_w, lru_conv_b, lru_w_a, lru_b_a, lru_w_i, lru_b_i, lru_lambda, s5_a_re, s5_a_im, s5_log_dt, s5_b_re, s5_b_im, s5_c_re, s5_c_im, s5_d, s5_w_glu, s5_b_glu, nsa_g_q, nsa_g_k, nsa_pe_k, nsa_w1_k, nsa_w2_k, nsa_pe_v, nsa_w1_v, nsa_w2_v, out_norm, w_out, ffn_norm, moe_w_rg, moe_b_rg, moe_w_re, moe_b_re, moe_w_gate, moe_w_up, moe_w_down):
    b, s, d = x.shape
    t = b * s
    trig = _trig(positions.astype(jnp.int32)[:, :, None])
    x2 = x.reshape(t, d)
    for l in range(w_in.shape[0]):
        o_mla, o_lru, o_s5, o_nq, o_nkv = _inproj(x2, mix_norm[l], _prep_w_in(w_in[l]))
        y_a = _mla(o_mla.reshape(b, s, MLA_IN_W), trig,
                   _prep_mla(mla_g_cq[l], mla_g_ckv[l], mla_w_uq[l], mla_w_ukv[l], mla_g_q[l], mla_g_k[l]))
        y_b = _lru(o_lru.reshape(b, s, LRU_IN_W),
                   _prep_lru(lru_conv_w[l], lru_conv_b[l], lru_w_a[l], lru_b_a[l], lru_w_i[l], lru_b_i[l],
                             lru_lambda[l]))
        u_tm = o_s5.reshape(b, s, S5_W).transpose(1, 0, 2)
        y_c = _s5(u_tm, _prep_s5(s5_a_re[l], s5_a_im[l], s5_log_dt[l], s5_b_re[l], s5_b_im[l], s5_c_re[l],
                                 s5_c_im[l], s5_d[l], s5_w_glu[l], s5_b_glu[l])).transpose(1, 0, 2)
        y_d = _nsa(o_nq.reshape(b, s, NQ_IN_W), o_nkv.reshape(b, s, NKV_IN_W), trig,
                   _prep_nsa(nsa_g_q[l], nsa_g_k[l], nsa_pe_k[l], nsa_w1_k[l], nsa_w2_k[l], nsa_pe_v[l],
                             nsa_w1_v[l], nsa_w2_v[l]))
        ys = [y.reshape(t, GROUP_W) for y in (y_a, y_b, y_c, y_d)]
        x2 = _outproj_moe(ys, x2, out_norm[l], w_out[l].astype(BF16),
                          _prep_moe(ffn_norm[l], moe_w_rg[l], moe_b_rg[l], moe_w_re[l], moe_b_re[l],
                                    moe_w_gate[l], moe_w_up[l], moe_w_down[l]))
    return x2.reshape(b, s, d)
```

```python
import functools
import math

import numpy as np
import jax
import jax.numpy as jnp
from jax import lax
from jax.experimental import pallas as pl
from jax.experimental.pallas import tpu as pltpu

F32 = jnp.float32
BF16 = jnp.bfloat16

D_MODEL = 1024
DEPTH = 2
GROUP_W = 256
EPS = 1e-6
ROPE_THETA = 500000.0
NEG = -1e30
LOG2E = math.log2(math.e)

MLA_HEADS = 4
MLA_ROPE = 32
MLA_NOPE = 64
MLA_V = 64
MLA_QK = 96
MLA_Q_RANK = 192
MLA_KV_RANK = 128

LRU_W = 256
LRU_BLOCKS = 4
LRU_BW = 64
CONV_W = 4
LRU_C = 8.0

S5_W = 256
S5_CH = 16
S5_GROUPS = 16
S5_P = 64
S5_STATE = S5_GROUPS * S5_P

NSA_HEADS = 4
NSA_DK = 64
NSA_ROT = 16
CMP_LEN = 32
CMP_STRIDE = 16
CMP_HID = 128
SEL_LEN = 64
SEL_TOPK = 5
WIN = 512

N_GROUPS = 4
EXP_PER_GROUP = 4
N_EXPERTS = 16
D_EXPERT = 256

D_IN = 1772

LANES = 128
VMEM_LIMIT = 48 * 1024 * 1024

MLA_IN_W = 512
LRU_IN_W = 512
S5_IN_W = 256
NQ_IN_W = NSA_HEADS * LANES
NKV_IN_W = 6 * LANES
IN_W = MLA_IN_W + LRU_IN_W + S5_IN_W + NQ_IN_W + NKV_IN_W


def _cparams(sem):
    return pltpu.CompilerParams(dimension_semantics=sem, vmem_limit_bytes=VMEM_LIMIT)


def _dot(a, b):
    return jnp.dot(a.astype(BF16), b.astype(BF16), preferred_element_type=F32)


def _dot_nt(a, b):
    return lax.dot_general(a.astype(BF16), b.astype(BF16), (((1,), (1,)), ((), ())),
                           preferred_element_type=F32)


def _split3(x):
    hi = x.astype(BF16)
    r = x - hi.astype(F32)
    mid = r.astype(BF16)
    lo = (r - mid.astype(F32)).astype(BF16)
    return hi, mid, lo


def _dot_f32_by_exact(x, w_bf16):
    hi, mid, lo = _split3(x)
    return (jnp.dot(hi, w_bf16, preferred_element_type=F32)
            + jnp.dot(mid, w_bf16, preferred_element_type=F32)
            + jnp.dot(lo, w_bf16, preferred_element_type=F32))


def _rms(x, g, n):
    return x * lax.rsqrt(jnp.sum(x * x, axis=-1, keepdims=True) * (1.0 / n) + EPS) * g


def _lane_sum(x):
    hi = x.astype(BF16)
    lo = (x - hi.astype(F32)).astype(BF16)
    ones = jnp.ones((x.shape[-1], LANES), BF16)
    return (jnp.dot(hi, ones, preferred_element_type=F32) + jnp.dot(lo, ones, preferred_element_type=F32))


def _rms_mxu(x, g, n):
    inv = lax.rsqrt(_lane_sum(x * x) * (1.0 / n) + EPS)
    if x.shape[-1] > LANES:
        inv = jnp.concatenate([inv] * (x.shape[-1] // LANES), axis=-1)
    return x * inv * g


def _gelu(x):
    return 0.5 * x * (1.0 + jnp.tanh(math.sqrt(2.0 / math.pi) * (x + 0.044715 * (x * x * x))))


def _rope(x, cos, sin_lo, sin_hi, half):
    return (x * cos + pltpu.roll(x, LANES - half, axis=1) * sin_lo
            + pltpu.roll(x, half, axis=1) * sin_hi)


def _with_ones(x, upper):
    if upper:
        x = pltpu.roll(x, LANES // 2, axis=1)
    lane = lax.broadcasted_iota(jnp.int32, x.shape, 1)
    return jnp.where(lane < LANES // 2, x, 1.0).astype(BF16)


NSA_TRIG_LANE0 = MLA_ROPE


def _trig_kernel(pos_ref, inv_ref, cos_ref, sin_ref):
    ang = pos_ref[0].astype(F32) * inv_ref[...]
    cos_ref[0] = jnp.cos(ang)
    sin_ref[0] = jnp.sin(ang)


def _trig(pos3, ts=512):
    def inv(rot):
        v = ROPE_THETA ** (-jnp.arange(rot // 2, dtype=F32) * 2.0 / rot)
        return jnp.concatenate([v, v])
    inv_l = _pad_last(jnp.concatenate([inv(MLA_ROPE), inv(NSA_ROT)]), LANES)[None, :]
    b, s, _ = pos3.shape
    spec = pl.BlockSpec((1, ts, LANES), lambda i, j: (i, j, 0))
    return pl.pallas_call(
        _trig_kernel,
        grid=(b, s // ts),
        in_specs=[pl.BlockSpec((1, ts, 1), lambda i, j: (i, j, 0)), _full((1, LANES))],
        out_specs=[spec, spec],
        out_shape=[jax.ShapeDtypeStruct((b, s, LANES), F32)] * 2,
        compiler_params=_cparams(("parallel", "parallel")),
        name="rope_trig",
    )(pos3, inv_l)


def _rope_tables(cs, sn, lane0, half):
    if lane0:
        cs = pltpu.roll(cs, LANES - lane0, axis=1)
        sn = pltpu.roll(sn, LANES - lane0, axis=1)
    lane = lax.broadcasted_iota(jnp.int32, cs.shape, 1)
    cos = jnp.where(lane < 2 * half, cs, 1.0)
    s_lo = jnp.where(lane < half, -sn, 0.0)
    s_hi = jnp.where((lane >= half) & (lane < 2 * half), sn, 0.0)
    return cos, s_lo, s_hi


def _pad_last(a, n):
    return jnp.pad(a, [(0, 0)] * (a.ndim - 1) + [(0, n - a.shape[-1])])


def _full(shape):
    nd = len(shape)
    return pl.BlockSpec(shape, lambda *_: (0,) * nd)


def _proj_prep_kernel(x_ref, g_ref, w_ref, cs_ref, sn_ref, gcq_ref, wuq_ref, gckv_ref, wk_ref, wv_ref,
                      gqm_ref, gkm_ref, gqn_ref, gkn_ref,
                      mq_out, mk_out, mv_out, lru_out, s5_out, nq_out, ks_out, kw_out, vs_out, vw_out,
                      cg_out):
    x = x_ref[0]
    h = x * lax.rsqrt(jnp.mean(x * x, axis=-1, keepdims=True) + EPS) * g_ref[...]
    y = jnp.dot(h.astype(BF16), w_ref[...], preferred_element_type=F32)
    cs, sn = cs_ref[0], sn_ref[0]
    c0 = 0
    _mla_prep(y[:, c0:c0 + MLA_IN_W], cs, sn, gcq_ref, wuq_ref, gckv_ref, wk_ref, wv_ref, gqm_ref, gkm_ref,
              mq_out, mk_out, mv_out)
    c0 += MLA_IN_W
    lru_out[0] = y[:, c0:c0 + LRU_IN_W]
    c0 += LRU_IN_W
    s5_out[0] = y[:, c0:c0 + S5_IN_W].astype(BF16)
    c0 += S5_IN_W
    nkv = y[:, c0 + NQ_IN_W:c0 + NQ_IN_W + NKV_IN_W]
    _nsa_prep(y[:, c0:c0 + NQ_IN_W], nkv, cs, sn, gqn_ref, gkn_ref, nq_out, ks_out, kw_out, vs_out, vw_out)
    for i, slot in enumerate((0, 3, 5)):
        cg_out[0, :, LANES * i:LANES * (i + 1)] = nkv[:, LANES * slot:LANES * (slot + 1)]


def _inproj_cols():
    src = -np.ones((IN_W,), np.int64)
    o = 0
    src[o:o + 192] = np.arange(0, 192)
    src[o + 256:o + 384] = np.arange(192, 320)
    src[o + 384:o + 416] = np.arange(320, 352)
    o += MLA_IN_W
    src[o:o + 512] = np.arange(352, 864)
    o += LRU_IN_W
    src[o:o + 256] = np.arange(864, 1120)
    o += S5_IN_W
    for h in range(NSA_HEADS):
        src[o + LANES * h:o + LANES * h + 64] = np.arange(1120 + 64 * h, 1120 + 64 * h + 64)
    o += NQ_IN_W
    kv0 = 1376
    src[o:o + 64] = np.arange(kv0, kv0 + 64)
    src[o + 128:o + 192] = np.arange(kv0 + 128, kv0 + 192)
    src[o + 256:o + 320] = np.arange(kv0 + 256, kv0 + 320)
    src[o + 384:o + 448] = np.arange(kv0 + 64, kv0 + 128)
    src[o + 448:o + 512] = np.arange(kv0 + 192, kv0 + 256)
    src[o + 512:o + 576] = np.arange(kv0 + 320, kv0 + 384)
    src[o + 640:o + 652] = np.arange(1760, 1772)
    return src


_INPROJ_SRC = _inproj_cols()


def _prep_w_in(w_in):
    idx = jnp.asarray(np.maximum(_INPROJ_SRC, 0), jnp.int32)
    keep = jnp.asarray(_INPROJ_SRC >= 0)
    return jnp.where(keep[None, :], jnp.take(w_in, idx, axis=1), 0.0).astype(BF16)


CG_W = 3 * LANES


def _proj_prep(x3, g, w_pad, trig, pm, pn, ts=512):
    b, s, _ = x3.shape
    heads = lambda: pl.BlockSpec((1, MLA_HEADS, ts, LANES), lambda i, j: (i, 0, j, 0))
    rows = lambda w: pl.BlockSpec((1, ts, w), lambda i, j: (i, j, 0))
    hshape = jax.ShapeDtypeStruct((b, MLA_HEADS, s, LANES), BF16)
    tok = lambda w, dt: jax.ShapeDtypeStruct((b, s, w), dt)
    return pl.pallas_call(
        _proj_prep_kernel,
        grid=(b, s // ts),
        in_specs=[rows(D_MODEL), _full((1, D_MODEL)), _full((D_MODEL, IN_W)), rows(LANES), rows(LANES),
                  _full((1, 256)), _full((256, 512)), _full((1, 128)), _full((128, 512)),
                  _full((128, 256)), _full((1, 128)), _full((1, 128)),
                  _full((1, LANES)), _full((3, LANES))],
        out_specs=[heads(), heads(), heads(), rows(LRU_IN_W), rows(S5_IN_W),
                   heads(), rows(LANES), rows(LANES), rows(LANES), rows(LANES), rows(CG_W)],
        out_shape=[hshape, hshape, hshape, tok(LRU_IN_W, F32), tok(S5_IN_W, BF16),
                   hshape, tok(LANES, BF16), tok(LANES, BF16), tok(LANES, BF16), tok(LANES, BF16),
                   tok(CG_W, F32)],
        compiler_params=_cparams(("parallel", "parallel")),
        name="proj_prep",
    )(x3, g[None, :], w_pad, trig[0], trig[1], pm["g_cq"], pm["w_uq"], pm["g_ckv"], pm["w_k"], pm["w_v"],
      pm["g_q"], pm["g_k"], pn["g_q"], pn["g_k"])


def _mla_prep(xin, cs, sn, gcq_ref, wuq_ref, gckv_ref, wk_ref, wv_ref, gq_ref, gk_ref, q_out, k_out, v_out):
    cq = _rms_mxu(xin[:, 0:256], gcq_ref[...], MLA_Q_RANK)
    ckv = _rms_mxu(xin[:, 256:384], gckv_ref[...], MLA_KV_RANK)
    kpe = xin[:, 384:512]
    q = _dot(cq, wuq_ref[...])
    kn = _dot(ckv, wk_ref[...])
    v = _dot(ckv, wv_ref[...])
    cos, s_lo, s_hi = _rope_tables(cs, sn, 0, MLA_ROPE // 2)
    scale = MLA_QK ** -0.5 * LOG2E
    for h in range(MLA_HEADS):
        qh = _rms_mxu(q[:, LANES * h:LANES * (h + 1)], gq_ref[...], MLA_QK)
        q_out[0, h] = (_rope(qh, cos, s_lo, s_hi, MLA_ROPE // 2) * scale).astype(BF16)
        kh = _rms_mxu(kn[:, LANES * h:LANES * (h + 1)] + kpe, gk_ref[...], MLA_QK)
        k_out[0, h] = _rope(kh, cos, s_lo, s_hi, MLA_ROPE // 2).astype(BF16)
        v_out[0, h] = _with_ones(v[:, LANES * (h // 2):LANES * (h // 2 + 1)], h % 2 == 1)


def _attend(q, segs, v1, s_scr, p_scr, m_scr, nh, chunk, nsplit=2):
    rows = q.shape[0]
    part = rows // nsplit
    per_head = rows // nh
    outs = []
    for r_lo in range(0, rows, part):
        qs = q[r_lo:r_lo + part]
        c0 = 0
        m = None
        for k, bias in segs:
            n = k.shape[0]
            sb = _dot_nt(qs, k)
            if bias is not None and part >= per_head:
                sb = (sb.reshape(part // per_head, per_head, n) + bias[None]).reshape(part, n)
            elif bias is not None:
                b_lo = r_lo % per_head
                sb = sb + bias[b_lo:b_lo + part]
            s_scr[r_lo:r_lo + part, c0:c0 + n] = sb
            mx = jnp.max(sb, axis=-1, keepdims=True)
            m = mx if m is None else jnp.maximum(m, mx)
            c0 += n
        m_scr[r_lo:r_lo + part, :] = jnp.broadcast_to(m, (part, LANES))
        for r0 in range(r_lo, r_lo + part, chunk):
            mb = m_scr[r0:r0 + chunk, :]
            for j0 in range(0, c0, LANES):
                p = jnp.exp2(s_scr[r0:r0 + chunk, j0:j0 + LANES] - mb)
                p_scr[r0:r0 + chunk, j0:j0 + LANES] = p.astype(BF16)
        o = jnp.dot(p_scr[r_lo:r_lo + part, 0:c0], v1, preferred_element_type=F32)
        outs.append((o * (1.0 / pltpu.roll(o, LANES // 2, axis=1)))[:, 0:LANES // 2])
    return outs[0] if nsplit == 1 else jnp.concatenate(outs, axis=0)


def _mla_attn_kernel(q_ref, k_ref, v_ref, o_ref, s_scr, p_scr, m_scr, o_scr, bias_scr, *, tq, chunk):
    qi = pl.program_id(1)
    row = lax.broadcasted_iota(jnp.int32, (tq, tq), 0)
    col = lax.broadcasted_iota(jnp.int32, (tq, tq), 1)
    bias_scr[...] = jnp.where(col <= row, 0.0, NEG)
    nset = s_scr.shape[0]
    for c in range(k_ref.shape[2] // tq):

        @pl.when(qi == c)
        def _(c=c):
            n = (c + 1) * tq

            def heads(i, carry):
                for j in range(nset):
                    h = i * nset + j
                    segs = [(k_ref[0, h, n - tq:n, :], bias_scr[...])]
                    if c > 0:
                        segs = [(k_ref[0, h, 0:n - tq, :], None)] + segs
                    o_scr[h] = _attend(q_ref[0, h], segs, v_ref[0, h, 0:n, :], s_scr.at[j], p_scr.at[j],
                                       m_scr.at[j], 1, chunk)
                return carry

            lax.fori_loop(0, MLA_HEADS // nset, heads, 0)

    for h in range(MLA_HEADS):
        o_ref[0, :, MLA_V * h:MLA_V * (h + 1)] = o_scr[h].astype(o_ref.dtype)


def _mla(q, k, v, tq=512, chunk=128, nset=2):
    b, _, s, _ = q.shape
    return pl.pallas_call(
        functools.partial(_mla_attn_kernel, tq=tq, chunk=chunk),
        grid=(b, s // tq),
        in_specs=[pl.BlockSpec((1, MLA_HEADS, tq, LANES), lambda i, j: (i, 0, j, 0)),
                  pl.BlockSpec((1, MLA_HEADS, s, LANES), lambda i, j: (i, 0, 0, 0)),
                  pl.BlockSpec((1, MLA_HEADS, s, LANES), lambda i, j: (i, 0, 0, 0))],
        out_specs=pl.BlockSpec((1, tq, GROUP_W), lambda i, j: (i, j, 0)),
        out_shape=jax.ShapeDtypeStruct((b, s, GROUP_W), BF16),
        scratch_shapes=[pltpu.VMEM((nset, tq, s), F32), pltpu.VMEM((nset, tq, s), BF16),
                        pltpu.VMEM((nset, tq, LANES), F32),
                        pltpu.VMEM((MLA_HEADS, tq, MLA_V), F32), pltpu.VMEM((tq, tq), F32)],
        compiler_params=_cparams(("parallel", "arbitrary")),
        name="mla_attn",
    )(q, k, v)


def _prep_mla(g_cq, g_ckv, w_uq, w_ukv, g_q, g_k):
    wq = w_uq.reshape(MLA_Q_RANK, MLA_HEADS, MLA_QK)
    wq = _pad_last(wq, LANES).reshape(MLA_Q_RANK, MLA_HEADS * LANES)
    wq = jnp.pad(wq, ((0, 256 - MLA_Q_RANK), (0, 0)))
    wkv = w_ukv.reshape(MLA_KV_RANK, MLA_HEADS, MLA_NOPE + MLA_V)
    wk = jnp.pad(wkv[:, :, :MLA_NOPE], ((0, 0), (0, 0), (MLA_ROPE, LANES - MLA_QK)))
    wk = wk.reshape(MLA_KV_RANK, MLA_HEADS * LANES)
    wv = wkv[:, :, MLA_NOPE:].reshape(MLA_KV_RANK, MLA_HEADS * MLA_V)
    return dict(g_cq=_pad_last(g_cq, 256)[None, :], g_ckv=g_ckv[None, :], w_uq=wq.astype(BF16),
                w_k=wk.astype(BF16), w_v=wv.astype(BF16), g_q=_pad_last(g_q, LANES)[None, :],
                g_k=_pad_last(g_k, LANES)[None, :])


def _shift_rows(x, k, row, fill):
    return jnp.where(row >= k, pltpu.roll(x, k, axis=0), fill)


def _lru_kernel(in_ref, cw_ref, cb_ref, wa_ref, ba_ref, wi_ref, bi_ref, lam_ref, o_ref, a_scr, b_scr, *,
                chunk):
    xin = in_ref[0]
    s = xin.shape[0]
    xb = xin[:, :LRU_W]
    row = lax.broadcasted_iota(jnp.int32, (s, LRU_W), 0)
    u = cb_ref[...] + xb * cw_ref[CONV_W - 1:CONV_W, :]
    for j in range(CONV_W - 1):
        u = u + _shift_rows(xb, CONV_W - 1 - j, row, 0.0) * cw_ref[j:j + 1, :]
    r = jax.nn.sigmoid(_dot(u, wa_ref[...]) + ba_ref[...])
    gi = jax.nn.sigmoid(_dot(u, wi_ref[...]) + bi_ref[...])
    nlam = -lam_ref[...]
    softplus = jnp.maximum(nlam, 0.0) + jnp.log1p(jnp.exp(-jnp.abs(nlam)))
    log_a = (-LRU_C) * r * softplus
    a = jnp.exp(log_a)
    y = jnp.maximum(-jnp.tanh(log_a) * (a * a + 1.0), 0.0)
    mult = jnp.where(y > 0.0, y * lax.rsqrt(y), 0.0)
    mult = jnp.where(row == 0, 1.0, mult)
    bt = mult * gi * u
    rin = jnp.bitwise_and(row, chunk - 1)
    k = 1
    while k < chunk:
        bt = a * _shift_rows(bt, k, rin, 0.0) + bt
        a = a * _shift_rows(a, k, rin, 1.0)
        k *= 2
    a_scr[...] = a
    b_scr[...] = bt

    def carry_chunk(j, h_prev):
        r0 = pl.multiple_of(j * chunk, chunk)
        h = b_scr[pl.ds(r0, chunk), :] + a_scr[pl.ds(r0, chunk), :] * h_prev
        o_ref[0, pl.ds(r0, chunk), :] = (h * _gelu(in_ref[0, pl.ds(r0, chunk), LRU_W:])).astype(o_ref.dtype)
        return h[chunk - 1:chunk, :]

    lax.fori_loop(0, s // chunk, carry_chunk, jnp.zeros((1, LRU_W), F32))


def _block_diag(w):
    n, i, j = w.shape
    eye = jnp.eye(n, dtype=w.dtype)
    return (eye[:, None, :, None] * w[:, :, None, :]).reshape(n * i, n * j)


def _lru(lru_in, p, chunk=64):
    b, s, _ = lru_in.shape
    return pl.pallas_call(
        functools.partial(_lru_kernel, chunk=chunk),
        grid=(b,),
        in_specs=[pl.BlockSpec((1, s, LRU_IN_W), lambda i: (i, 0, 0)),
                  _full((CONV_W, LRU_W)), _full((1, LRU_W)), _full((LRU_W, LRU_W)), _full((1, LRU_W)),
                  _full((LRU_W, LRU_W)), _full((1, LRU_W)), _full((1, LRU_W))],
        out_specs=pl.BlockSpec((1, s, LRU_W), lambda i: (i, 0, 0)),
        out_shape=jax.ShapeDtypeStruct((b, s, LRU_W), BF16),
        scratch_shapes=[pltpu.VMEM((s, LRU_W), F32), pltpu.VMEM((s, LRU_W), F32)],
        compiler_params=_cparams(("parallel",)),
        name="rglru",
    )(lru_in, p["cw"], p["cb"], p["wa"], p["ba"], p["wi"], p["bi"], p["lam"])


def _prep_lru(conv_w, conv_b, w_a, b_a, w_i, b_i, lam):
    return dict(cw=conv_w, cb=conv_b[None, :], wa=_block_diag(w_a).astype(BF16),
                ba=b_a.reshape(1, LRU_W), wi=_block_diag(w_i).astype(BF16), bi=b_i.reshape(1, LRU_W),
                lam=lam[None, :])


def _s5_disc_kernel(are_ref, aim_ref, ldt_ref, arer_ref, aimr_ref, bre_ref, bim_ref,
                    abre_ref, abim_ref, bbre_ref, bbim_ref):
    dt = jnp.exp(ldt_ref[...])

    def disc(a_re, a_im):
        mag = jnp.exp(dt * a_re)
        ab_re = mag * jnp.cos(dt * a_im)
        ab_im = mag * jnp.sin(dt * a_im)
        den = a_re * a_re + a_im * a_im
        n_re = ab_re - 1.0
        g_re = (n_re * a_re + ab_im * a_im) / den
        g_im = (ab_im * a_re - n_re * a_im) / den
        return ab_re, ab_im, g_re, g_im

    ab_re, ab_im, _, _ = disc(are_ref[...], aim_ref[...])
    abre_ref[...] = ab_re
    abim_ref[...] = ab_im
    _, _, g_re, g_im = disc(arer_ref[...], aimr_ref[...])
    bbre_ref[...] = g_re * bre_ref[...] - g_im * bim_ref[...]
    bbim_ref[...] = g_re * bim_ref[...] + g_im * bre_ref[...]


def _s5_kernel(u_ref, are_ref, aim_ref, bre_ref, bim_ref, cre_ref, cim_ref, d_ref, wg_ref, bg_ref,
               o_ref, hre, him, st_re, st_im, *, tc, nb, cw):
    @pl.when(pl.program_id(0) == 0)
    def _():
        st_re[...] = jnp.zeros_like(st_re)
        st_im[...] = jnp.zeros_like(st_im)

    ub = u_ref[...].reshape(tc * nb, S5_W)
    u = ub.astype(F32)
    hre[...] = jnp.dot(ub, bre_ref[...], preferred_element_type=F32)
    him[...] = jnp.dot(ub, bim_ref[...], preferred_element_type=F32)
    for c in range(S5_STATE // cw):
        cs = slice(c * cw, (c + 1) * cw)
        ar = jnp.broadcast_to(are_ref[:, cs], (nb, cw))
        ai = jnp.broadcast_to(aim_ref[:, cs], (nb, cw))

        def body(t, carry, cs=cs, ar=ar, ai=ai):
            hr, hi = carry
            r0 = pl.multiple_of(t * nb, nb)
            nr = ar * hr - ai * hi + hre[pl.ds(r0, nb), cs]
            ni = ar * hi + ai * hr + him[pl.ds(r0, nb), cs]
            hre[pl.ds(r0, nb), cs] = nr
            him[pl.ds(r0, nb), cs] = ni
            return nr, ni

        hr, hi = lax.fori_loop(0, tc, body, (st_re[:, cs], st_im[:, cs]), unroll=4)
        st_re[:, cs] = hr
        st_im[:, cs] = hi
    y = (jnp.dot(hre[...].astype(BF16), cre_ref[...], preferred_element_type=F32)
         - jnp.dot(him[...].astype(BF16), cim_ref[...], preferred_element_type=F32))
    y = _gelu(y + d_ref[...] * u)
    z = _dot(y, wg_ref[...]) + bg_ref[...]
    o_ref[...] = (y * jax.nn.sigmoid(z)).astype(o_ref.dtype).reshape(tc, nb, S5_W)


def _prep_s5(a_re, a_im, log_dt, b_re, b_im, c_re, c_im, d, w_glu, b_glu):
    g, p, ch = S5_GROUPS, S5_P, S5_CH
    ab_re, ab_im, bb_re, bb_im = pl.pallas_call(
        _s5_disc_kernel,
        out_shape=[jax.ShapeDtypeStruct((g, p), F32), jax.ShapeDtypeStruct((g, p), F32),
                   jax.ShapeDtypeStruct((g, p * ch), F32), jax.ShapeDtypeStruct((g, p * ch), F32)],
        name="s5_discretize",
    )(a_re, a_im, log_dt[:, None], jnp.repeat(a_re, ch, axis=1), jnp.repeat(a_im, ch, axis=1),
      b_re.reshape(g, p * ch), b_im.reshape(g, p * ch))

    def b_dense(bb):
        return _block_diag(bb.reshape(g, p, ch).transpose(0, 2, 1)).astype(BF16)

    def c_dense(c):
        return _block_diag(c.transpose(0, 2, 1)).astype(BF16)

    return dict(a_re=ab_re.reshape(1, S5_STATE), a_im=ab_im.reshape(1, S5_STATE),
                b_re=b_dense(bb_re), b_im=b_dense(bb_im), c_re=c_dense(c_re), c_im=c_dense(c_im),
                d=d[None, :], w_glu=w_glu.astype(BF16), b_glu=b_glu[None, :])


def _s5(u_tm, p, tc=64, cw=256):
    s, nb, _ = u_tm.shape
    return pl.pallas_call(
        functools.partial(_s5_kernel, tc=tc, nb=nb, cw=cw),
        grid=(s // tc,),
        in_specs=[pl.BlockSpec((tc, nb, S5_W), lambda i: (i, 0, 0)),
                  _full((1, S5_STATE)), _full((1, S5_STATE)),
                  _full((S5_W, S5_STATE)), _full((S5_W, S5_STATE)),
                  _full((S5_STATE, S5_W)), _full((S5_STATE, S5_W)),
                  _full((1, S5_W)), _full((S5_W, S5_W)), _full((1, S5_W))],
        out_specs=pl.BlockSpec((tc, nb, S5_W), lambda i: (i, 0, 0)),
        out_shape=jax.ShapeDtypeStruct((s, nb, S5_W), BF16),
        scratch_shapes=[pltpu.VMEM((tc * nb, S5_STATE), F32), pltpu.VMEM((tc * nb, S5_STATE), F32),
                        pltpu.VMEM((nb, S5_STATE), F32), pltpu.VMEM((nb, S5_STATE), F32)],
        compiler_params=_cparams(("arbitrary",)),
        name="s5",
    )(u_tm, p["a_re"], p["a_im"], p["b_re"], p["b_im"], p["c_re"], p["c_im"], p["d"], p["w_glu"],
      p["b_glu"])


def _nsa_prep(nq, nkv, cs, sn, gq_ref, gk_ref, q_out, ks_out, kw_out, vs_out, vw_out):
    half = NSA_ROT // 2
    cos, s_lo, s_hi = _rope_tables(cs, sn, NSA_TRIG_LANE0, half)
    scale = NSA_DK ** -0.5 * LOG2E
    for h in range(NSA_HEADS):
        qh = _rms_mxu(nq[:, LANES * h:LANES * (h + 1)], gq_ref[...], NSA_DK)
        q_out[0, h] = (_rope(qh, cos, s_lo, s_hi, half) * scale).astype(BF16)
    ks = _rms_mxu(nkv[:, 128:256], gk_ref[1:2, :], NSA_DK)
    ks_out[0] = _rope(ks, cos, s_lo, s_hi, half).astype(BF16)
    kw = _rms_mxu(nkv[:, 256:384], gk_ref[2:3, :], NSA_DK)
    kw_out[0] = _rope(kw, cos, s_lo, s_hi, half).astype(BF16)
    vs_out[0] = _with_ones(nkv[:, 384:512], True)
    vw_out[0] = _with_ones(nkv[:, 512:640], False)


def _nsa_cmp_kernel(k_in, v_in, cs_ref, sn_ref, pek_ref, w1k_ref, w2k_ref, pev_ref, w1v_ref, w2v_ref,
                    g_ref, kc_out, vc_out):
    nc = kc_out.shape[1]

    def compress(x_ref, pe_ref, w1_ref, w2_ref):
        lo = hi = None
        for j in range(CMP_STRIDE):
            xj = x_ref[0, pl.ds(j, nc, stride=CMP_STRIDE), :]
            dl = _dot(xj + pe_ref[j:j + 1, :], w1_ref[j])
            dh = _dot(xj + pe_ref[CMP_STRIDE + j:CMP_STRIDE + j + 1, :], w1_ref[CMP_STRIDE + j])
            lo = dl if lo is None else lo + dl
            hi = dh if hi is None else hi + dh
        hid = lo + pltpu.roll(hi, nc - 1, axis=0)
        return _dot(_gelu(hid), w2_ref[...])

    kc = compress(k_in, pek_ref, w1k_ref, w2k_ref)
    cos, s_lo, s_hi = _rope_tables(cs_ref[0], sn_ref[0], NSA_TRIG_LANE0, NSA_ROT // 2)
    kc = _rope(_rms(kc, g_ref[0:1, :], NSA_DK), cos, s_lo, s_hi, NSA_ROT // 2)
    kc_out[0] = kc.astype(BF16)
    vc_out[0] = compress(v_in, pev_ref, w1v_ref, w2v_ref).astype(BF16)


def _nsa_cmpsel_kernel(q_ref, kc_ref, vc_ref, ov_ref, ocmp_ref, sel_ref, *, tc):
    qi = pl.program_id(1)
    nh = NSA_HEADS
    q = q_ref[0].reshape(nh * tc, LANES)
    lane = lax.broadcasted_iota(jnp.int32, (tc, LANES), 1)
    qpos = qi * tc + lax.broadcasted_iota(jnp.int32, (tc, LANES), 0)

    s = _dot_nt(q, kc_ref[0]).reshape(nh, tc, LANES)
    valid = (lane * CMP_STRIDE + (CMP_LEN - 1)) <= qpos
    s = jnp.where(valid, s, NEG)
    e = jnp.exp2(s - jnp.max(s, axis=-1, keepdims=True))
    p_c = jnp.where(valid, e / jnp.sum(e, axis=-1, keepdims=True), 0.0)
    ocmp_ref[0] = jnp.dot(p_c.astype(BF16).reshape(nh * tc, LANES), vc_ref[0],
                          preferred_element_type=F32).reshape(nh, tc, NSA_DK)

    imp = _dot_f32_by_exact(jnp.sum(p_c, axis=0), ov_ref[...])
    cur = qpos // SEL_LEN
    lane_f = lane.astype(F32)
    sel = (lane == 0) | (lane == cur) | (lane == cur - 1)
    cand = (lane < cur - 1) & (lane > 0)
    for _ in range(SEL_TOPK - 3):
        sc = jnp.where(cand, imp, -jnp.inf)
        best = jnp.max(sc, axis=-1, keepdims=True)
        pick = jnp.min(jnp.where(cand & (sc == best), lane_f, float(LANES)), axis=-1, keepdims=True)
        hit = lane_f == pick
        sel = sel | hit
        cand = cand & jnp.logical_not(hit)
    sel_ref[0] = jnp.where(sel, 1.0, 0.0).astype(BF16)


def _nsa_attn_kernel(q_ref, sel_ref, ocmp_ref, ks_ref, vs_ref, kw_ref, vw_ref, gate_ref, ex_ref,
                     o_ref, s_scr, p_scr, m_scr, obr_scr, *, tq, chunk, kchunk):
    qi = pl.program_id(1)
    nh = NSA_HEADS
    rows = nh * tq
    s_len = ks_ref.shape[1]

    def window():
        wk = WIN + tq
        start = pl.multiple_of(jnp.maximum(qi - WIN // tq, 0) * tq, tq)
        kpos = start + lax.broadcasted_iota(jnp.int32, (tq, wk), 1)
        qrow = qi * tq + lax.broadcasted_iota(jnp.int32, (tq, wk), 0)
        bias = jnp.where((kpos <= qrow) & (qrow - kpos < WIN), 0.0, NEG)
        obr_scr[1] = _attend(q_ref[0].reshape(rows, LANES), [(kw_ref[0, pl.ds(start, wk), :], bias)],
                             vw_ref[0, pl.ds(start, wk), :], s_scr.at[1], p_scr.at[1], m_scr.at[1],
                             nh, chunk, nsplit=1)

    for c in range(s_len // kchunk):

        @pl.when(qi // (kchunk // tq) == c)
        def _(c=c):
            n = (c + 1) * kchunk
            em = jnp.dot(sel_ref[0], ex_ref[:, 0:n], preferred_element_type=F32)
            kpos = lax.broadcasted_iota(jnp.int32, (tq, n), 1)
            qrow = qi * tq + lax.broadcasted_iota(jnp.int32, (tq, n), 0)
            bias = jnp.where((em > 0.5) & (kpos <= qrow), 0.0, NEG)
            obr_scr[0] = _attend(q_ref[0].reshape(rows, LANES), [(ks_ref[0, 0:n, :], bias)],
                                 vs_ref[0, 0:n, :], s_scr.at[0], p_scr.at[0], m_scr.at[0], nh, chunk, nsplit=1)
            window()

    g = jax.nn.sigmoid(gate_ref[0])
    for h in range(nh):
        r = slice(h * tq, (h + 1) * tq)
        o = (g[:, 3 * h:3 * h + 1] * ocmp_ref[0, h] + g[:, 3 * h + 1:3 * h + 2] * obr_scr[0, r, :]
             + g[:, 3 * h + 2:3 * h + 3] * obr_scr[1, r, :])
        o_ref[0, :, NSA_DK * h:NSA_DK * (h + 1)] = o.astype(o_ref.dtype)


def _nsa_tables(s, tq):
    nc = s // CMP_STRIDE
    nsb = s // SEL_LEN
    cs = np.arange(nc) * CMP_STRIDE
    ss = np.arange(nsb) * SEL_LEN
    ov = np.clip(np.minimum(cs[:, None] + CMP_LEN, ss[None, :] + SEL_LEN)
                 - np.maximum(cs[:, None], ss[None, :]), 0, None) / CMP_STRIDE
    ov[(s - CMP_LEN) // CMP_STRIDE + 1:] = 0.0
    ov_pad = np.zeros((nc, LANES), np.float32)
    ov_pad[:, :nsb] = ov
    ex = np.zeros((LANES, s), np.float32)
    ex[np.arange(s) // SEL_LEN, np.arange(s)] = 1.0
    return jnp.asarray(ov_pad, BF16), jnp.asarray(ex, BF16)


def _nsa(q, ks, kw, vs, vw, cg, trig, p, tq=256, tc=512, chunk=128, kchunk=512):
    b, s, _ = ks.shape
    nc = s // CMP_STRIDE
    assert nc == LANES and s // SEL_LEN <= LANES and s >= WIN + tq and s % kchunk == 0

    last = np.minimum(np.arange(nc) * CMP_STRIDE + CMP_LEN - 1, s - 1)
    cspec = pl.BlockSpec((1, nc, LANES), lambda i: (i, 0, 0))
    kc, vc = pl.pallas_call(
        _nsa_cmp_kernel,
        grid=(b,),
        in_specs=[pl.BlockSpec((1, s, LANES), lambda i: (i, 0, 0)),
                  pl.BlockSpec((1, s, LANES), lambda i: (i, 0, 1)),
                  cspec, cspec,
                  _full((CMP_LEN, LANES)), _full((CMP_LEN, LANES, CMP_HID)), _full((CMP_HID, LANES)),
                  _full((CMP_LEN, LANES)), _full((CMP_LEN, LANES, CMP_HID)), _full((CMP_HID, NSA_DK)),
                  _full((3, LANES))],
        out_specs=[cspec, pl.BlockSpec((1, nc, NSA_DK), lambda i: (i, 0, 0))],
        out_shape=[jax.ShapeDtypeStruct((b, nc, LANES), BF16),
                   jax.ShapeDtypeStruct((b, nc, NSA_DK), BF16)],
        compiler_params=_cparams(("parallel",)),
        name="nsa_compress",
    )(cg, cg, trig[0][:, last, :], trig[1][:, last, :], p["pe_k"], p["w1_k"], p["w2_k"], p["pe_v"],
      p["w1_v"], p["w2_v"], p["g_k"])

    ov, ex = _nsa_tables(s, tq)
    o_cmp, sel = pl.pallas_call(
        functools.partial(_nsa_cmpsel_kernel, tc=tc),
        grid=(b, s // tc),
        in_specs=[pl.BlockSpec((1, NSA_HEADS, tc, LANES), lambda i, j: (i, 0, j, 0)),
                  pl.BlockSpec((1, nc, LANES), lambda i, j: (i, 0, 0)),
                  pl.BlockSpec((1, nc, NSA_DK), lambda i, j: (i, 0, 0)),
                  _full((nc, LANES))],
        out_specs=[pl.BlockSpec((1, NSA_HEADS, tc, NSA_DK), lambda i, j: (i, 0, j, 0)),
                   pl.BlockSpec((1, tc, LANES), lambda i, j: (i, j, 0))],
        out_shape=[jax.ShapeDtypeStruct((b, NSA_HEADS, s, NSA_DK), F32),
                   jax.ShapeDtypeStruct((b, s, LANES), BF16)],
        compiler_params=_cparams(("parallel", "parallel")),
        name="nsa_cmpsel",
    )(q, kc, vc, ov)

    rows = NSA_HEADS * tq
    kvspec = pl.BlockSpec((1, s, LANES), lambda i, j: (i, 0, 0))
    return pl.pallas_call(
        functools.partial(_nsa_attn_kernel, tq=tq, chunk=chunk, kchunk=kchunk),
        grid=(b, s // tq),
        in_specs=[pl.BlockSpec((1, NSA_HEADS, tq, LANES), lambda i, j: (i, 0, j, 0)),
                  pl.BlockSpec((1, tq, LANES), lambda i, j: (i, j, 0)),
                  pl.BlockSpec((1, NSA_HEADS, tq, NSA_DK), lambda i, j: (i, 0, j, 0)),
                  kvspec, kvspec, kvspec, kvspec,
                  pl.BlockSpec((1, tq, LANES), lambda i, j: (i, j, CG_W // LANES - 1)),
                  _full((LANES, s))],
        out_specs=pl.BlockSpec((1, tq, GROUP_W), lambda i, j: (i, j, 0)),
        out_shape=jax.ShapeDtypeStruct((b, s, GROUP_W), BF16),
        scratch_shapes=[pltpu.VMEM((2, rows, s), F32), pltpu.VMEM((2, rows, s), BF16),
                        pltpu.VMEM((2, rows, LANES), F32), pltpu.VMEM((2, rows, NSA_DK), F32)],
        compiler_params=_cparams(("parallel", "arbitrary")),
        name="nsa_attn",
    )(q, sel, o_cmp, ks, vs, kw, vw, cg, ex)


def _prep_nsa(g_q, g_k, pe_k, w1_k, w2_k, pe_v, w1_v, w2_v):
    def per_token(w1):
        w = w1.reshape(CMP_LEN, NSA_DK, CMP_HID)
        return jnp.pad(w, ((0, 0), (0, LANES - NSA_DK), (0, 0))).astype(BF16)

    return dict(g_q=_pad_last(g_q, LANES)[None, :], g_k=_pad_last(g_k, LANES),
                pe_k=_pad_last(pe_k, LANES), w1_k=per_token(w1_k),
                w2_k=_pad_last(w2_k, LANES).astype(BF16),
                pe_v=_pad_last(pe_v, LANES), w1_v=per_token(w1_v), w2_v=w2_v.astype(BF16))


MOE_BLOCK = 128


def _route_t(lt):
    row = lambda r: lt[r:r + 1, :]
    lg = [row(i) for i in range(N_GROUPS)]
    gmax = functools.reduce(jnp.maximum, lg)
    pg_top = 1.0 / functools.reduce(lambda a, b: a + b, [jnp.exp(v - gmax) for v in lg])
    taken = jnp.zeros_like(gmax) > 1.0
    oh = []
    for v in lg:
        hit = (v == gmax) & jnp.logical_not(taken)
        oh.append(hit)
        taken = taken | hit
    le = []
    for k in range(EXP_PER_GROUP):
        v = row(N_GROUPS + k)
        for i in range(1, N_GROUPS):
            v = jnp.where(oh[i], row(N_GROUPS + EXP_PER_GROUP * i + k), v)
        le.append(v)
    m1 = functools.reduce(jnp.maximum, le)
    taken = jnp.zeros_like(m1) > 1.0
    first = []
    for v in le:
        hit = (v == m1) & jnp.logical_not(taken)
        first.append(hit)
        taken = taken | hit
    le2 = [jnp.where(f, -jnp.inf, v) for f, v in zip(first, le)]
    m2 = functools.reduce(jnp.maximum, le2)
    taken = jnp.zeros_like(m1) > 1.0
    second = []
    for v in le2:
        hit = (v == m2) & jnp.logical_not(taken)
        second.append(hit)
        taken = taken | hit
    v2 = jnp.exp(m2 - m1)
    w1 = pg_top / (1.0 + v2)
    w2 = pg_top * v2 / (1.0 + v2)
    comb = []
    for i in range(N_GROUPS):
        for k in range(EXP_PER_GROUP):
            w = jnp.where(first[k], w1, 0.0) + jnp.where(second[k], w2, 0.0)
            comb.append(jnp.where(oh[i], w, 0.0))
    return [jnp.where(o, 1.0, 0.0) for o in oh], comb


def _moe_kernel(ya_ref, yb_ref, yc_ref, yd_ref, x_ref, gout_ref, wout_ref, g_ref, wr_ref,
                br_ref, tri_ref, wg_ref, wu_ref, wd_ref, o_ref, hs, cs, ys):
    tm = x_ref.shape[0]
    rs = hs.shape[0]
    x = x_ref[...]
    for i, y_ref in enumerate((ya_ref, yb_ref, yc_ref, yd_ref)):
        y = _rms(y_ref[...].astype(F32), gout_ref[i:i + 1, :], GROUP_W)
        x = x + _dot(y, wout_ref[GROUP_W * i:GROUP_W * (i + 1), :])
    o_ref[...] = x
    h = x * lax.rsqrt(jnp.mean(x * x, axis=-1, keepdims=True) + EPS) * g_ref[...]
    h_hi = h.astype(BF16)
    h_lo = (h - h_hi.astype(F32)).astype(BF16)
    lhl = jnp.dot(h_hi, wr_ref[...], preferred_element_type=F32)
    logits = (lhl[:, 0:LANES] + lhl[:, LANES:2 * LANES]
              + jnp.dot(h_lo, wr_ref[:, 0:LANES], preferred_element_type=F32)) + br_ref[...]
    oh, comb = _route_t(logits.T)

    oh8 = jnp.concatenate(oh + [jnp.zeros((8 - N_GROUPS, tm), F32)], axis=0)
    cum = jnp.dot(oh8.astype(BF16), tri_ref[...], preferred_element_type=F32)
    cnt = jnp.sum(oh8, axis=1, keepdims=True)
    padded = jnp.floor((cnt + (MOE_BLOCK - 1)) * (1.0 / MOE_BLOCK)) * MOE_BLOCK
    starts, acc0 = [], jnp.zeros((1, 1), F32)
    for i in range(N_GROUPS):
        starts.append(acc0)
        acc0 = acc0 + padded[i:i + 1, :]
    dest = functools.reduce(lambda a, b: a + b,
                            [oh[i] * (starts[i] + cum[i:i + 1, :] - 1.0) for i in range(N_GROUPS)])
    perm = jnp.where(lax.broadcasted_iota(jnp.int32, (rs, tm), 0).astype(F32) == dest, 1.0, 0.0)
    perm = perm.astype(BF16)

    stack = jnp.concatenate(comb + [dest] + [jnp.zeros((LANES - N_EXPERTS - 1, tm), F32)], axis=0)
    stack_t = stack.T
    c_hi = stack_t.astype(BF16)
    c_lo = (stack_t - c_hi.astype(F32)).astype(BF16)
    hs[...] = jnp.dot(perm, h_hi, preferred_element_type=F32).astype(BF16)
    cs[...] = jnp.dot(perm, jnp.concatenate([c_hi, c_lo], axis=1), preferred_element_type=F32)
    ys[...] = jnp.zeros_like(ys)

    def experts(i, r0, nrows):
        hb = hs[pl.ds(r0, nrows), :]
        cb = cs[pl.ds(r0, nrows), :]
        cb = cb[:, 0:LANES] + cb[:, LANES:2 * LANES]
        acc = None
        for k in range(EXP_PER_GROUP):
            e = EXP_PER_GROUP * i + k
            a = jax.nn.silu(jnp.dot(hb, wg_ref[e], preferred_element_type=F32)) \
                * jnp.dot(hb, wu_ref[e], preferred_element_type=F32)
            d = jnp.dot((a * cb[:, e:e + 1]).astype(BF16), wd_ref[e], preferred_element_type=F32)
            acc = d if acc is None else acc + d
        ys[pl.ds(r0, nrows), :] = acc.astype(BF16)

    for i in range(N_GROUPS):
        base = starts[i][0, 0].astype(jnp.int32)
        nblk = (padded[i:i + 1, :][0, 0] * (1.0 / MOE_BLOCK)).astype(jnp.int32)

        def pair(j, carry, i=i, base=base):
            experts(i, pl.multiple_of(base + j * (2 * MOE_BLOCK), MOE_BLOCK), 2 * MOE_BLOCK)
            return carry

        lax.fori_loop(0, nblk // 2, pair, 0)

        @pl.when(nblk % 2 == 1)
        def _(i=i, base=base, nblk=nblk):
            experts(i, pl.multiple_of(base + (nblk - 1) * MOE_BLOCK, MOE_BLOCK), MOE_BLOCK)

    dest_t = stack_t[:, N_EXPERTS:N_EXPERTS + 1]
    unperm = jnp.where(lax.broadcasted_iota(jnp.int32, (tm, rs), 1).astype(F32) == dest_t, 1.0, 0.0)
    o_ref[...] += jnp.dot(unperm.astype(BF16), ys[...], preferred_element_type=F32)


def _outproj_moe(ys, x2, g_out, w_out, p, experts, layer, tm=512):
    t = x2.shape[0]
    rs = tm + N_GROUPS * MOE_BLOCK
    yspec = pl.BlockSpec((tm, GROUP_W), lambda i: (i, 0))
    tri = jnp.asarray(np.triu(np.ones((tm, tm), np.float32)), BF16)

    def resident(shape):
        nd = len(shape)
        return pl.BlockSpec(shape, lambda i: (0,) * nd, pipeline_mode=pl.Buffered(1))

    def of_layer(shape):
        return pl.BlockSpec((None,) + shape, lambda i: (layer,) + (0,) * len(shape),
                            pipeline_mode=pl.Buffered(1))

    return pl.pallas_call(
        _moe_kernel,
        grid=(t // tm,),
        in_specs=[yspec, yspec, yspec, yspec, pl.BlockSpec((tm, D_MODEL), lambda i: (i, 0)),
                  _full((4, GROUP_W)), resident((D_MODEL, D_MODEL)),
                  _full((1, D_MODEL)), _full((D_MODEL, 2 * LANES)),
                  _full((1, LANES)), resident((tm, tm)),
                  of_layer((N_EXPERTS, D_MODEL, D_EXPERT)), of_layer((N_EXPERTS, D_MODEL, D_EXPERT)),
                  of_layer((N_EXPERTS, D_EXPERT, D_MODEL))],
        out_specs=pl.BlockSpec((tm, D_MODEL), lambda i: (i, 0)),
        out_shape=jax.ShapeDtypeStruct((t, D_MODEL), F32),
        scratch_shapes=[pltpu.VMEM((rs, D_MODEL), BF16), pltpu.VMEM((rs, 2 * LANES), F32),
                        pltpu.VMEM((rs, D_MODEL), BF16)],
        compiler_params=_cparams(("parallel",)),
        name="outproj_moe",
    )(*ys, x2, g_out, w_out, p["g"], p["wr"], p["br"], tri, *experts)


def _prep_moe(g, w_rg, b_rg, w_re, b_re):
    wr = _pad_last(jnp.concatenate([w_rg, w_re], axis=1), LANES)
    wr_hi = wr.astype(BF16)
    wr_lo = (wr - wr_hi.astype(F32)).astype(BF16)
    br = _pad_last(jnp.concatenate([b_rg, b_re]), LANES)[None, :]
    return dict(g=g[None, :], wr=jnp.concatenate([wr_hi, wr_lo], axis=1), br=br)


def kernel(x, positions, mix_norm, w_in, mla_g_cq, mla_g_ckv, mla_w_uq, mla_w_ukv, mla_g_q, mla_g_k, lru_conv_w, lru_conv_b, lru_w_a, lru_b_a, lru_w_i, lru_b_i, lru_lambda, s5_a_re, s5_a_im, s5_log_dt, s5_b_re, s5_b_im, s5_c_re, s5_c_im, s5_d, s5_w_glu, s5_b_glu, nsa_g_q, nsa_g_k, nsa_pe_k, nsa_w1_k, nsa_w2_k, nsa_pe_v, nsa_w1_v, nsa_w2_v, out_norm, w_out, ffn_norm, moe_w_rg, moe_b_rg, moe_w_re, moe_b_re, moe_w_gate, moe_w_up, moe_w_down):
    b, s, d = x.shape
    t = b * s
    trig = _trig(positions.astype(jnp.int32)[:, :, None])
    x2 = x.reshape(t, d)
    experts = (moe_w_gate.astype(BF16), moe_w_up.astype(BF16), moe_w_down.astype(BF16))
    for l in range(w_in.shape[0]):
        pn = _prep_nsa(nsa_g_q[l], nsa_g_k[l], nsa_pe_k[l], nsa_w1_k[l], nsa_w2_k[l], nsa_pe_v[l],
                       nsa_w1_v[l], nsa_w2_v[l])
        mq, mk, mv, o_lru, o_s5, nq, ks, kw, vs, vw, cg = _proj_prep(
            x2.reshape(b, s, d), mix_norm[l], _prep_w_in(w_in[l]), trig,
            _prep_mla(mla_g_cq[l], mla_g_ckv[l], mla_w_uq[l], mla_w_ukv[l], mla_g_q[l], mla_g_k[l]), pn)
        y_a = _mla(mq, mk, mv)
        y_b = _lru(o_lru, _prep_lru(lru_conv_w[l], lru_conv_b[l], lru_w_a[l], lru_b_a[l], lru_w_i[l],
                                    lru_b_i[l], lru_lambda[l]))
        y_c = _s5(o_s5.transpose(1, 0, 2),
                  _prep_s5(s5_a_re[l], s5_a_im[l], s5_log_dt[l], s5_b_re[l], s5_b_im[l], s5_c_re[l],
                           s5_c_im[l], s5_d[l], s5_w_glu[l], s5_b_glu[l])).transpose(1, 0, 2)
        y_d = _nsa(nq, ks, kw, vs, vw, cg, trig, pn)
        ys = [y.reshape(t, GROUP_W) for y in (y_a, y_b, y_c, y_d)]
        x2 = _outproj_moe(ys, x2, out_norm[l], w_out[l].astype(BF16),
                          _prep_moe(ffn_norm[l], moe_w_rg[l], moe_b_rg[l], moe_w_re[l], moe_b_re[l]),
                          experts, l)
    return x2.reshape(b, s, d)
```

```python
import functools
import math

import numpy as np
import jax
import jax.numpy as jnp
from jax import lax
from jax.experimental import pallas as pl
from jax.experimental.pallas import tpu as pltpu

F32 = jnp.float32
BF16 = jnp.bfloat16

D_MODEL = 1024
DEPTH = 2
GROUP_W = 256
EPS = 1e-6
ROPE_THETA = 500000.0
NEG = -1e30
LOG2E = math.log2(math.e)

MLA_HEADS = 4
MLA_ROPE = 32
MLA_NOPE = 64
MLA_V = 64
MLA_QK = 96
MLA_Q_RANK = 192
MLA_KV_RANK = 128

LRU_W = 256
LRU_BLOCKS = 4
LRU_BW = 64
CONV_W = 4
LRU_C = 8.0

S5_W = 256
S5_CH = 16
S5_GROUPS = 16
S5_P = 64
S5_STATE = S5_GROUPS * S5_P

NSA_HEADS = 4
NSA_DK = 64
NSA_ROT = 16
CMP_LEN = 32
CMP_STRIDE = 16
CMP_HID = 128
SEL_LEN = 64
SEL_TOPK = 5
WIN = 512

N_GROUPS = 4
EXP_PER_GROUP = 4
N_EXPERTS = 16
D_EXPERT = 256

D_IN = 1772

LANES = 128
VMEM_LIMIT = 48 * 1024 * 1024

MLA_IN_W = 512
LRU_IN_W = 512
S5_IN_W = 256
NQ_IN_W = NSA_HEADS * LANES
NKV_IN_W = 6 * LANES
IN_W = MLA_IN_W + LRU_IN_W + S5_IN_W + NQ_IN_W + NKV_IN_W


def _cparams(sem):
    return pltpu.CompilerParams(dimension_semantics=sem, vmem_limit_bytes=VMEM_LIMIT)


def _dot(a, b):
    return jnp.dot(a.astype(BF16), b.astype(BF16), preferred_element_type=F32)


def _dot_nt(a, b):
    return lax.dot_general(a.astype(BF16), b.astype(BF16), (((1,), (1,)), ((), ())),
                           preferred_element_type=F32)


def _split3(x):
    hi = x.astype(BF16)
    r = x - hi.astype(F32)
    mid = r.astype(BF16)
    lo = (r - mid.astype(F32)).astype(BF16)
    return hi, mid, lo


def _dot_f32_by_exact(x, w_bf16):
    hi, mid, lo = _split3(x)
    return (jnp.dot(hi, w_bf16, preferred_element_type=F32)
            + jnp.dot(mid, w_bf16, preferred_element_type=F32)
            + jnp.dot(lo, w_bf16, preferred_element_type=F32))


def _rms(x, g, n):
    return x * lax.rsqrt(jnp.sum(x * x, axis=-1, keepdims=True) * (1.0 / n) + EPS) * g


def _lane_sum(x):
    hi = x.astype(BF16)
    lo = (x - hi.astype(F32)).astype(BF16)
    ones = jnp.ones((x.shape[-1], LANES), BF16)
    return (jnp.dot(hi, ones, preferred_element_type=F32) + jnp.dot(lo, ones, preferred_element_type=F32))


def _rms_mxu(x, g, n):
    inv = lax.rsqrt(_lane_sum(x * x) * (1.0 / n) + EPS)
    if x.shape[-1] > LANES:
        inv = jnp.concatenate([inv] * (x.shape[-1] // LANES), axis=-1)
    return x * inv * g


def _gelu(x):
    return 0.5 * x * (1.0 + jnp.tanh(math.sqrt(2.0 / math.pi) * (x + 0.044715 * (x * x * x))))


def _rope(x, cos, sin_lo, sin_hi, half):
    return (x * cos + pltpu.roll(x, LANES - half, axis=1) * sin_lo
            + pltpu.roll(x, half, axis=1) * sin_hi)


def _with_ones(x, upper):
    if upper:
        x = pltpu.roll(x, LANES // 2, axis=1)
    lane = lax.broadcasted_iota(jnp.int32, x.shape, 1)
    return jnp.where(lane < LANES // 2, x, 1.0).astype(BF16)


NSA_TRIG_LANE0 = MLA_ROPE


def _trig_kernel(pos_ref, inv_ref, cos_ref, sin_ref):
    ang = pos_ref[0].astype(F32) * inv_ref[...]
    cos_ref[0] = jnp.cos(ang)
    sin_ref[0] = jnp.sin(ang)


def _trig(pos3, ts=512):
    def inv(rot):
        v = ROPE_THETA ** (-jnp.arange(rot // 2, dtype=F32) * 2.0 / rot)
        return jnp.concatenate([v, v])
    inv_l = _pad_last(jnp.concatenate([inv(MLA_ROPE), inv(NSA_ROT)]), LANES)[None, :]
    b, s, _ = pos3.shape
    spec = pl.BlockSpec((1, ts, LANES), lambda i, j: (i, j, 0))
    return pl.pallas_call(
        _trig_kernel,
        grid=(b, s // ts),
        in_specs=[pl.BlockSpec((1, ts, 1), lambda i, j: (i, j, 0)), _full((1, LANES))],
        out_specs=[spec, spec],
        out_shape=[jax.ShapeDtypeStruct((b, s, LANES), F32)] * 2,
        compiler_params=_cparams(("parallel", "parallel")),
        name="rope_trig",
    )(pos3, inv_l)


def _rope_tables(cs, sn, lane0, half):
    if lane0:
        cs = pltpu.roll(cs, LANES - lane0, axis=1)
        sn = pltpu.roll(sn, LANES - lane0, axis=1)
    lane = lax.broadcasted_iota(jnp.int32, cs.shape, 1)
    cos = jnp.where(lane < 2 * half, cs, 1.0)
    s_lo = jnp.where(lane < half, -sn, 0.0)
    s_hi = jnp.where((lane >= half) & (lane < 2 * half), sn, 0.0)
    return cos, s_lo, s_hi


def _pad_last(a, n):
    return jnp.pad(a, [(0, 0)] * (a.ndim - 1) + [(0, n - a.shape[-1])])


def _full(shape):
    nd = len(shape)
    return pl.BlockSpec(shape, lambda *_: (0,) * nd)


def _proj_prep_kernel(x_ref, g_ref, w_ref, cs_ref, sn_ref, gcq_ref, wuq_ref, gckv_ref, wk_ref, wv_ref,
                      gqm_ref, gkm_ref, gqn_ref, gkn_ref,
                      mq_out, mk_out, mv_out, lru_out, s5_out, nq_out, ks_out, kw_out, vs_out, vw_out,
                      cg_out):
    x = x_ref[0]
    h = x * lax.rsqrt(jnp.mean(x * x, axis=-1, keepdims=True) + EPS) * g_ref[...]
    y = jnp.dot(h.astype(BF16), w_ref[...], preferred_element_type=F32)
    cs, sn = cs_ref[0], sn_ref[0]
    c0 = 0
    _mla_prep(y[:, c0:c0 + MLA_IN_W], cs, sn, gcq_ref, wuq_ref, gckv_ref, wk_ref, wv_ref, gqm_ref, gkm_ref,
              mq_out, mk_out, mv_out)
    c0 += MLA_IN_W
    lru_out[0] = y[:, c0:c0 + LRU_IN_W]
    c0 += LRU_IN_W
    s5_out[0] = y[:, c0:c0 + S5_IN_W].astype(BF16)
    c0 += S5_IN_W
    nkv = y[:, c0 + NQ_IN_W:c0 + NQ_IN_W + NKV_IN_W]
    _nsa_prep(y[:, c0:c0 + NQ_IN_W], nkv, cs, sn, gqn_ref, gkn_ref, nq_out, ks_out, kw_out, vs_out, vw_out)
    for i, slot in enumerate((0, 3, 5)):
        cg_out[0, :, LANES * i:LANES * (i + 1)] = nkv[:, LANES * slot:LANES * (slot + 1)]


def _inproj_cols():
    src = -np.ones((IN_W,), np.int64)
    o = 0
    src[o:o + 192] = np.arange(0, 192)
    src[o + 256:o + 384] = np.arange(192, 320)
    src[o + 384:o + 416] = np.arange(320, 352)
    o += MLA_IN_W
    src[o:o + 512] = np.arange(352, 864)
    o += LRU_IN_W
    src[o:o + 256] = np.arange(864, 1120)
    o += S5_IN_W
    for h in range(NSA_HEADS):
        src[o + LANES * h:o + LANES * h + 64] = np.arange(1120 + 64 * h, 1120 + 64 * h + 64)
    o += NQ_IN_W
    kv0 = 1376
    src[o:o + 64] = np.arange(kv0, kv0 + 64)
    src[o + 128:o + 192] = np.arange(kv0 + 128, kv0 + 192)
    src[o + 256:o + 320] = np.arange(kv0 + 256, kv0 + 320)
    src[o + 384:o + 448] = np.arange(kv0 + 64, kv0 + 128)
    src[o + 448:o + 512] = np.arange(kv0 + 192, kv0 + 256)
    src[o + 512:o + 576] = np.arange(kv0 + 320, kv0 + 384)
    src[o + 640:o + 652] = np.arange(1760, 1772)
    return src


_INPROJ_SRC = _inproj_cols()


def _prep_w_in(w_in):
    idx = jnp.asarray(np.maximum(_INPROJ_SRC, 0), jnp.int32)
    keep = jnp.asarray(_INPROJ_SRC >= 0)
    return jnp.where(keep[None, :], jnp.take(w_in, idx, axis=1), 0.0).astype(BF16)


CG_W = 3 * LANES


def _proj_prep(x3, g, w_pad, trig, pm, pn, ts=512):
    b, s, _ = x3.shape
    heads = lambda: pl.BlockSpec((1, MLA_HEADS, ts, LANES), lambda i, j: (i, 0, j, 0))
    rows = lambda w: pl.BlockSpec((1, ts, w), lambda i, j: (i, j, 0))
    hshape = jax.ShapeDtypeStruct((b, MLA_HEADS, s, LANES), BF16)
    tok = lambda w, dt: jax.ShapeDtypeStruct((b, s, w), dt)
    return pl.pallas_call(
        _proj_prep_kernel,
        grid=(b, s // ts),
        in_specs=[rows(D_MODEL), _full((1, D_MODEL)), _full((D_MODEL, IN_W)), rows(LANES), rows(LANES),
                  _full((1, 256)), _full((256, 512)), _full((1, 128)), _full((128, 512)),
                  _full((128, 256)), _full((1, 128)), _full((1, 128)),
                  _full((1, LANES)), _full((3, LANES))],
        out_specs=[heads(), heads(), heads(), rows(LRU_IN_W), rows(S5_IN_W),
                   heads(), rows(LANES), rows(LANES), rows(LANES), rows(LANES), rows(CG_W)],
        out_shape=[hshape, hshape, hshape, tok(LRU_IN_W, F32), tok(S5_IN_W, BF16),
                   hshape, tok(LANES, BF16), tok(LANES, BF16), tok(LANES, BF16), tok(LANES, BF16),
                   tok(CG_W, F32)],
        compiler_params=_cparams(("parallel", "parallel")),
        name="proj_prep",
    )(x3, g[None, :], w_pad, trig[0], trig[1], pm["g_cq"], pm["w_uq"], pm["g_ckv"], pm["w_k"], pm["w_v"],
      pm["g_q"], pm["g_k"], pn["g_q"], pn["g_k"])


def _mla_prep(xin, cs, sn, gcq_ref, wuq_ref, gckv_ref, wk_ref, wv_ref, gq_ref, gk_ref, q_out, k_out, v_out):
    cq = _rms_mxu(xin[:, 0:256], gcq_ref[...], MLA_Q_RANK)
    ckv = _rms_mxu(xin[:, 256:384], gckv_ref[...], MLA_KV_RANK)
    kpe = xin[:, 384:512]
    q = _dot(cq, wuq_ref[...])
    kn = _dot(ckv, wk_ref[...])
    v = _dot(ckv, wv_ref[...])
    cos, s_lo, s_hi = _rope_tables(cs, sn, 0, MLA_ROPE // 2)
    scale = MLA_QK ** -0.5 * LOG2E
    for h in range(MLA_HEADS):
        qh = _rms_mxu(q[:, LANES * h:LANES * (h + 1)], gq_ref[...], MLA_QK)
        q_out[0, h] = (_rope(qh, cos, s_lo, s_hi, MLA_ROPE // 2) * scale).astype(BF16)
        kh = _rms_mxu(kn[:, LANES * h:LANES * (h + 1)] + kpe, gk_ref[...], MLA_QK)
        k_out[0, h] = _rope(kh, cos, s_lo, s_hi, MLA_ROPE // 2).astype(BF16)
        v_out[0, h] = _with_ones(v[:, LANES * (h // 2):LANES * (h // 2 + 1)], h % 2 == 1)


def _attend(q, segs, v1, s_scr, p_scr, m_scr, nh, chunk, nsplit=1):
    rows = q.shape[0]
    part = rows // nsplit
    per_head = rows // nh
    outs = []
    for r_lo in range(0, rows, part):
        qs = q[r_lo:r_lo + part]
        c0 = 0
        m = None
        for k, bias in segs:
            n = k.shape[0]
            sb = _dot_nt(qs, k)
            if bias is not None and part >= per_head:
                sb = (sb.reshape(part // per_head, per_head, n) + bias[None]).reshape(part, n)
            elif bias is not None:
                b_lo = r_lo % per_head
                sb = sb + bias[b_lo:b_lo + part]
            s_scr[r_lo:r_lo + part, c0:c0 + n] = sb
            mx = jnp.max(sb, axis=-1, keepdims=True)
            m = mx if m is None else jnp.maximum(m, mx)
            c0 += n
        m_scr[r_lo:r_lo + part, :] = jnp.broadcast_to(m, (part, LANES))
        for r0 in range(r_lo, r_lo + part, chunk):
            mb = m_scr[r0:r0 + chunk, :]
            for j0 in range(0, c0, LANES):
                p = jnp.exp2(s_scr[r0:r0 + chunk, j0:j0 + LANES] - mb)
                p_scr[r0:r0 + chunk, j0:j0 + LANES] = p.astype(BF16)
        o = jnp.dot(p_scr[r_lo:r_lo + part, 0:c0], v1, preferred_element_type=F32)
        outs.append((o * (1.0 / pltpu.roll(o, LANES // 2, axis=1)))[:, 0:LANES // 2])
    return outs[0] if nsplit == 1 else jnp.concatenate(outs, axis=0)


def _mla_attn_kernel(q_ref, k_ref, v_ref, o_ref, s_scr, p_scr, m_scr, o_scr, bias_scr, *, tq, chunk):
    qi = pl.program_id(1)
    row = lax.broadcasted_iota(jnp.int32, (tq, tq), 0)
    col = lax.broadcasted_iota(jnp.int32, (tq, tq), 1)
    bias_scr[...] = jnp.where(col <= row, 0.0, NEG)
    nset = s_scr.shape[0]
    for c in range(k_ref.shape[2] // tq):

        @pl.when(qi == c)
        def _(c=c):
            n = (c + 1) * tq

            def heads(i, carry):
                for j in range(nset):
                    h = i * nset + j
                    segs = [(k_ref[0, h, n - tq:n, :], bias_scr[...])]
                    if c > 0:
                        segs = [(k_ref[0, h, 0:n - tq, :], None)] + segs
                    o_scr[h] = _attend(q_ref[0, h], segs, v_ref[0, h, 0:n, :], s_scr.at[j], p_scr.at[j],
                                       m_scr.at[j], 1, chunk)
                return carry

            lax.fori_loop(0, MLA_HEADS // nset, heads, 0)

    for h in range(MLA_HEADS):
        o_ref[0, :, MLA_V * h:MLA_V * (h + 1)] = o_scr[h].astype(o_ref.dtype)


def _mla(q, k, v, tq=512, chunk=128, nset=2):
    b, _, s, _ = q.shape
    return pl.pallas_call(
        functools.partial(_mla_attn_kernel, tq=tq, chunk=chunk),
        grid=(b, s // tq),
        in_specs=[pl.BlockSpec((1, MLA_HEADS, tq, LANES), lambda i, j: (i, 0, j, 0)),
                  pl.BlockSpec((1, MLA_HEADS, s, LANES), lambda i, j: (i, 0, 0, 0)),
                  pl.BlockSpec((1, MLA_HEADS, s, LANES), lambda i, j: (i, 0, 0, 0))],
        out_specs=pl.BlockSpec((1, tq, GROUP_W), lambda i, j: (i, j, 0)),
        out_shape=jax.ShapeDtypeStruct((b, s, GROUP_W), BF16),
        scratch_shapes=[pltpu.VMEM((nset, tq, s), F32), pltpu.VMEM((nset, tq, s), BF16),
                        pltpu.VMEM((nset, tq, LANES), F32),
                        pltpu.VMEM((MLA_HEADS, tq, MLA_V), F32), pltpu.VMEM((tq, tq), F32)],
        compiler_params=_cparams(("parallel", "arbitrary")),
        name="mla_attn",
    )(q, k, v)


def _prep_mla(g_cq, g_ckv, w_uq, w_ukv, g_q, g_k):
    wq = w_uq.reshape(MLA_Q_RANK, MLA_HEADS, MLA_QK)
    wq = _pad_last(wq, LANES).reshape(MLA_Q_RANK, MLA_HEADS * LANES)
    wq = jnp.pad(wq, ((0, 256 - MLA_Q_RANK), (0, 0)))
    wkv = w_ukv.reshape(MLA_KV_RANK, MLA_HEADS, MLA_NOPE + MLA_V)
    wk = jnp.pad(wkv[:, :, :MLA_NOPE], ((0, 0), (0, 0), (MLA_ROPE, LANES - MLA_QK)))
    wk = wk.reshape(MLA_KV_RANK, MLA_HEADS * LANES)
    wv = wkv[:, :, MLA_NOPE:].reshape(MLA_KV_RANK, MLA_HEADS * MLA_V)
    return dict(g_cq=_pad_last(g_cq, 256)[None, :], g_ckv=g_ckv[None, :], w_uq=wq.astype(BF16),
                w_k=wk.astype(BF16), w_v=wv.astype(BF16), g_q=_pad_last(g_q, LANES)[None, :],
                g_k=_pad_last(g_k, LANES)[None, :])


def _shift_rows(x, k, row, fill):
    return jnp.where(row >= k, pltpu.roll(x, k, axis=0), fill)


def _lru_kernel(in_ref, cw_ref, cb_ref, wa_ref, ba_ref, wi_ref, bi_ref, lam_ref, o_ref, a_scr, b_scr, *,
                chunk):
    xin = in_ref[0]
    s = xin.shape[0]
    xb = xin[:, :LRU_W]
    row = lax.broadcasted_iota(jnp.int32, (s, LRU_W), 0)
    u = cb_ref[...] + xb * cw_ref[CONV_W - 1:CONV_W, :]
    for j in range(CONV_W - 1):
        u = u + _shift_rows(xb, CONV_W - 1 - j, row, 0.0) * cw_ref[j:j + 1, :]
    r = jax.nn.sigmoid(_dot(u, wa_ref[...]) + ba_ref[...])
    gi = jax.nn.sigmoid(_dot(u, wi_ref[...]) + bi_ref[...])
    nlam = -lam_ref[...]
    softplus = jnp.maximum(nlam, 0.0) + jnp.log1p(jnp.exp(-jnp.abs(nlam)))
    log_a = (-LRU_C) * r * softplus
    a = jnp.exp(log_a)
    y = jnp.maximum(-jnp.tanh(log_a) * (a * a + 1.0), 0.0)
    mult = jnp.where(y > 0.0, y * lax.rsqrt(y), 0.0)
    mult = jnp.where(row == 0, 1.0, mult)
    bt = mult * gi * u
    rin = jnp.bitwise_and(row, chunk - 1)
    k = 1
    while k < chunk:
        bt = a * _shift_rows(bt, k, rin, 0.0) + bt
        a = a * _shift_rows(a, k, rin, 1.0)
        k *= 2
    a_scr[...] = a
    b_scr[...] = bt

    def carry_chunk(j, h_prev):
        r0 = pl.multiple_of(j * chunk, chunk)
        h = b_scr[pl.ds(r0, chunk), :] + a_scr[pl.ds(r0, chunk), :] * h_prev
        o_ref[0, pl.ds(r0, chunk), :] = (h * _gelu(in_ref[0, pl.ds(r0, chunk), LRU_W:])).astype(o_ref.dtype)
        return h[chunk - 1:chunk, :]

    lax.fori_loop(0, s // chunk, carry_chunk, jnp.zeros((1, LRU_W), F32))


def _block_diag(w):
    n, i, j = w.shape
    eye = jnp.eye(n, dtype=w.dtype)
    return (eye[:, None, :, None] * w[:, :, None, :]).reshape(n * i, n * j)


def _lru(lru_in, p, chunk=64):
    b, s, _ = lru_in.shape
    return pl.pallas_call(
        functools.partial(_lru_kernel, chunk=chunk),
        grid=(b,),
        in_specs=[pl.BlockSpec((1, s, LRU_IN_W), lambda i: (i, 0, 0)),
                  _full((CONV_W, LRU_W)), _full((1, LRU_W)), _full((LRU_W, LRU_W)), _full((1, LRU_W)),
                  _full((LRU_W, LRU_W)), _full((1, LRU_W)), _full((1, LRU_W))],
        out_specs=pl.BlockSpec((1, s, LRU_W), lambda i: (i, 0, 0)),
        out_shape=jax.ShapeDtypeStruct((b, s, LRU_W), BF16),
        scratch_shapes=[pltpu.VMEM((s, LRU_W), F32), pltpu.VMEM((s, LRU_W), F32)],
        compiler_params=_cparams(("parallel",)),
        name="rglru",
    )(lru_in, p["cw"], p["cb"], p["wa"], p["ba"], p["wi"], p["bi"], p["lam"])


def _prep_lru(conv_w, conv_b, w_a, b_a, w_i, b_i, lam):
    return dict(cw=conv_w, cb=conv_b[None, :], wa=_block_diag(w_a).astype(BF16),
                ba=b_a.reshape(1, LRU_W), wi=_block_diag(w_i).astype(BF16), bi=b_i.reshape(1, LRU_W),
                lam=lam[None, :])


def _s5_disc_kernel(are_ref, aim_ref, ldt_ref, arer_ref, aimr_ref, bre_ref, bim_ref,
                    abre_ref, abim_ref, bbre_ref, bbim_ref):
    dt = jnp.exp(ldt_ref[...])

    def disc(a_re, a_im):
        mag = jnp.exp(dt * a_re)
        ab_re = mag * jnp.cos(dt * a_im)
        ab_im = mag * jnp.sin(dt * a_im)
        den = a_re * a_re + a_im * a_im
        n_re = ab_re - 1.0
        g_re = (n_re * a_re + ab_im * a_im) / den
        g_im = (ab_im * a_re - n_re * a_im) / den
        return ab_re, ab_im, g_re, g_im

    ab_re, ab_im, _, _ = disc(are_ref[...], aim_ref[...])
    abre_ref[...] = ab_re
    abim_ref[...] = ab_im
    _, _, g_re, g_im = disc(arer_ref[...], aimr_ref[...])
    bbre_ref[...] = g_re * bre_ref[...] - g_im * bim_ref[...]
    bbim_ref[...] = g_re * bim_ref[...] + g_im * bre_ref[...]


def _s5_kernel(u_ref, are_ref, aim_ref, bre_ref, bim_ref, cre_ref, cim_ref, d_ref, wg_ref, bg_ref,
               o_ref, hre, him, st_re, st_im, *, tc, nb, cw):
    @pl.when(pl.program_id(0) == 0)
    def _():
        st_re[...] = jnp.zeros_like(st_re)
        st_im[...] = jnp.zeros_like(st_im)

    ub = u_ref[...].reshape(tc * nb, S5_W)
    u = ub.astype(F32)
    hre[...] = jnp.dot(ub, bre_ref[...], preferred_element_type=F32)
    him[...] = jnp.dot(ub, bim_ref[...], preferred_element_type=F32)
    for c in range(S5_STATE // cw):
        cs = slice(c * cw, (c + 1) * cw)
        ar = jnp.broadcast_to(are_ref[:, cs], (nb, cw))
        ai = jnp.broadcast_to(aim_ref[:, cs], (nb, cw))

        def body(t, carry, cs=cs, ar=ar, ai=ai):
            hr, hi = carry
            r0 = pl.multiple_of(t * nb, nb)
            nr = ar * hr - ai * hi + hre[pl.ds(r0, nb), cs]
            ni = ar * hi + ai * hr + him[pl.ds(r0, nb), cs]
            hre[pl.ds(r0, nb), cs] = nr
            him[pl.ds(r0, nb), cs] = ni
            return nr, ni

        hr, hi = lax.fori_loop(0, tc, body, (st_re[:, cs], st_im[:, cs]), unroll=4)
        st_re[:, cs] = hr
        st_im[:, cs] = hi
    y = (jnp.dot(hre[...].astype(BF16), cre_ref[...], preferred_element_type=F32)
         - jnp.dot(him[...].astype(BF16), cim_ref[...], preferred_element_type=F32))
    y = _gelu(y + d_ref[...] * u)
    z = _dot(y, wg_ref[...]) + bg_ref[...]
    o_ref[...] = (y * jax.nn.sigmoid(z)).astype(o_ref.dtype).reshape(tc, nb, S5_W)


def _prep_s5(a_re, a_im, log_dt, b_re, b_im, c_re, c_im, d, w_glu, b_glu):
    g, p, ch = S5_GROUPS, S5_P, S5_CH
    ab_re, ab_im, bb_re, bb_im = pl.pallas_call(
        _s5_disc_kernel,
        out_shape=[jax.ShapeDtypeStruct((g, p), F32), jax.ShapeDtypeStruct((g, p), F32),
                   jax.ShapeDtypeStruct((g, p * ch), F32), jax.ShapeDtypeStruct((g, p * ch), F32)],
        name="s5_discretize",
    )(a_re, a_im, log_dt[:, None], jnp.repeat(a_re, ch, axis=1), jnp.repeat(a_im, ch, axis=1),
      b_re.reshape(g, p * ch), b_im.reshape(g, p * ch))

    def b_dense(bb):
        return _block_diag(bb.reshape(g, p, ch).transpose(0, 2, 1)).astype(BF16)

    def c_dense(c):
        return _block_diag(c.transpose(0, 2, 1)).astype(BF16)

    return dict(a_re=ab_re.reshape(1, S5_STATE), a_im=ab_im.reshape(1, S5_STATE),
                b_re=b_dense(bb_re), b_im=b_dense(bb_im), c_re=c_dense(c_re), c_im=c_dense(c_im),
                d=d[None, :], w_glu=w_glu.astype(BF16), b_glu=b_glu[None, :])


def _s5(u_tm, p, tc=64, cw=256):
    s, nb, _ = u_tm.shape
    return pl.pallas_call(
        functools.partial(_s5_kernel, tc=tc, nb=nb, cw=cw),
        grid=(s // tc,),
        in_specs=[pl.BlockSpec((tc, nb, S5_W), lambda i: (i, 0, 0)),
                  _full((1, S5_STATE)), _full((1, S5_STATE)),
                  _full((S5_W, S5_STATE)), _full((S5_W, S5_STATE)),
                  _full((S5_STATE, S5_W)), _full((S5_STATE, S5_W)),
                  _full((1, S5_W)), _full((S5_W, S5_W)), _full((1, S5_W))],
        out_specs=pl.BlockSpec((tc, nb, S5_W), lambda i: (i, 0, 0)),
        out_shape=jax.ShapeDtypeStruct((s, nb, S5_W), BF16),
        scratch_shapes=[pltpu.VMEM((tc * nb, S5_STATE), F32), pltpu.VMEM((tc * nb, S5_STATE), F32),
                        pltpu.VMEM((nb, S5_STATE), F32), pltpu.VMEM((nb, S5_STATE), F32)],
        compiler_params=_cparams(("arbitrary",)),
        name="s5",
    )(u_tm, p["a_re"], p["a_im"], p["b_re"], p["b_im"], p["c_re"], p["c_im"], p["d"], p["w_glu"],
      p["b_glu"])


def _nsa_prep(nq, nkv, cs, sn, gq_ref, gk_ref, q_out, ks_out, kw_out, vs_out, vw_out):
    half = NSA_ROT // 2
    cos, s_lo, s_hi = _rope_tables(cs, sn, NSA_TRIG_LANE0, half)
    scale = NSA_DK ** -0.5 * LOG2E
    for h in range(NSA_HEADS):
        qh = _rms_mxu(nq[:, LANES * h:LANES * (h + 1)], gq_ref[...], NSA_DK)
        q_out[0, h] = (_rope(qh, cos, s_lo, s_hi, half) * scale).astype(BF16)
    ks = _rms_mxu(nkv[:, 128:256], gk_ref[1:2, :], NSA_DK)
    ks_out[0] = _rope(ks, cos, s_lo, s_hi, half).astype(BF16)
    kw = _rms_mxu(nkv[:, 256:384], gk_ref[2:3, :], NSA_DK)
    kw_out[0] = _rope(kw, cos, s_lo, s_hi, half).astype(BF16)
    vs_out[0] = _with_ones(nkv[:, 384:512], True)
    vw_out[0] = _with_ones(nkv[:, 512:640], False)


def _nsa_cmp_kernel(k_in, v_in, cs_ref, sn_ref, pek_ref, w1k_ref, w2k_ref, pev_ref, w1v_ref, w2v_ref,
                    g_ref, kc_out, vc_out):
    nc = kc_out.shape[1]

    def compress(x_ref, pe_ref, w1_ref, w2_ref):
        lo = hi = None
        for j in range(CMP_STRIDE):
            xj = x_ref[0, pl.ds(j, nc, stride=CMP_STRIDE), :]
            dl = _dot(xj + pe_ref[j:j + 1, :], w1_ref[j])
            dh = _dot(xj + pe_ref[CMP_STRIDE + j:CMP_STRIDE + j + 1, :], w1_ref[CMP_STRIDE + j])
            lo = dl if lo is None else lo + dl
            hi = dh if hi is None else hi + dh
        hid = lo + pltpu.roll(hi, nc - 1, axis=0)
        return _dot(_gelu(hid), w2_ref[...])

    kc = compress(k_in, pek_ref, w1k_ref, w2k_ref)
    cos, s_lo, s_hi = _rope_tables(cs_ref[0], sn_ref[0], NSA_TRIG_LANE0, NSA_ROT // 2)
    kc = _rope(_rms(kc, g_ref[0:1, :], NSA_DK), cos, s_lo, s_hi, NSA_ROT // 2)
    kc_out[0] = kc.astype(BF16)
    vc_out[0] = compress(v_in, pev_ref, w1v_ref, w2v_ref).astype(BF16)


def _nsa_cmpsel_kernel(q_ref, kc_ref, vc_ref, ov_ref, ocmp_ref, sel_ref, *, tc):
    qi = pl.program_id(1)
    nh = NSA_HEADS
    q = q_ref[0].reshape(nh * tc, LANES)
    lane = lax.broadcasted_iota(jnp.int32, (tc, LANES), 1)
    qpos = qi * tc + lax.broadcasted_iota(jnp.int32, (tc, LANES), 0)

    s = _dot_nt(q, kc_ref[0]).reshape(nh, tc, LANES)
    valid = (lane * CMP_STRIDE + (CMP_LEN - 1)) <= qpos
    s = jnp.where(valid, s, NEG)
    e = jnp.exp2(s - jnp.max(s, axis=-1, keepdims=True))
    p_c = jnp.where(valid, e / jnp.sum(e, axis=-1, keepdims=True), 0.0)
    ocmp_ref[0] = jnp.dot(p_c.astype(BF16).reshape(nh * tc, LANES), vc_ref[0],
                          preferred_element_type=F32).reshape(nh, tc, NSA_DK)

    imp = _dot_f32_by_exact(jnp.sum(p_c, axis=0), ov_ref[...])
    cur = qpos // SEL_LEN
    lane_f = lane.astype(F32)
    sel = (lane == 0) | (lane == cur) | (lane == cur - 1)
    cand = (lane < cur - 1) & (lane > 0)
    for _ in range(SEL_TOPK - 3):
        sc = jnp.where(cand, imp, -jnp.inf)
        best = jnp.max(sc, axis=-1, keepdims=True)
        pick = jnp.min(jnp.where(cand & (sc == best), lane_f, float(LANES)), axis=-1, keepdims=True)
        hit = lane_f == pick
        sel = sel | hit
        cand = cand & jnp.logical_not(hit)
    sel_ref[0] = jnp.where(sel, 1.0, 0.0).astype(BF16)


def _nsa_attn_kernel(q_ref, sel_ref, ocmp_ref, ks_ref, vs_ref, kw_ref, vw_ref, gate_ref, ex_ref,
                     o_ref, s_scr, p_scr, m_scr, obr_scr, *, tq, chunk, kchunk):
    qi = pl.program_id(1)
    nh = NSA_HEADS
    rows = nh * tq
    s_len = ks_ref.shape[1]

    def window():
        wk = WIN + tq
        start = pl.multiple_of(jnp.maximum(qi - WIN // tq, 0) * tq, tq)
        kpos = start + lax.broadcasted_iota(jnp.int32, (tq, wk), 1)
        qrow = qi * tq + lax.broadcasted_iota(jnp.int32, (tq, wk), 0)
        bias = jnp.where((kpos <= qrow) & (qrow - kpos < WIN), 0.0, NEG)
        obr_scr[1] = _attend(q_ref[0].reshape(rows, LANES), [(kw_ref[0, pl.ds(start, wk), :], bias)],
                             vw_ref[0, pl.ds(start, wk), :], s_scr.at[1], p_scr.at[1], m_scr.at[1],
                             nh, chunk, nsplit=1)

    for c in range(s_len // kchunk):

        @pl.when(qi // (kchunk // tq) == c)
        def _(c=c):
            n = (c + 1) * kchunk
            em = jnp.dot(sel_ref[0], ex_ref[:, 0:n], preferred_element_type=F32)
            kpos = lax.broadcasted_iota(jnp.int32, (tq, n), 1)
            qrow = qi * tq + lax.broadcasted_iota(jnp.int32, (tq, n), 0)
            bias = jnp.where((em > 0.5) & (kpos <= qrow), 0.0, NEG)
            obr_scr[0] = _attend(q_ref[0].reshape(rows, LANES), [(ks_ref[0, 0:n, :], bias)],
                                 vs_ref[0, 0:n, :], s_scr.at[0], p_scr.at[0], m_scr.at[0], nh, chunk, nsplit=1)
            window()

    g = jax.nn.sigmoid(gate_ref[0])
    for h in range(nh):
        r = slice(h * tq, (h + 1) * tq)
        o = (g[:, 3 * h:3 * h + 1] * ocmp_ref[0, h] + g[:, 3 * h + 1:3 * h + 2] * obr_scr[0, r, :]
             + g[:, 3 * h + 2:3 * h + 3] * obr_scr[1, r, :])
        o_ref[0, :, NSA_DK * h:NSA_DK * (h + 1)] = o.astype(o_ref.dtype)


def _nsa_tables(s, tq):
    nc = s // CMP_STRIDE
    nsb = s // SEL_LEN
    cs = np.arange(nc) * CMP_STRIDE
    ss = np.arange(nsb) * SEL_LEN
    ov = np.clip(np.minimum(cs[:, None] + CMP_LEN, ss[None, :] + SEL_LEN)
                 - np.maximum(cs[:, None], ss[None, :]), 0, None) / CMP_STRIDE
    ov[(s - CMP_LEN) // CMP_STRIDE + 1:] = 0.0
    ov_pad = np.zeros((nc, LANES), np.float32)
    ov_pad[:, :nsb] = ov
    ex = np.zeros((LANES, s), np.float32)
    ex[np.arange(s) // SEL_LEN, np.arange(s)] = 1.0
    return jnp.asarray(ov_pad, BF16), jnp.asarray(ex, BF16)


def _nsa(q, ks, kw, vs, vw, cg, trig, p, tq=256, tc=512, chunk=128, kchunk=512):
    b, s, _ = ks.shape
    nc = s // CMP_STRIDE
    assert nc == LANES and s // SEL_LEN <= LANES and s >= WIN + tq and s % kchunk == 0

    last = np.minimum(np.arange(nc) * CMP_STRIDE + CMP_LEN - 1, s - 1)
    cspec = pl.BlockSpec((1, nc, LANES), lambda i: (i, 0, 0))
    kc, vc = pl.pallas_call(
        _nsa_cmp_kernel,
        grid=(b,),
        in_specs=[pl.BlockSpec((1, s, LANES), lambda i: (i, 0, 0)),
                  pl.BlockSpec((1, s, LANES), lambda i: (i, 0, 1)),
                  cspec, cspec,
                  _full((CMP_LEN, LANES)), _full((CMP_LEN, LANES, CMP_HID)), _full((CMP_HID, LANES)),
                  _full((CMP_LEN, LANES)), _full((CMP_LEN, LANES, CMP_HID)), _full((CMP_HID, NSA_DK)),
                  _full((3, LANES))],
        out_specs=[cspec, pl.BlockSpec((1, nc, NSA_DK), lambda i: (i, 0, 0))],
        out_shape=[jax.ShapeDtypeStruct((b, nc, LANES), BF16),
                   jax.ShapeDtypeStruct((b, nc, NSA_DK), BF16)],
        compiler_params=_cparams(("parallel",)),
        name="nsa_compress",
    )(cg, cg, trig[0][:, last, :], trig[1][:, last, :], p["pe_k"], p["w1_k"], p["w2_k"], p["pe_v"],
      p["w1_v"], p["w2_v"], p["g_k"])

    ov, ex = _nsa_tables(s, tq)
    o_cmp, sel = pl.pallas_call(
        functools.partial(_nsa_cmpsel_kernel, tc=tc),
        grid=(b, s // tc),
        in_specs=[pl.BlockSpec((1, NSA_HEADS, tc, LANES), lambda i, j: (i, 0, j, 0)),
                  pl.BlockSpec((1, nc, LANES), lambda i, j: (i, 0, 0)),
                  pl.BlockSpec((1, nc, NSA_DK), lambda i, j: (i, 0, 0)),
                  _full((nc, LANES))],
        out_specs=[pl.BlockSpec((1, NSA_HEADS, tc, NSA_DK), lambda i, j: (i, 0, j, 0)),
                   pl.BlockSpec((1, tc, LANES), lambda i, j: (i, j, 0))],
        out_shape=[jax.ShapeDtypeStruct((b, NSA_HEADS, s, NSA_DK), F32),
                   jax.ShapeDtypeStruct((b, s, LANES), BF16)],
        compiler_params=_cparams(("parallel", "parallel")),
        name="nsa_cmpsel",
    )(q, kc, vc, ov)

    rows = NSA_HEADS * tq
    kvspec = pl.BlockSpec((1, s, LANES), lambda i, j: (i, 0, 0))
    return pl.pallas_call(
        functools.partial(_nsa_attn_kernel, tq=tq, chunk=chunk, kchunk=kchunk),
        grid=(b, s // tq),
        in_specs=[pl.BlockSpec((1, NSA_HEADS, tq, LANES), lambda i, j: (i, 0, j, 0)),
                  pl.BlockSpec((1, tq, LANES), lambda i, j: (i, j, 0)),
                  pl.BlockSpec((1, NSA_HEADS, tq, NSA_DK), lambda i, j: (i, 0, j, 0)),
                  kvspec, kvspec, kvspec, kvspec,
                  pl.BlockSpec((1, tq, LANES), lambda i, j: (i, j, CG_W // LANES - 1)),
                  _full((LANES, s))],
        out_specs=pl.BlockSpec((1, tq, GROUP_W), lambda i, j: (i, j, 0)),
        out_shape=jax.ShapeDtypeStruct((b, s, GROUP_W), BF16),
        scratch_shapes=[pltpu.VMEM((2, rows, s), F32), pltpu.VMEM((2, rows, s), BF16),
                        pltpu.VMEM((2, rows, LANES), F32), pltpu.VMEM((2, rows, NSA_DK), F32)],
        compiler_params=_cparams(("parallel", "arbitrary")),
        name="nsa_attn",
    )(q, sel, o_cmp, ks, vs, kw, vw, cg, ex)


def _prep_nsa(g_q, g_k, pe_k, w1_k, w2_k, pe_v, w1_v, w2_v):
    def per_token(w1):
        w = w1.reshape(CMP_LEN, NSA_DK, CMP_HID)
        return jnp.pad(w, ((0, 0), (0, LANES - NSA_DK), (0, 0))).astype(BF16)

    return dict(g_q=_pad_last(g_q, LANES)[None, :], g_k=_pad_last(g_k, LANES),
                pe_k=_pad_last(pe_k, LANES), w1_k=per_token(w1_k),
                w2_k=_pad_last(w2_k, LANES).astype(BF16),
                pe_v=_pad_last(pe_v, LANES), w1_v=per_token(w1_v), w2_v=w2_v.astype(BF16))


MOE_BLOCK = 144
MOE_ALIGN = 16


def _route_t(lt):
    row = lambda r: lt[r:r + 1, :]
    lg = [row(i) for i in range(N_GROUPS)]
    gmax = functools.reduce(jnp.maximum, lg)
    pg_top = 1.0 / functools.reduce(lambda a, b: a + b, [jnp.exp(v - gmax) for v in lg])
    taken = jnp.zeros_like(gmax) > 1.0
    oh = []
    for v in lg:
        hit = (v == gmax) & jnp.logical_not(taken)
        oh.append(hit)
        taken = taken | hit
    le = []
    for k in range(EXP_PER_GROUP):
        v = row(N_GROUPS + k)
        for i in range(1, N_GROUPS):
            v = jnp.where(oh[i], row(N_GROUPS + EXP_PER_GROUP * i + k), v)
        le.append(v)
    m1 = functools.reduce(jnp.maximum, le)
    taken = jnp.zeros_like(m1) > 1.0
    first = []
    for v in le:
        hit = (v == m1) & jnp.logical_not(taken)
        first.append(hit)
        taken = taken | hit
    le2 = [jnp.where(f, -jnp.inf, v) for f, v in zip(first, le)]
    m2 = functools.reduce(jnp.maximum, le2)
    taken = jnp.zeros_like(m1) > 1.0
    second = []
    for v in le2:
        hit = (v == m2) & jnp.logical_not(taken)
        second.append(hit)
        taken = taken | hit
    v2 = jnp.exp(m2 - m1)
    w1 = pg_top / (1.0 + v2)
    w2 = pg_top * v2 / (1.0 + v2)
    comb = []
    for i in range(N_GROUPS):
        for k in range(EXP_PER_GROUP):
            w = jnp.where(first[k], w1, 0.0) + jnp.where(second[k], w2, 0.0)
            comb.append(jnp.where(oh[i], w, 0.0))
    return [jnp.where(o, 1.0, 0.0) for o in oh], comb


def _moe_kernel(ya_ref, yb_ref, yc_ref, yd_ref, x_ref, gout_ref, wout_ref, g_ref, wr_ref,
                br_ref, tri_ref, wg_ref, wu_ref, wd_ref, o_ref, hs, cs, ys):
    tm = x_ref.shape[0]
    rs = hs.shape[0]
    x = x_ref[...]
    for i, y_ref in enumerate((ya_ref, yb_ref, yc_ref, yd_ref)):
        y = _rms(y_ref[...].astype(F32), gout_ref[i:i + 1, :], GROUP_W)
        x = x + _dot(y, wout_ref[GROUP_W * i:GROUP_W * (i + 1), :])
    o_ref[...] = x
    h = x * lax.rsqrt(jnp.mean(x * x, axis=-1, keepdims=True) + EPS) * g_ref[...]
    h_hi = h.astype(BF16)
    h_lo = (h - h_hi.astype(F32)).astype(BF16)
    lhl = jnp.dot(h_hi, wr_ref[...], preferred_element_type=F32)
    logits = (lhl[:, 0:LANES] + lhl[:, LANES:2 * LANES]
              + jnp.dot(h_lo, wr_ref[:, 0:LANES], preferred_element_type=F32)) + br_ref[...]
    oh, comb = _route_t(logits.T)

    oh8 = jnp.concatenate(oh + [jnp.zeros((8 - N_GROUPS, tm), F32)], axis=0)
    cum = jnp.dot(oh8.astype(BF16), tri_ref[...], preferred_element_type=F32)
    cnt = jnp.sum(oh8, axis=1, keepdims=True)
    padded = jnp.floor((cnt + (MOE_BLOCK - 1)) * (1.0 / MOE_BLOCK)) * MOE_BLOCK
    starts, acc0 = [], jnp.zeros((1, 1), F32)
    for i in range(N_GROUPS):
        starts.append(acc0)
        acc0 = acc0 + padded[i:i + 1, :]
    dest = functools.reduce(lambda a, b: a + b,
                            [oh[i] * (starts[i] + cum[i:i + 1, :] - 1.0) for i in range(N_GROUPS)])
    perm = jnp.where(lax.broadcasted_iota(jnp.int32, (rs, tm), 0).astype(F32) == dest, 1.0, 0.0)
    perm = perm.astype(BF16)

    stack = jnp.concatenate(comb + [dest] + [jnp.zeros((LANES - N_EXPERTS - 1, tm), F32)], axis=0)
    stack_t = stack.T
    c_hi = stack_t.astype(BF16)
    c_lo = (stack_t - c_hi.astype(F32)).astype(BF16)
    hs[...] = jnp.dot(perm, h_hi, preferred_element_type=F32).astype(BF16)
    cs[...] = jnp.dot(perm, jnp.concatenate([c_hi, c_lo], axis=1), preferred_element_type=F32)
    ys[...] = jnp.zeros_like(ys)

    def experts(i, r0, nrows):
        hb = hs[pl.ds(r0, nrows), :]
        cb = cs[pl.ds(r0, nrows), :]
        cb = cb[:, 0:LANES] + cb[:, LANES:2 * LANES]
        acc = None
        for k in range(EXP_PER_GROUP):
            e = EXP_PER_GROUP * i + k
            a = jax.nn.silu(jnp.dot(hb, wg_ref[e], preferred_element_type=F32)) \
                * jnp.dot(hb, wu_ref[e], preferred_element_type=F32)
            d = jnp.dot((a * cb[:, e:e + 1]).astype(BF16), wd_ref[e], preferred_element_type=F32)
            acc = d if acc is None else acc + d
        ys[pl.ds(r0, nrows), :] = acc.astype(BF16)

    for i in range(N_GROUPS):
        base = starts[i][0, 0].astype(jnp.int32)
        nblk = (padded[i:i + 1, :][0, 0] * (1.0 / MOE_BLOCK)).astype(jnp.int32)

        def pair(j, carry, i=i, base=base):
            experts(i, pl.multiple_of(base + j * (2 * MOE_BLOCK), MOE_ALIGN), 2 * MOE_BLOCK)
            return carry

        lax.fori_loop(0, nblk // 2, pair, 0)

        @pl.when(nblk % 2 == 1)
        def _(i=i, base=base, nblk=nblk):
            experts(i, pl.multiple_of(base + (nblk - 1) * MOE_BLOCK, MOE_ALIGN), MOE_BLOCK)

    dest_t = stack_t[:, N_EXPERTS:N_EXPERTS + 1]
    unperm = jnp.where(lax.broadcasted_iota(jnp.int32, (tm, rs), 1).astype(F32) == dest_t, 1.0, 0.0)
    o_ref[...] += jnp.dot(unperm.astype(BF16), ys[...], preferred_element_type=F32)


def _sorted_rows(tm):
    low = -tm % MOE_BLOCK
    pad = low + MOE_BLOCK * ((N_GROUPS * (MOE_BLOCK - 1) - low) // MOE_BLOCK)
    return -(-(tm + pad) // LANES) * LANES


def _outproj_moe(ys, x2, g_out, w_out, p, experts, layer, tm=512):
    t = x2.shape[0]
    rs = _sorted_rows(tm)
    yspec = pl.BlockSpec((tm, GROUP_W), lambda i: (i, 0))
    tri = jnp.asarray(np.triu(np.ones((tm, tm), np.float32)), BF16)

    def resident(shape):
        nd = len(shape)
        return pl.BlockSpec(shape, lambda i: (0,) * nd, pipeline_mode=pl.Buffered(1))

    def of_layer(shape):
        return pl.BlockSpec((None,) + shape, lambda i: (layer,) + (0,) * len(shape),
                            pipeline_mode=pl.Buffered(1))

    return pl.pallas_call(
        _moe_kernel,
        grid=(t // tm,),
        in_specs=[yspec, yspec, yspec, yspec, pl.BlockSpec((tm, D_MODEL), lambda i: (i, 0)),
                  _full((4, GROUP_W)), resident((D_MODEL, D_MODEL)),
                  _full((1, D_MODEL)), _full((D_MODEL, 2 * LANES)),
                  _full((1, LANES)), resident((tm, tm)),
                  of_layer((N_EXPERTS, D_MODEL, D_EXPERT)), of_layer((N_EXPERTS, D_MODEL, D_EXPERT)),
                  of_layer((N_EXPERTS, D_EXPERT, D_MODEL))],
        out_specs=pl.BlockSpec((tm, D_MODEL), lambda i: (i, 0)),
        out_shape=jax.ShapeDtypeStruct((t, D_MODEL), F32),
        scratch_shapes=[pltpu.VMEM((rs, D_MODEL), BF16), pltpu.VMEM((rs, 2 * LANES), F32),
                        pltpu.VMEM((rs, D_MODEL), BF16)],
        compiler_params=_cparams(("parallel",)),
        name="outproj_moe",
    )(*ys, x2, g_out, w_out, p["g"], p["wr"], p["br"], tri, *experts)


def _prep_moe(g, w_rg, b_rg, w_re, b_re):
    wr = _pad_last(jnp.concatenate([w_rg, w_re], axis=1), LANES)
    wr_hi = wr.astype(BF16)
    wr_lo = (wr - wr_hi.astype(F32)).astype(BF16)
    br = _pad_last(jnp.concatenate([b_rg, b_re]), LANES)[None, :]
    return dict(g=g[None, :], wr=jnp.concatenate([wr_hi, wr_lo], axis=1), br=br)


def kernel(x, positions, mix_norm, w_in, mla_g_cq, mla_g_ckv, mla_w_uq, mla_w_ukv, mla_g_q, mla_g_k, lru_conv_w, lru_conv_b, lru_w_a, lru_b_a, lru_w_i, lru_b_i, lru_lambda, s5_a_re, s5_a_im, s5_log_dt, s5_b_re, s5_b_im, s5_c_re, s5_c_im, s5_d, s5_w_glu, s5_b_glu, nsa_g_q, nsa_g_k, nsa_pe_k, nsa_w1_k, nsa_w2_k, nsa_pe_v, nsa_w1_v, nsa_w2_v, out_norm, w_out, ffn_norm, moe_w_rg, moe_b_rg, moe_w_re, moe_b_re, moe_w_gate, moe_w_up, moe_w_down):
    b, s, d = x.shape
    t = b * s
    trig = _trig(positions.astype(jnp.int32)[:, :, None])
    x2 = x.reshape(t, d)
    experts = (moe_w_gate.astype(BF16), moe_w_up.astype(BF16), moe_w_down.astype(BF16))
    for l in range(w_in.shape[0]):
        pn = _prep_nsa(nsa_g_q[l], nsa_g_k[l], nsa_pe_k[l], nsa_w1_k[l], nsa_w2_k[l], nsa_pe_v[l],
                       nsa_w1_v[l], nsa_w2_v[l])
        mq, mk, mv, o_lru, o_s5, nq, ks, kw, vs, vw, cg = _proj_prep(
            x2.reshape(b, s, d), mix_norm[l], _prep_w_in(w_in[l]), trig,
            _prep_mla(mla_g_cq[l], mla_g_ckv[l], mla_w_uq[l], mla_w_ukv[l], mla_g_q[l], mla_g_k[l]), pn)
        y_a = _mla(mq, mk, mv)
        y_b = _lru(o_lru, _prep_lru(lru_conv_w[l], lru_conv_b[l], lru_w_a[l], lru_b_a[l], lru_w_i[l],
                                    lru_b_i[l], lru_lambda[l]))
        y_c = _s5(o_s5.transpose(1, 0, 2),
                  _prep_s5(s5_a_re[l], s5_a_im[l], s5_log_dt[l], s5_b_re[l], s5_b_im[l], s5_c_re[l],
                           s5_c_im[l], s5_d[l], s5_w_glu[l], s5_b_glu[l])).transpose(1, 0, 2)
        y_d = _nsa(nq, ks, kw, vs, vw, cg, trig, pn)
        ys = [y.reshape(t, GROUP_W) for y in (y_a, y_b, y_c, y_d)]
        x2 = _outproj_moe(ys, x2, out_norm[l], w_out[l].astype(BF16),
                          _prep_moe(ffn_norm[l], moe_w_rg[l], moe_b_rg[l], moe_w_re[l], moe_b_re[l]),
                          experts, l)
    return x2.reshape(b, s, d)
```

```python
import functools
import math

import numpy as np
import jax
import jax.numpy as jnp
from jax import lax
from jax.experimental import pallas as pl
from jax.experimental.pallas import tpu as pltpu

F32 = jnp.float32
BF16 = jnp.bfloat16

D_MODEL = 1024
DEPTH = 2
GROUP_W = 256
EPS = 1e-6
ROPE_THETA = 500000.0
NEG = -1e30
LOG2E = math.log2(math.e)

MLA_HEADS = 4
MLA_ROPE = 32
MLA_NOPE = 64
MLA_V = 64
MLA_QK = 96
MLA_Q_RANK = 192
MLA_KV_RANK = 128

LRU_W = 256
LRU_BLOCKS = 4
LRU_BW = 64
CONV_W = 4
LRU_C = 8.0

S5_W = 256
S5_CH = 16
S5_GROUPS = 16
S5_P = 64
S5_STATE = S5_GROUPS * S5_P

NSA_HEADS = 4
NSA_DK = 64
NSA_ROT = 16
CMP_LEN = 32
CMP_STRIDE = 16
CMP_HID = 128
SEL_LEN = 64
SEL_TOPK = 5
WIN = 512

N_GROUPS = 4
EXP_PER_GROUP = 4
N_EXPERTS = 16
D_EXPERT = 256

D_IN = 1772

LANES = 128
VMEM_LIMIT = 48 * 1024 * 1024
MOE_VMEM_LIMIT = 56 * 1024 * 1024

MLA_IN_W = 512
LRU_IN_W = 512
S5_IN_W = 256
NQ_IN_W = NSA_HEADS * LANES
NKV_IN_W = 6 * LANES
IN_W = MLA_IN_W + LRU_IN_W + S5_IN_W + NQ_IN_W + NKV_IN_W


def _cparams(sem):
    return pltpu.CompilerParams(dimension_semantics=sem, vmem_limit_bytes=VMEM_LIMIT)


def _dot(a, b):
    return jnp.dot(a.astype(BF16), b.astype(BF16), preferred_element_type=F32)


def _dot_nt(a, b):
    return lax.dot_general(a.astype(BF16), b.astype(BF16), (((1,), (1,)), ((), ())),
                           preferred_element_type=F32)


def _split3(x):
    hi = x.astype(BF16)
    r = x - hi.astype(F32)
    mid = r.astype(BF16)
    lo = (r - mid.astype(F32)).astype(BF16)
    return hi, mid, lo


def _dot_f32_by_exact(x, w_bf16):
    hi, mid, lo = _split3(x)
    return (jnp.dot(hi, w_bf16, preferred_element_type=F32)
            + jnp.dot(mid, w_bf16, preferred_element_type=F32)
            + jnp.dot(lo, w_bf16, preferred_element_type=F32))


def _rms(x, g, n):
    return x * lax.rsqrt(jnp.sum(x * x, axis=-1, keepdims=True) * (1.0 / n) + EPS) * g


def _lane_sum(x):
    hi = x.astype(BF16)
    lo = (x - hi.astype(F32)).astype(BF16)
    ones = jnp.ones((x.shape[-1], LANES), BF16)
    return (jnp.dot(hi, ones, preferred_element_type=F32) + jnp.dot(lo, ones, preferred_element_type=F32))


def _rms_mxu(x, g, n):
    inv = lax.rsqrt(_lane_sum(x * x) * (1.0 / n) + EPS)
    if x.shape[-1] > LANES:
        inv = jnp.concatenate([inv] * (x.shape[-1] // LANES), axis=-1)
    return x * inv * g


def _gelu(x):
    return 0.5 * x * (1.0 + jnp.tanh(math.sqrt(2.0 / math.pi) * (x + 0.044715 * (x * x * x))))


def _rope(x, cos, sin_lo, sin_hi, half):
    return (x * cos + pltpu.roll(x, LANES - half, axis=1) * sin_lo
            + pltpu.roll(x, half, axis=1) * sin_hi)


def _with_ones(x, upper):
    if upper:
        x = pltpu.roll(x, LANES // 2, axis=1)
    lane = lax.broadcasted_iota(jnp.int32, x.shape, 1)
    return jnp.where(lane < LANES // 2, x, 1.0).astype(BF16)


NSA_TRIG_LANE0 = MLA_ROPE


def _trig_kernel(pos_ref, inv_ref, cos_ref, sin_ref):
    ang = pos_ref[0].astype(F32) * inv_ref[...]
    cos_ref[0] = jnp.cos(ang)
    sin_ref[0] = jnp.sin(ang)


def _trig(pos3, ts=512):
    def inv(rot):
        v = ROPE_THETA ** (-jnp.arange(rot // 2, dtype=F32) * 2.0 / rot)
        return jnp.concatenate([v, v])
    inv_l = _pad_last(jnp.concatenate([inv(MLA_ROPE), inv(NSA_ROT)]), LANES)[None, :]
    b, s, _ = pos3.shape
    spec = pl.BlockSpec((1, ts, LANES), lambda i, j: (i, j, 0))
    return pl.pallas_call(
        _trig_kernel,
        grid=(b, s // ts),
        in_specs=[pl.BlockSpec((1, ts, 1), lambda i, j: (i, j, 0)), _full((1, LANES))],
        out_specs=[spec, spec],
        out_shape=[jax.ShapeDtypeStruct((b, s, LANES), F32)] * 2,
        compiler_params=_cparams(("parallel", "parallel")),
        name="rope_trig",
    )(pos3, inv_l)


def _rope_tables(cs, sn, lane0, half):
    if lane0:
        cs = pltpu.roll(cs, LANES - lane0, axis=1)
        sn = pltpu.roll(sn, LANES - lane0, axis=1)
    lane = lax.broadcasted_iota(jnp.int32, cs.shape, 1)
    cos = jnp.where(lane < 2 * half, cs, 1.0)
    s_lo = jnp.where(lane < half, -sn, 0.0)
    s_hi = jnp.where((lane >= half) & (lane < 2 * half), sn, 0.0)
    return cos, s_lo, s_hi


def _pad_last(a, n):
    return jnp.pad(a, [(0, 0)] * (a.ndim - 1) + [(0, n - a.shape[-1])])


def _full(shape):
    nd = len(shape)
    return pl.BlockSpec(shape, lambda *_: (0,) * nd)


def _proj_prep_kernel(x_ref, g_ref, w_ref, cs_ref, sn_ref, gcq_ref, wuq_ref, gckv_ref, wk_ref, wv_ref,
                      gqm_ref, gkm_ref, gqn_ref, gkn_ref,
                      mq_out, mk_out, mv_out, lru_out, s5_out, nq_out, ks_out, kw_out, vs_out, vw_out,
                      cg_out):
    x = x_ref[0]
    h = x * lax.rsqrt(jnp.mean(x * x, axis=-1, keepdims=True) + EPS) * g_ref[...]
    y = jnp.dot(h.astype(BF16), w_ref[...], preferred_element_type=F32)
    cs, sn = cs_ref[0], sn_ref[0]
    c0 = 0
    _mla_prep(y[:, c0:c0 + MLA_IN_W], cs, sn, gcq_ref, wuq_ref, gckv_ref, wk_ref, wv_ref, gqm_ref, gkm_ref,
              mq_out, mk_out, mv_out)
    c0 += MLA_IN_W
    lru_out[0] = y[:, c0:c0 + LRU_IN_W]
    c0 += LRU_IN_W
    s5_out[0] = y[:, c0:c0 + S5_IN_W].astype(BF16)
    c0 += S5_IN_W
    nkv = y[:, c0 + NQ_IN_W:c0 + NQ_IN_W + NKV_IN_W]
    _nsa_prep(y[:, c0:c0 + NQ_IN_W], nkv, cs, sn, gqn_ref, gkn_ref, nq_out, ks_out, kw_out, vs_out, vw_out)
    for i, slot in enumerate((0, 3, 5)):
        cg_out[0, :, LANES * i:LANES * (i + 1)] = nkv[:, LANES * slot:LANES * (slot + 1)]


def _inproj_cols():
    src = -np.ones((IN_W,), np.int64)
    o = 0
    src[o:o + 192] = np.arange(0, 192)
    src[o + 256:o + 384] = np.arange(192, 320)
    src[o + 384:o + 416] = np.arange(320, 352)
    o += MLA_IN_W
    src[o:o + 512] = np.arange(352, 864)
    o += LRU_IN_W
    src[o:o + 256] = np.arange(864, 1120)
    o += S5_IN_W
    for h in range(NSA_HEADS):
        src[o + LANES * h:o + LANES * h + 64] = np.arange(1120 + 64 * h, 1120 + 64 * h + 64)
    o += NQ_IN_W
    kv0 = 1376
    src[o:o + 64] = np.arange(kv0, kv0 + 64)
    src[o + 128:o + 192] = np.arange(kv0 + 128, kv0 + 192)
    src[o + 256:o + 320] = np.arange(kv0 + 256, kv0 + 320)
    src[o + 384:o + 448] = np.arange(kv0 + 64, kv0 + 128)
    src[o + 448:o + 512] = np.arange(kv0 + 192, kv0 + 256)
    src[o + 512:o + 576] = np.arange(kv0 + 320, kv0 + 384)
    src[o + 640:o + 652] = np.arange(1760, 1772)
    return src


_INPROJ_SRC = _inproj_cols()


def _prep_w_in(w_in):
    idx = jnp.asarray(np.maximum(_INPROJ_SRC, 0), jnp.int32)
    keep = jnp.asarray(_INPROJ_SRC >= 0)
    return jnp.where(keep[None, :], jnp.take(w_in, idx, axis=1), 0.0).astype(BF16)


CG_W = 3 * LANES


def _proj_prep(x3, g, w_pad, trig, pm, pn, ts=512):
    b, s, _ = x3.shape
    heads = lambda: pl.BlockSpec((1, MLA_HEADS, ts, LANES), lambda i, j: (i, 0, j, 0))
    rows = lambda w: pl.BlockSpec((1, ts, w), lambda i, j: (i, j, 0))
    hshape = jax.ShapeDtypeStruct((b, MLA_HEADS, s, LANES), BF16)
    tok = lambda w, dt: jax.ShapeDtypeStruct((b, s, w), dt)
    return pl.pallas_call(
        _proj_prep_kernel,
        grid=(b, s // ts),
        in_specs=[rows(D_MODEL), _full((1, D_MODEL)), _full((D_MODEL, IN_W)), rows(LANES), rows(LANES),
                  _full((1, 2 * LANES)), _full((2 * LANES, MLA_HEADS * LANES)), _full((1, MLA_KV_RANK)),
                  _full((MLA_KV_RANK, MLA_HEADS * LANES)), _full((MLA_KV_RANK, MLA_HEADS * MLA_V)),
                  _full((1, LANES)), _full((1, LANES)),
                  _full((1, LANES)), _full((3, LANES))],
        out_specs=[heads(), heads(), heads(), rows(LRU_IN_W), rows(S5_IN_W),
                   heads(), rows(LANES), rows(LANES), rows(LANES), rows(LANES), rows(CG_W)],
        out_shape=[hshape, hshape, hshape, tok(LRU_IN_W, F32), tok(S5_IN_W, BF16),
                   hshape, tok(LANES, BF16), tok(LANES, BF16), tok(LANES, BF16), tok(LANES, BF16),
                   tok(CG_W, F32)],
        compiler_params=_cparams(("parallel", "parallel")),
        name="proj_prep",
    )(x3, g[None, :], w_pad, trig[0], trig[1], pm["g_cq"], pm["w_uq"], pm["g_ckv"], pm["w_k"], pm["w_v"],
      pm["g_q"], pm["g_k"], pn["g_q"], pn["g_k"])


def _mla_prep(xin, cs, sn, gcq_ref, wuq_ref, gckv_ref, wk_ref, wv_ref, gq_ref, gk_ref, q_out, k_out, v_out):
    cq = _rms_mxu(xin[:, 0:256], gcq_ref[...], MLA_Q_RANK)
    ckv = _rms_mxu(xin[:, 256:384], gckv_ref[...], MLA_KV_RANK)
    kpe = xin[:, 384:512]
    q = _dot(cq, wuq_ref[...])
    kn = _dot(ckv, wk_ref[...])
    v = _dot(ckv, wv_ref[...])
    cos, s_lo, s_hi = _rope_tables(cs, sn, 0, MLA_ROPE // 2)
    scale = MLA_QK ** -0.5 * LOG2E
    for h in range(MLA_HEADS):
        qh = _rms_mxu(q[:, LANES * h:LANES * (h + 1)], gq_ref[...], MLA_QK)
        q_out[0, h] = (_rope(qh, cos, s_lo, s_hi, MLA_ROPE // 2) * scale).astype(BF16)
        kh = _rms_mxu(kn[:, LANES * h:LANES * (h + 1)] + kpe, gk_ref[...], MLA_QK)
        k_out[0, h] = _rope(kh, cos, s_lo, s_hi, MLA_ROPE // 2).astype(BF16)
        v_out[0, h] = _with_ones(v[:, LANES * (h // 2):LANES * (h // 2 + 1)], h % 2 == 1)


def _attend(q, segs, v1, s_scr, p_scr, m_scr, nh, chunk):
    rows = q.shape[0]
    c0 = 0
    m = None
    for k, bias in segs:
        n = k.shape[0]
        sb = _dot_nt(q, k)
        if bias is not None:
            sb = (sb.reshape(nh, rows // nh, n) + bias[None]).reshape(rows, n)
        s_scr[0:rows, c0:c0 + n] = sb
        mx = jnp.max(sb, axis=-1, keepdims=True)
        m = mx if m is None else jnp.maximum(m, mx)
        c0 += n
    m_scr[0:rows, :] = jnp.broadcast_to(m, (rows, LANES))
    for r0 in range(0, rows, chunk):
        mb = m_scr[r0:r0 + chunk, :]
        for j0 in range(0, c0, LANES):
            p = jnp.exp2(s_scr[r0:r0 + chunk, j0:j0 + LANES] - mb)
            p_scr[r0:r0 + chunk, j0:j0 + LANES] = p.astype(BF16)
    o = jnp.dot(p_scr[0:rows, 0:c0], v1, preferred_element_type=F32)
    return (o * (1.0 / pltpu.roll(o, LANES // 2, axis=1)))[:, 0:LANES // 2]


def _mla_attn_kernel(q_ref, k_ref, v_ref, o_ref, s_scr, p_scr, m_scr, o_scr, bias_scr, *, tq, chunk):
    qi = pl.program_id(1)
    row = lax.broadcasted_iota(jnp.int32, (tq, tq), 0)
    col = lax.broadcasted_iota(jnp.int32, (tq, tq), 1)
    bias_scr[...] = jnp.where(col <= row, 0.0, NEG)
    nset = s_scr.shape[0]
    for c in range(k_ref.shape[2] // tq):

        @pl.when(qi == c)
        def _(c=c):
            n = (c + 1) * tq

            def heads(i, carry):
                for j in range(nset):
                    h = i * nset + j
                    segs = [(k_ref[0, h, n - tq:n, :], bias_scr[...])]
                    if c > 0:
                        segs = [(k_ref[0, h, 0:n - tq, :], None)] + segs
                    o_scr[h] = _attend(q_ref[0, h], segs, v_ref[0, h, 0:n, :], s_scr.at[j], p_scr.at[j],
                                       m_scr.at[j], 1, chunk)
                return carry

            lax.fori_loop(0, MLA_HEADS // nset, heads, 0)

    for h in range(MLA_HEADS):
        o_ref[0, :, MLA_V * h:MLA_V * (h + 1)] = o_scr[h].astype(o_ref.dtype)


def _mla(q, k, v, tq=512, chunk=128, nset=2):
    b, _, s, _ = q.shape
    return pl.pallas_call(
        functools.partial(_mla_attn_kernel, tq=tq, chunk=chunk),
        grid=(b, s // tq),
        in_specs=[pl.BlockSpec((1, MLA_HEADS, tq, LANES), lambda i, j: (i, 0, j, 0)),
                  pl.BlockSpec((1, MLA_HEADS, s, LANES), lambda i, j: (i, 0, 0, 0)),
                  pl.BlockSpec((1, MLA_HEADS, s, LANES), lambda i, j: (i, 0, 0, 0))],
        out_specs=pl.BlockSpec((1, tq, GROUP_W), lambda i, j: (i, j, 0)),
        out_shape=jax.ShapeDtypeStruct((b, s, GROUP_W), BF16),
        scratch_shapes=[pltpu.VMEM((nset, tq, s), F32), pltpu.VMEM((nset, tq, s), BF16),
                        pltpu.VMEM((nset, tq, LANES), F32),
                        pltpu.VMEM((MLA_HEADS, tq, MLA_V), F32), pltpu.VMEM((tq, tq), F32)],
        compiler_params=_cparams(("parallel", "arbitrary")),
        name="mla_attn",
    )(q, k, v)


def _prep_mla(g_cq, g_ckv, w_uq, w_ukv, g_q, g_k):
    wq = w_uq.reshape(MLA_Q_RANK, MLA_HEADS, MLA_QK)
    wq = _pad_last(wq, LANES).reshape(MLA_Q_RANK, MLA_HEADS * LANES)
    wq = jnp.pad(wq, ((0, 256 - MLA_Q_RANK), (0, 0)))
    wkv = w_ukv.reshape(MLA_KV_RANK, MLA_HEADS, MLA_NOPE + MLA_V)
    wk = jnp.pad(wkv[:, :, :MLA_NOPE], ((0, 0), (0, 0), (MLA_ROPE, LANES - MLA_QK)))
    wk = wk.reshape(MLA_KV_RANK, MLA_HEADS * LANES)
    wv = wkv[:, :, MLA_NOPE:].reshape(MLA_KV_RANK, MLA_HEADS * MLA_V)
    return dict(g_cq=_pad_last(g_cq, 256)[None, :], g_ckv=g_ckv[None, :], w_uq=wq.astype(BF16),
                w_k=wk.astype(BF16), w_v=wv.astype(BF16), g_q=_pad_last(g_q, LANES)[None, :],
                g_k=_pad_last(g_k, LANES)[None, :])


def _shift_rows(x, k, row, fill):
    return jnp.where(row >= k, pltpu.roll(x, k, axis=0), fill)


def _lru_kernel(in_ref, cw_ref, cb_ref, wa_ref, ba_ref, wi_ref, bi_ref, lam_ref, o_ref, a_scr, b_scr, *,
                chunk):
    xin = in_ref[0]
    s = xin.shape[0]
    xb = xin[:, :LRU_W]
    row = lax.broadcasted_iota(jnp.int32, (s, LRU_W), 0)
    u = cb_ref[...] + xb * cw_ref[CONV_W - 1:CONV_W, :]
    for j in range(CONV_W - 1):
        u = u + _shift_rows(xb, CONV_W - 1 - j, row, 0.0) * cw_ref[j:j + 1, :]
    r = jax.nn.sigmoid(_dot(u, wa_ref[...]) + ba_ref[...])
    gi = jax.nn.sigmoid(_dot(u, wi_ref[...]) + bi_ref[...])
    nlam = -lam_ref[...]
    softplus = jnp.maximum(nlam, 0.0) + jnp.log1p(jnp.exp(-jnp.abs(nlam)))
    log_a = (-LRU_C) * r * softplus
    a = jnp.exp(log_a)
    y = jnp.maximum(-jnp.tanh(log_a) * (a * a + 1.0), 0.0)
    mult = jnp.where(y > 0.0, y * lax.rsqrt(y), 0.0)
    mult = jnp.where(row == 0, 1.0, mult)
    bt = mult * gi * u
    rin = jnp.bitwise_and(row, chunk - 1)
    k = 1
    while k < chunk:
        bt = a * _shift_rows(bt, k, rin, 0.0) + bt
        a = a * _shift_rows(a, k, rin, 1.0)
        k *= 2
    a_scr[...] = a
    b_scr[...] = bt

    def carry_chunk(j, h_prev):
        r0 = pl.multiple_of(j * chunk, chunk)
        h = b_scr[pl.ds(r0, chunk), :] + a_scr[pl.ds(r0, chunk), :] * h_prev
        o_ref[0, pl.ds(r0, chunk), :] = (h * _gelu(in_ref[0, pl.ds(r0, chunk), LRU_W:])).astype(o_ref.dtype)
        return h[chunk - 1:chunk, :]

    lax.fori_loop(0, s // chunk, carry_chunk, jnp.zeros((1, LRU_W), F32))


def _block_diag(w):
    n, i, j = w.shape
    eye = jnp.eye(n, dtype=w.dtype)
    return (eye[:, None, :, None] * w[:, :, None, :]).reshape(n * i, n * j)


def _lru(lru_in, p, chunk=64):
    b, s, _ = lru_in.shape
    return pl.pallas_call(
        functools.partial(_lru_kernel, chunk=chunk),
        grid=(b,),
        in_specs=[pl.BlockSpec((1, s, LRU_IN_W), lambda i: (i, 0, 0)),
                  _full((CONV_W, LRU_W)), _full((1, LRU_W)), _full((LRU_W, LRU_W)), _full((1, LRU_W)),
                  _full((LRU_W, LRU_W)), _full((1, LRU_W)), _full((1, LRU_W))],
        out_specs=pl.BlockSpec((1, s, LRU_W), lambda i: (i, 0, 0)),
        out_shape=jax.ShapeDtypeStruct((b, s, LRU_W), BF16),
        scratch_shapes=[pltpu.VMEM((s, LRU_W), F32), pltpu.VMEM((s, LRU_W), F32)],
        compiler_params=_cparams(("parallel",)),
        name="rglru",
    )(lru_in, p["cw"], p["cb"], p["wa"], p["ba"], p["wi"], p["bi"], p["lam"])


def _prep_lru(conv_w, conv_b, w_a, b_a, w_i, b_i, lam):
    return dict(cw=conv_w, cb=conv_b[None, :], wa=_block_diag(w_a).astype(BF16),
                ba=b_a.reshape(1, LRU_W), wi=_block_diag(w_i).astype(BF16), bi=b_i.reshape(1, LRU_W),
                lam=lam[None, :])


def _s5_disc_kernel(are_ref, aim_ref, ldt_ref, arer_ref, aimr_ref, bre_ref, bim_ref,
                    abre_ref, abim_ref, bbre_ref, bbim_ref):
    dt = jnp.exp(ldt_ref[...])

    def disc(a_re, a_im):
        mag = jnp.exp(dt * a_re)
        ab_re = mag * jnp.cos(dt * a_im)
        ab_im = mag * jnp.sin(dt * a_im)
        den = a_re * a_re + a_im * a_im
        n_re = ab_re - 1.0
        g_re = (n_re * a_re + ab_im * a_im) / den
        g_im = (ab_im * a_re - n_re * a_im) / den
        return ab_re, ab_im, g_re, g_im

    ab_re, ab_im, _, _ = disc(are_ref[...], aim_ref[...])
    abre_ref[...] = ab_re
    abim_ref[...] = ab_im
    _, _, g_re, g_im = disc(arer_ref[...], aimr_ref[...])
    bbre_ref[...] = g_re * bre_ref[...] - g_im * bim_ref[...]
    bbim_ref[...] = g_re * bim_ref[...] + g_im * bre_ref[...]


def _s5_kernel(u_ref, are_ref, aim_ref, bre_ref, bim_ref, cre_ref, cim_ref, d_ref, wg_ref, bg_ref,
               o_ref, hre, him, st_re, st_im, *, tc, nb, cw):
    @pl.when(pl.program_id(0) == 0)
    def _():
        st_re[...] = jnp.zeros_like(st_re)
        st_im[...] = jnp.zeros_like(st_im)

    ub = u_ref[...].reshape(tc * nb, S5_W)
    u = ub.astype(F32)
    hre[...] = jnp.dot(ub, bre_ref[...], preferred_element_type=F32)
    him[...] = jnp.dot(ub, bim_ref[...], preferred_element_type=F32)
    for c in range(S5_STATE // cw):
        cs = slice(c * cw, (c + 1) * cw)
        ar = jnp.broadcast_to(are_ref[:, cs], (nb, cw))
        ai = jnp.broadcast_to(aim_ref[:, cs], (nb, cw))

        def body(t, carry, cs=cs, ar=ar, ai=ai):
            hr, hi = carry
            r0 = pl.multiple_of(t * nb, nb)
            nr = ar * hr - ai * hi + hre[pl.ds(r0, nb), cs]
            ni = ar * hi + ai * hr + him[pl.ds(r0, nb), cs]
            hre[pl.ds(r0, nb), cs] = nr
            him[pl.ds(r0, nb), cs] = ni
            return nr, ni

        hr, hi = lax.fori_loop(0, tc, body, (st_re[:, cs], st_im[:, cs]), unroll=4)
        st_re[:, cs] = hr
        st_im[:, cs] = hi
    y = (jnp.dot(hre[...].astype(BF16), cre_ref[...], preferred_element_type=F32)
         - jnp.dot(him[...].astype(BF16), cim_ref[...], preferred_element_type=F32))
    y = _gelu(y + d_ref[...] * u)
    z = _dot(y, wg_ref[...]) + bg_ref[...]
    o_ref[...] = (y * jax.nn.sigmoid(z)).astype(o_ref.dtype).reshape(tc, nb, S5_W)


def _prep_s5(a_re, a_im, log_dt, b_re, b_im, c_re, c_im, d, w_glu, b_glu):
    g, p, ch = S5_GROUPS, S5_P, S5_CH
    ab_re, ab_im, bb_re, bb_im = pl.pallas_call(
        _s5_disc_kernel,
        out_shape=[jax.ShapeDtypeStruct((g, p), F32), jax.ShapeDtypeStruct((g, p), F32),
                   jax.ShapeDtypeStruct((g, p * ch), F32), jax.ShapeDtypeStruct((g, p * ch), F32)],
        name="s5_discretize",
    )(a_re, a_im, log_dt[:, None], jnp.repeat(a_re, ch, axis=1), jnp.repeat(a_im, ch, axis=1),
      b_re.reshape(g, p * ch), b_im.reshape(g, p * ch))

    def b_dense(bb):
        return _block_diag(bb.reshape(g, p, ch).transpose(0, 2, 1)).astype(BF16)

    def c_dense(c):
        return _block_diag(c.transpose(0, 2, 1)).astype(BF16)

    return dict(a_re=ab_re.reshape(1, S5_STATE), a_im=ab_im.reshape(1, S5_STATE),
                b_re=b_dense(bb_re), b_im=b_dense(bb_im), c_re=c_dense(c_re), c_im=c_dense(c_im),
                d=d[None, :], w_glu=w_glu.astype(BF16), b_glu=b_glu[None, :])


def _s5(u_tm, p, tc=64, cw=256):
    s, nb, _ = u_tm.shape
    return pl.pallas_call(
        functools.partial(_s5_kernel, tc=tc, nb=nb, cw=cw),
        grid=(s // tc,),
        in_specs=[pl.BlockSpec((tc, nb, S5_W), lambda i: (i, 0, 0)),
                  _full((1, S5_STATE)), _full((1, S5_STATE)),
                  _full((S5_W, S5_STATE)), _full((S5_W, S5_STATE)),
                  _full((S5_STATE, S5_W)), _full((S5_STATE, S5_W)),
                  _full((1, S5_W)), _full((S5_W, S5_W)), _full((1, S5_W))],
        out_specs=pl.BlockSpec((tc, nb, S5_W), lambda i: (i, 0, 0)),
        out_shape=jax.ShapeDtypeStruct((s, nb, S5_W), BF16),
        scratch_shapes=[pltpu.VMEM((tc * nb, S5_STATE), F32), pltpu.VMEM((tc * nb, S5_STATE), F32),
                        pltpu.VMEM((nb, S5_STATE), F32), pltpu.VMEM((nb, S5_STATE), F32)],
        compiler_params=_cparams(("arbitrary",)),
        name="s5",
    )(u_tm, p["a_re"], p["a_im"], p["b_re"], p["b_im"], p["c_re"], p["c_im"], p["d"], p["w_glu"],
      p["b_glu"])


def _nsa_prep(nq, nkv, cs, sn, gq_ref, gk_ref, q_out, ks_out, kw_out, vs_out, vw_out):
    half = NSA_ROT // 2
    cos, s_lo, s_hi = _rope_tables(cs, sn, NSA_TRIG_LANE0, half)
    scale = NSA_DK ** -0.5 * LOG2E
    for h in range(NSA_HEADS):
        qh = _rms_mxu(nq[:, LANES * h:LANES * (h + 1)], gq_ref[...], NSA_DK)
        q_out[0, h] = (_rope(qh, cos, s_lo, s_hi, half) * scale).astype(BF16)
    ks = _rms_mxu(nkv[:, 128:256], gk_ref[1:2, :], NSA_DK)
    ks_out[0] = _rope(ks, cos, s_lo, s_hi, half).astype(BF16)
    kw = _rms_mxu(nkv[:, 256:384], gk_ref[2:3, :], NSA_DK)
    kw_out[0] = _rope(kw, cos, s_lo, s_hi, half).astype(BF16)
    vs_out[0] = _with_ones(nkv[:, 384:512], True)
    vw_out[0] = _with_ones(nkv[:, 512:640], False)


def _nsa_cmp_kernel(k_in, v_in, cs_ref, sn_ref, pek_ref, w1k_ref, w2k_ref, pev_ref, w1v_ref, w2v_ref,
                    g_ref, kc_out, vc_out):
    nc = kc_out.shape[1]

    def compress(x_ref, pe_ref, w1_ref, w2_ref):
        lo = hi = None
        for j in range(CMP_STRIDE):
            xj = x_ref[0, pl.ds(j, nc, stride=CMP_STRIDE), :]
            dl = _dot(xj + pe_ref[j:j + 1, :], w1_ref[j])
            dh = _dot(xj + pe_ref[CMP_STRIDE + j:CMP_STRIDE + j + 1, :], w1_ref[CMP_STRIDE + j])
            lo = dl if lo is None else lo + dl
            hi = dh if hi is None else hi + dh
        hid = lo + pltpu.roll(hi, nc - 1, axis=0)
        return _dot(_gelu(hid), w2_ref[...])

    kc = compress(k_in, pek_ref, w1k_ref, w2k_ref)
    cos, s_lo, s_hi = _rope_tables(cs_ref[0], sn_ref[0], NSA_TRIG_LANE0, NSA_ROT // 2)
    kc = _rope(_rms(kc, g_ref[0:1, :], NSA_DK), cos, s_lo, s_hi, NSA_ROT // 2)
    kc_out[0] = kc.astype(BF16)
    vc_out[0] = compress(v_in, pev_ref, w1v_ref, w2v_ref).astype(BF16)


def _nsa_cmpsel_kernel(q_ref, kc_ref, vc_ref, ov_ref, ocmp_ref, sel_ref, *, tc):
    qi = pl.program_id(1)
    nh = NSA_HEADS
    q = q_ref[0].reshape(nh * tc, LANES)
    lane = lax.broadcasted_iota(jnp.int32, (tc, LANES), 1)
    qpos = qi * tc + lax.broadcasted_iota(jnp.int32, (tc, LANES), 0)

    s = _dot_nt(q, kc_ref[0]).reshape(nh, tc, LANES)
    valid = (lane * CMP_STRIDE + (CMP_LEN - 1)) <= qpos
    s = jnp.where(valid, s, NEG)
    e = jnp.exp2(s - jnp.max(s, axis=-1, keepdims=True))
    p_c = jnp.where(valid, e / jnp.sum(e, axis=-1, keepdims=True), 0.0)
    ocmp_ref[0] = jnp.dot(p_c.astype(BF16).reshape(nh * tc, LANES), vc_ref[0],
                          preferred_element_type=F32).reshape(nh, tc, NSA_DK)

    imp = _dot_f32_by_exact(jnp.sum(p_c, axis=0), ov_ref[...])
    cur = qpos // SEL_LEN
    lane_f = lane.astype(F32)
    sel = (lane == 0) | (lane == cur) | (lane == cur - 1)
    cand = (lane < cur - 1) & (lane > 0)
    for _ in range(SEL_TOPK - 3):
        sc = jnp.where(cand, imp, -jnp.inf)
        best = jnp.max(sc, axis=-1, keepdims=True)
        pick = jnp.min(jnp.where(cand & (sc == best), lane_f, float(LANES)), axis=-1, keepdims=True)
        hit = lane_f == pick
        sel = sel | hit
        cand = cand & jnp.logical_not(hit)
    sel_ref[0] = jnp.where(sel, 1.0, 0.0).astype(BF16)


def _nsa_attn_kernel(q_ref, sel_ref, ocmp_ref, ks_ref, vs_ref, kw_ref, vw_ref, gate_ref, ex_ref,
                     o_ref, s_scr, p_scr, m_scr, obr_scr, *, tq, chunk, kchunk):
    qi = pl.program_id(1)
    nh = NSA_HEADS
    rows = nh * tq
    s_len = ks_ref.shape[1]

    def window():
        wk = WIN + tq
        start = pl.multiple_of(jnp.maximum(qi - WIN // tq, 0) * tq, tq)
        kpos = start + lax.broadcasted_iota(jnp.int32, (tq, wk), 1)
        qrow = qi * tq + lax.broadcasted_iota(jnp.int32, (tq, wk), 0)
        bias = jnp.where((kpos <= qrow) & (qrow - kpos < WIN), 0.0, NEG)
        obr_scr[1] = _attend(q_ref[0].reshape(rows, LANES), [(kw_ref[0, pl.ds(start, wk), :], bias)],
                             vw_ref[0, pl.ds(start, wk), :], s_scr.at[1], p_scr.at[1], m_scr.at[1],
                             nh, chunk)

    for c in range(s_len // kchunk):

        @pl.when(qi // (kchunk // tq) == c)
        def _(c=c):
            n = (c + 1) * kchunk
            em = jnp.dot(sel_ref[0], ex_ref[:, 0:n], preferred_element_type=F32)
            kpos = lax.broadcasted_iota(jnp.int32, (tq, n), 1)
            qrow = qi * tq + lax.broadcasted_iota(jnp.int32, (tq, n), 0)
            bias = jnp.where((em > 0.5) & (kpos <= qrow), 0.0, NEG)
            obr_scr[0] = _attend(q_ref[0].reshape(rows, LANES), [(ks_ref[0, 0:n, :], bias)],
                                 vs_ref[0, 0:n, :], s_scr.at[0], p_scr.at[0], m_scr.at[0], nh, chunk)
            window()

    g = jax.nn.sigmoid(gate_ref[0])
    for h in range(nh):
        r = slice(h * tq, (h + 1) * tq)
        o = (g[:, 3 * h:3 * h + 1] * ocmp_ref[0, h] + g[:, 3 * h + 1:3 * h + 2] * obr_scr[0, r, :]
             + g[:, 3 * h + 2:3 * h + 3] * obr_scr[1, r, :])
        o_ref[0, :, NSA_DK * h:NSA_DK * (h + 1)] = o.astype(o_ref.dtype)


def _nsa_tables(s, tq):
    nc = s // CMP_STRIDE
    nsb = s // SEL_LEN
    cs = np.arange(nc) * CMP_STRIDE
    ss = np.arange(nsb) * SEL_LEN
    ov = np.clip(np.minimum(cs[:, None] + CMP_LEN, ss[None, :] + SEL_LEN)
                 - np.maximum(cs[:, None], ss[None, :]), 0, None) / CMP_STRIDE
    ov[(s - CMP_LEN) // CMP_STRIDE + 1:] = 0.0
    ov_pad = np.zeros((nc, LANES), np.float32)
    ov_pad[:, :nsb] = ov
    ex = np.zeros((LANES, s), np.float32)
    ex[np.arange(s) // SEL_LEN, np.arange(s)] = 1.0
    return jnp.asarray(ov_pad, BF16), jnp.asarray(ex, BF16)


def _nsa(q, ks, kw, vs, vw, cg, trig, p, tq=256, tc=512, chunk=128, kchunk=512):
    b, s, _ = ks.shape
    nc = s // CMP_STRIDE
    assert nc == LANES and s // SEL_LEN <= LANES and s >= WIN + tq and s % kchunk == 0

    last = np.minimum(np.arange(nc) * CMP_STRIDE + CMP_LEN - 1, s - 1)
    cspec = pl.BlockSpec((1, nc, LANES), lambda i: (i, 0, 0))
    kc, vc = pl.pallas_call(
        _nsa_cmp_kernel,
        grid=(b,),
        in_specs=[pl.BlockSpec((1, s, LANES), lambda i: (i, 0, 0)),
                  pl.BlockSpec((1, s, LANES), lambda i: (i, 0, 1)),
                  cspec, cspec,
                  _full((CMP_LEN, LANES)), _full((CMP_LEN, LANES, CMP_HID)), _full((CMP_HID, LANES)),
                  _full((CMP_LEN, LANES)), _full((CMP_LEN, LANES, CMP_HID)), _full((CMP_HID, NSA_DK)),
                  _full((3, LANES))],
        out_specs=[cspec, pl.BlockSpec((1, nc, NSA_DK), lambda i: (i, 0, 0))],
        out_shape=[jax.ShapeDtypeStruct((b, nc, LANES), BF16),
                   jax.ShapeDtypeStruct((b, nc, NSA_DK), BF16)],
        compiler_params=_cparams(("parallel",)),
        name="nsa_compress",
    )(cg, cg, trig[0][:, last, :], trig[1][:, last, :], p["pe_k"], p["w1_k"], p["w2_k"], p["pe_v"],
      p["w1_v"], p["w2_v"], p["g_k"])

    ov, ex = _nsa_tables(s, tq)
    o_cmp, sel = pl.pallas_call(
        functools.partial(_nsa_cmpsel_kernel, tc=tc),
        grid=(b, s // tc),
        in_specs=[pl.BlockSpec((1, NSA_HEADS, tc, LANES), lambda i, j: (i, 0, j, 0)),
                  pl.BlockSpec((1, nc, LANES), lambda i, j: (i, 0, 0)),
                  pl.BlockSpec((1, nc, NSA_DK), lambda i, j: (i, 0, 0)),
                  _full((nc, LANES))],
        out_specs=[pl.BlockSpec((1, NSA_HEADS, tc, NSA_DK), lambda i, j: (i, 0, j, 0)),
                   pl.BlockSpec((1, tc, LANES), lambda i, j: (i, j, 0))],
        out_shape=[jax.ShapeDtypeStruct((b, NSA_HEADS, s, NSA_DK), F32),
                   jax.ShapeDtypeStruct((b, s, LANES), BF16)],
        compiler_params=_cparams(("parallel", "parallel")),
        name="nsa_cmpsel",
    )(q, kc, vc, ov)

    rows = NSA_HEADS * tq
    kvspec = pl.BlockSpec((1, s, LANES), lambda i, j: (i, 0, 0))
    return pl.pallas_call(
        functools.partial(_nsa_attn_kernel, tq=tq, chunk=chunk, kchunk=kchunk),
        grid=(b, s // tq),
        in_specs=[pl.BlockSpec((1, NSA_HEADS, tq, LANES), lambda i, j: (i, 0, j, 0)),
                  pl.BlockSpec((1, tq, LANES), lambda i, j: (i, j, 0)),
                  pl.BlockSpec((1, NSA_HEADS, tq, NSA_DK), lambda i, j: (i, 0, j, 0)),
                  kvspec, kvspec, kvspec, kvspec,
                  pl.BlockSpec((1, tq, LANES), lambda i, j: (i, j, CG_W // LANES - 1)),
                  _full((LANES, s))],
        out_specs=pl.BlockSpec((1, tq, GROUP_W), lambda i, j: (i, j, 0)),
        out_shape=jax.ShapeDtypeStruct((b, s, GROUP_W), BF16),
        scratch_shapes=[pltpu.VMEM((2, rows, s), F32), pltpu.VMEM((2, rows, s), BF16),
                        pltpu.VMEM((2, rows, LANES), F32), pltpu.VMEM((2, rows, NSA_DK), F32)],
        compiler_params=_cparams(("parallel", "arbitrary")),
        name="nsa_attn",
    )(q, sel, o_cmp, ks, vs, kw, vw, cg, ex)


def _prep_nsa(g_q, g_k, pe_k, w1_k, w2_k, pe_v, w1_v, w2_v):
    def per_token(w1):
        w = w1.reshape(CMP_LEN, NSA_DK, CMP_HID)
        return jnp.pad(w, ((0, 0), (0, LANES - NSA_DK), (0, 0))).astype(BF16)

    return dict(g_q=_pad_last(g_q, LANES)[None, :], g_k=_pad_last(g_k, LANES),
                pe_k=_pad_last(pe_k, LANES), w1_k=per_token(w1_k),
                w2_k=_pad_last(w2_k, LANES).astype(BF16),
                pe_v=_pad_last(pe_v, LANES), w1_v=per_token(w1_v), w2_v=w2_v.astype(BF16))


MOE_BLOCK = 144
MOE_ALIGN = 16


def _route_t(lt):
    row = lambda r: lt[r:r + 1, :]
    lg = [row(i) for i in range(N_GROUPS)]
    gmax = functools.reduce(jnp.maximum, lg)
    pg_top = 1.0 / functools.reduce(lambda a, b: a + b, [jnp.exp(v - gmax) for v in lg])
    taken = jnp.zeros_like(gmax) > 1.0
    oh = []
    for v in lg:
        hit = (v == gmax) & jnp.logical_not(taken)
        oh.append(hit)
        taken = taken | hit
    le = []
    for k in range(EXP_PER_GROUP):
        v = row(N_GROUPS + k)
        for i in range(1, N_GROUPS):
            v = jnp.where(oh[i], row(N_GROUPS + EXP_PER_GROUP * i + k), v)
        le.append(v)
    m1 = functools.reduce(jnp.maximum, le)
    taken = jnp.zeros_like(m1) > 1.0
    first = []
    for v in le:
        hit = (v == m1) & jnp.logical_not(taken)
        first.append(hit)
        taken = taken | hit
    le2 = [jnp.where(f, -jnp.inf, v) for f, v in zip(first, le)]
    m2 = functools.reduce(jnp.maximum, le2)
    taken = jnp.zeros_like(m1) > 1.0
    second = []
    for v in le2:
        hit = (v == m2) & jnp.logical_not(taken)
        second.append(hit)
        taken = taken | hit
    v2 = jnp.exp(m2 - m1)
    w1 = pg_top / (1.0 + v2)
    w2 = pg_top * v2 / (1.0 + v2)
    comb = []
    for i in range(N_GROUPS):
        for k in range(EXP_PER_GROUP):
            w = jnp.where(first[k], w1, 0.0) + jnp.where(second[k], w2, 0.0)
            comb.append(jnp.where(oh[i], w, 0.0))
    return [jnp.where(o, 1.0, 0.0) for o in oh], comb


def _moe_kernel(ya_ref, yb_ref, yc_ref, yd_ref, x_ref, gout_ref, wout_ref, g_ref, wr_ref,
                br_ref, tri_ref, wg_ref, wu_ref, wd_ref, o_ref, hs, cs, ys, tok, meta):
    grp = pl.program_id(1)
    tm = x_ref.shape[0]
    rs = hs.shape[0]

    @pl.when(grp == 0)
    def _():
        x = x_ref[...]
        for i, y_ref in enumerate((ya_ref, yb_ref, yc_ref, yd_ref)):
            y = _rms(y_ref[...].astype(F32), gout_ref[i:i + 1, :], GROUP_W)
            x = x + _dot(y, wout_ref[GROUP_W * i:GROUP_W * (i + 1), :])
        o_ref[...] = x
        h = x * lax.rsqrt(jnp.mean(x * x, axis=-1, keepdims=True) + EPS) * g_ref[...]
        h_hi = h.astype(BF16)
        h_lo = (h - h_hi.astype(F32)).astype(BF16)
        lhl = jnp.dot(h_hi, wr_ref[...], preferred_element_type=F32)
        logits = (lhl[:, 0:LANES] + lhl[:, LANES:2 * LANES]
                  + jnp.dot(h_lo, wr_ref[:, 0:LANES], preferred_element_type=F32)) + br_ref[...]
        oh, comb = _route_t(logits.T)

        oh8 = jnp.concatenate(oh + [jnp.zeros((8 - N_GROUPS, tm), F32)], axis=0)
        cum = jnp.dot(oh8.astype(BF16), tri_ref[...], preferred_element_type=F32)
        cnt = jnp.sum(oh8, axis=1, keepdims=True)
        padded = jnp.floor((cnt + (MOE_BLOCK - 1)) * (1.0 / MOE_BLOCK)) * MOE_BLOCK
        starts, acc0 = [], jnp.zeros((1, 1), F32)
        for i in range(N_GROUPS):
            starts.append(acc0)
            acc0 = acc0 + padded[i:i + 1, :]
            meta[i] = starts[i][0, 0].astype(jnp.int32)
            meta[N_GROUPS + i] = (padded[i:i + 1, :][0, 0] * (1.0 / MOE_BLOCK)).astype(jnp.int32)
        dest = functools.reduce(lambda a, b: a + b,
                                [oh[i] * (starts[i] + cum[i:i + 1, :] - 1.0) for i in range(N_GROUPS)])
        perm = jnp.where(lax.broadcasted_iota(jnp.int32, (rs, tm), 0).astype(F32) == dest, 1.0, 0.0)
        perm = perm.astype(BF16)

        comb4 = [functools.reduce(lambda a, b: a + b, comb[k::EXP_PER_GROUP]) for k in range(EXP_PER_GROUP)]
        stack = jnp.concatenate(comb4 + [dest] + [jnp.zeros((LANES - EXP_PER_GROUP - 1, tm), F32)], axis=0)
        stack_t = stack.T
        tok[...] = stack_t
        c_hi = stack_t.astype(BF16)
        c_lo = (stack_t - c_hi.astype(F32)).astype(BF16)
        hs[...] = jnp.dot(perm, h_hi, preferred_element_type=F32).astype(BF16)
        cs[...] = jnp.dot(perm, jnp.concatenate([c_hi, c_lo], axis=1), preferred_element_type=F32)
        ys[...] = jnp.zeros_like(ys)

    def experts(r0, nrows):
        hb = hs[pl.ds(r0, nrows), :]
        cb = cs[pl.ds(r0, nrows), :]
        cb = cb[:, 0:LANES] + cb[:, LANES:2 * LANES]
        acc = None
        for k in range(EXP_PER_GROUP):
            a = jax.nn.silu(jnp.dot(hb, wg_ref[k], preferred_element_type=F32)) \
                * jnp.dot(hb, wu_ref[k], preferred_element_type=F32)
            d = jnp.dot((a * cb[:, k:k + 1]).astype(BF16), wd_ref[k], preferred_element_type=F32)
            acc = d if acc is None else acc + d
        ys[pl.ds(r0, nrows), :] = acc.astype(BF16)

    base = meta[grp]
    nblk = meta[N_GROUPS + grp]

    def pair(j, carry):
        experts(pl.multiple_of(base + j * (2 * MOE_BLOCK), MOE_ALIGN), 2 * MOE_BLOCK)
        return carry

    lax.fori_loop(0, nblk // 2, pair, 0)

    @pl.when(nblk % 2 == 1)
    def _():
        experts(pl.multiple_of(base + (nblk - 1) * MOE_BLOCK, MOE_ALIGN), MOE_BLOCK)

    @pl.when(grp == N_GROUPS - 1)
    def _():
        dest_t = tok[:, EXP_PER_GROUP:EXP_PER_GROUP + 1]
        unperm = jnp.where(lax.broadcasted_iota(jnp.int32, (tm, rs), 1).astype(F32) == dest_t, 1.0, 0.0)
        o_ref[...] += jnp.dot(unperm.astype(BF16), ys[...], preferred_element_type=F32)


def _sorted_rows(tm):
    low = -tm % MOE_BLOCK
    pad = low + MOE_BLOCK * ((N_GROUPS * (MOE_BLOCK - 1) - low) // MOE_BLOCK)
    return -(-(tm + pad) // LANES) * LANES


def _outproj_moe(ys, x2, g_out, w_out, p, experts, layer, tm=1024):
    t = x2.shape[0]
    rs = _sorted_rows(tm)
    yspec = pl.BlockSpec((tm, GROUP_W), lambda i, g: (i, 0))
    tri = jnp.asarray(np.triu(np.ones((tm, tm), np.float32)), BF16)

    def const(shape):
        nd = len(shape)
        return pl.BlockSpec(shape, lambda i, g: (0,) * nd)

    def resident(shape):
        nd = len(shape)
        return pl.BlockSpec(shape, lambda i, g: (0,) * nd, pipeline_mode=pl.Buffered(1))

    def of_group(shape):
        return pl.BlockSpec((None, EXP_PER_GROUP) + shape, lambda i, g: (layer, g, 0, 0))

    return pl.pallas_call(
        _moe_kernel,
        grid=(t // tm, N_GROUPS),
        in_specs=[yspec, yspec, yspec, yspec, pl.BlockSpec((tm, D_MODEL), lambda i, g: (i, 0)),
                  const((4, GROUP_W)), resident((D_MODEL, D_MODEL)),
                  const((1, D_MODEL)), const((D_MODEL, 2 * LANES)),
                  const((1, LANES)), resident((tm, tm)),
                  of_group((D_MODEL, D_EXPERT)), of_group((D_MODEL, D_EXPERT)),
                  of_group((D_EXPERT, D_MODEL))],
        out_specs=pl.BlockSpec((tm, D_MODEL), lambda i, g: (i, 0)),
        out_shape=jax.ShapeDtypeStruct((t, D_MODEL), F32),
        scratch_shapes=[pltpu.VMEM((rs, D_MODEL), BF16), pltpu.VMEM((rs, 2 * LANES), F32),
                        pltpu.VMEM((rs, D_MODEL), BF16), pltpu.VMEM((tm, LANES), F32),
                        pltpu.SMEM((2 * N_GROUPS,), jnp.int32)],
        compiler_params=pltpu.CompilerParams(dimension_semantics=("parallel", "arbitrary"),
                                             vmem_limit_bytes=MOE_VMEM_LIMIT),
        name="outproj_moe",
    )(*ys, x2, g_out, w_out, p["g"], p["wr"], p["br"], tri, *experts)


def _prep_moe(g, w_rg, b_rg, w_re, b_re):
    wr = _pad_last(jnp.concatenate([w_rg, w_re], axis=1), LANES)
    wr_hi = wr.astype(BF16)
    wr_lo = (wr - wr_hi.astype(F32)).astype(BF16)
    br = _pad_last(jnp.concatenate([b_rg, b_re]), LANES)[None, :]
    return dict(g=g[None, :], wr=jnp.concatenate([wr_hi, wr_lo], axis=1), br=br)


def kernel(x, positions, mix_norm, w_in, mla_g_cq, mla_g_ckv, mla_w_uq, mla_w_ukv, mla_g_q, mla_g_k, lru_conv_w, lru_conv_b, lru_w_a, lru_b_a, lru_w_i, lru_b_i, lru_lambda, s5_a_re, s5_a_im, s5_log_dt, s5_b_re, s5_b_im, s5_c_re, s5_c_im, s5_d, s5_w_glu, s5_b_glu, nsa_g_q, nsa_g_k, nsa_pe_k, nsa_w1_k, nsa_w2_k, nsa_pe_v, nsa_w1_v, nsa_w2_v, out_norm, w_out, ffn_norm, moe_w_rg, moe_b_rg, moe_w_re, moe_b_re, moe_w_gate, moe_w_up, moe_w_down):
    b, s, d = x.shape
    t = b * s
    trig = _trig(positions.astype(jnp.int32)[:, :, None])
    x2 = x.reshape(t, d)
    experts = (moe_w_gate.astype(BF16), moe_w_up.astype(BF16), moe_w_down.astype(BF16))
    for l in range(w_in.shape[0]):
        pn = _prep_nsa(nsa_g_q[l], nsa_g_k[l], nsa_pe_k[l], nsa_w1_k[l], nsa_w2_k[l], nsa_pe_v[l],
                       nsa_w1_v[l], nsa_w2_v[l])
        mq, mk, mv, o_lru, o_s5, nq, ks, kw, vs, vw, cg = _proj_prep(
            x2.reshape(b, s, d), mix_norm[l], _prep_w_in(w_in[l]), trig,
            _prep_mla(mla_g_cq[l], mla_g_ckv[l], mla_w_uq[l], mla_w_ukv[l], mla_g_q[l], mla_g_k[l]), pn)
        y_a = _mla(mq, mk, mv)
        y_b = _lru(o_lru, _prep_lru(lru_conv_w[l], lru_conv_b[l], lru_w_a[l], lru_b_a[l], lru_w_i[l],
                                    lru_b_i[l], lru_lambda[l]))
        y_c = _s5(o_s5.transpose(1, 0, 2),
                  _prep_s5(s5_a_re[l], s5_a_im[l], s5_log_dt[l], s5_b_re[l], s5_b_im[l], s5_c_re[l],
                           s5_c_im[l], s5_d[l], s5_w_glu[l], s5_b_glu[l])).transpose(1, 0, 2)
        y_d = _nsa(nq, ks, kw, vs, vw, cg, trig, pn)
        ys = [y.reshape(t, GROUP_W) for y in (y_a, y_b, y_c, y_d)]
        x2 = _outproj_moe(ys, x2, out_norm[l], w_out[l].astype(BF16),
                          _prep_moe(ffn_norm[l], moe_w_rg[l], moe_b_rg[l], moe_w_re[l], moe_b_re[l]),
                          experts, l)
    return x2.reshape(b, s, d)
```

```python
import functools
import math

import numpy as np
import jax
import jax.numpy as jnp
from jax import lax
from jax.experimental import pallas as pl
from jax.experimental.pallas import tpu as pltpu

F32 = jnp.float32
BF16 = jnp.bfloat16

D_MODEL = 1024
DEPTH = 2
GROUP_W = 256
EPS = 1e-6
ROPE_THETA = 500000.0
NEG = -1e30
LOG2E = math.log2(math.e)

MLA_HEADS = 4
MLA_ROPE = 32
MLA_NOPE = 64
MLA_V = 64
MLA_QK = 96
MLA_Q_RANK = 192
MLA_KV_RANK = 128

LRU_W = 256
LRU_BLOCKS = 4
LRU_BW = 64
CONV_W = 4
LRU_C = 8.0

S5_W = 256
S5_CH = 16
S5_GROUPS = 16
S5_P = 64
S5_STATE = S5_GROUPS * S5_P

NSA_HEADS = 4
NSA_DK = 64
NSA_ROT = 16
CMP_LEN = 32
CMP_STRIDE = 16
CMP_HID = 128
SEL_LEN = 64
SEL_TOPK = 5
WIN = 512

N_GROUPS = 4
EXP_PER_GROUP = 4
N_EXPERTS = 16
D_EXPERT = 256

D_IN = 1772

LANES = 128
VMEM_LIMIT = 48 * 1024 * 1024
MOE_VMEM_LIMIT = 56 * 1024 * 1024

MLA_IN_W = 512
LRU_IN_W = 512
S5_IN_W = 256
NQ_IN_W = NSA_HEADS * LANES
NKV_IN_W = 6 * LANES
IN_W = MLA_IN_W + LRU_IN_W + S5_IN_W + NQ_IN_W + NKV_IN_W


def _cparams(sem):
    return pltpu.CompilerParams(dimension_semantics=sem, vmem_limit_bytes=VMEM_LIMIT)


def _dot(a, b):
    return jnp.dot(a.astype(BF16), b.astype(BF16), preferred_element_type=F32)


def _dot_nt(a, b):
    return lax.dot_general(a.astype(BF16), b.astype(BF16), (((1,), (1,)), ((), ())),
                           preferred_element_type=F32)


def _split3(x):
    hi = x.astype(BF16)
    r = x - hi.astype(F32)
    mid = r.astype(BF16)
    lo = (r - mid.astype(F32)).astype(BF16)
    return hi, mid, lo


def _dot_f32_by_exact(x, w_bf16):
    hi, mid, lo = _split3(x)
    return (jnp.dot(hi, w_bf16, preferred_element_type=F32)
            + jnp.dot(mid, w_bf16, preferred_element_type=F32)
            + jnp.dot(lo, w_bf16, preferred_element_type=F32))


def _rms(x, g, n):
    return x * lax.rsqrt(jnp.sum(x * x, axis=-1, keepdims=True) * (1.0 / n) + EPS) * g


def _lane_sum(x):
    hi = x.astype(BF16)
    lo = (x - hi.astype(F32)).astype(BF16)
    ones = jnp.ones((x.shape[-1], LANES), BF16)
    return (jnp.dot(hi, ones, preferred_element_type=F32) + jnp.dot(lo, ones, preferred_element_type=F32))


def _rms_mxu(x, g, n):
    inv = lax.rsqrt(_lane_sum(x * x) * (1.0 / n) + EPS)
    if x.shape[-1] > LANES:
        inv = jnp.concatenate([inv] * (x.shape[-1] // LANES), axis=-1)
    return x * inv * g


def _gelu(x):
    return 0.5 * x * (1.0 + jnp.tanh(math.sqrt(2.0 / math.pi) * (x + 0.044715 * (x * x * x))))


def _rope(x, cos, sin_lo, sin_hi, half):
    return (x * cos + pltpu.roll(x, LANES - half, axis=1) * sin_lo
            + pltpu.roll(x, half, axis=1) * sin_hi)


def _with_ones(x, upper):
    if upper:
        x = pltpu.roll(x, LANES // 2, axis=1)
    lane = lax.broadcasted_iota(jnp.int32, x.shape, 1)
    return jnp.where(lane < LANES // 2, x, 1.0).astype(BF16)


NSA_TRIG_LANE0 = MLA_ROPE


def _trig_kernel(pos_ref, inv_ref, cos_ref, sin_ref):
    ang = pos_ref[0].astype(F32) * inv_ref[...]
    cos_ref[0] = jnp.cos(ang)
    sin_ref[0] = jnp.sin(ang)


def _trig(pos3, ts=512):
    def inv(rot):
        v = ROPE_THETA ** (-jnp.arange(rot // 2, dtype=F32) * 2.0 / rot)
        return jnp.concatenate([v, v])
    inv_l = _pad_last(jnp.concatenate([inv(MLA_ROPE), inv(NSA_ROT)]), LANES)[None, :]
    b, s, _ = pos3.shape
    spec = pl.BlockSpec((1, ts, LANES), lambda i, j: (i, j, 0))
    return pl.pallas_call(
        _trig_kernel,
        grid=(b, s // ts),
        in_specs=[pl.BlockSpec((1, ts, 1), lambda i, j: (i, j, 0)), _full((1, LANES))],
        out_specs=[spec, spec],
        out_shape=[jax.ShapeDtypeStruct((b, s, LANES), F32)] * 2,
        compiler_params=_cparams(("parallel", "parallel")),
        name="rope_trig",
    )(pos3, inv_l)


def _rope_tables(cs, sn, lane0, half):
    if lane0:
        cs = pltpu.roll(cs, LANES - lane0, axis=1)
        sn = pltpu.roll(sn, LANES - lane0, axis=1)
    lane = lax.broadcasted_iota(jnp.int32, cs.shape, 1)
    cos = jnp.where(lane < 2 * half, cs, 1.0)
    s_lo = jnp.where(lane < half, -sn, 0.0)
    s_hi = jnp.where((lane >= half) & (lane < 2 * half), sn, 0.0)
    return cos, s_lo, s_hi


def _pad_last(a, n):
    return jnp.pad(a, [(0, 0)] * (a.ndim - 1) + [(0, n - a.shape[-1])])


def _full(shape):
    nd = len(shape)
    return pl.BlockSpec(shape, lambda *_: (0,) * nd)


def _proj_prep_kernel(x_ref, g_ref, w_ref, cs_ref, sn_ref, gcq_ref, wuq_ref, gckv_ref, wk_ref, wv_ref,
                      gqm_ref, gkm_ref, gqn_ref, gkn_ref,
                      mq_out, mk_out, mv_out, lru_out, s5_out, nq_out, ks_out, kw_out, vs_out, vw_out,
                      cg_out):
    x = x_ref[0]
    h = x * lax.rsqrt(jnp.mean(x * x, axis=-1, keepdims=True) + EPS) * g_ref[...]
    y = jnp.dot(h.astype(BF16), w_ref[...], preferred_element_type=F32)
    cs, sn = cs_ref[0], sn_ref[0]
    c0 = 0
    _mla_prep(y[:, c0:c0 + MLA_IN_W], cs, sn, gcq_ref, wuq_ref, gckv_ref, wk_ref, wv_ref, gqm_ref, gkm_ref,
              mq_out, mk_out, mv_out)
    c0 += MLA_IN_W
    lru_out[0] = y[:, c0:c0 + LRU_IN_W]
    c0 += LRU_IN_W
    s5_out[0] = y[:, c0:c0 + S5_IN_W].astype(BF16)
    c0 += S5_IN_W
    nkv = y[:, c0 + NQ_IN_W:c0 + NQ_IN_W + NKV_IN_W]
    _nsa_prep(y[:, c0:c0 + NQ_IN_W], nkv, cs, sn, gqn_ref, gkn_ref, nq_out, ks_out, kw_out, vs_out, vw_out)
    for i, slot in enumerate((0, 3, 5)):
        cg_out[0, :, LANES * i:LANES * (i + 1)] = nkv[:, LANES * slot:LANES * (slot + 1)]


def _inproj_cols():
    src = -np.ones((IN_W,), np.int64)
    o = 0
    src[o:o + 192] = np.arange(0, 192)
    src[o + 256:o + 384] = np.arange(192, 320)
    src[o + 384:o + 416] = np.arange(320, 352)
    o += MLA_IN_W
    src[o:o + 512] = np.arange(352, 864)
    o += LRU_IN_W
    src[o:o + 256] = np.arange(864, 1120)
    o += S5_IN_W
    for h in range(NSA_HEADS):
        src[o + LANES * h:o + LANES * h + 64] = np.arange(1120 + 64 * h, 1120 + 64 * h + 64)
    o += NQ_IN_W
    kv0 = 1376
    src[o:o + 64] = np.arange(kv0, kv0 + 64)
    src[o + 128:o + 192] = np.arange(kv0 + 128, kv0 + 192)
    src[o + 256:o + 320] = np.arange(kv0 + 256, kv0 + 320)
    src[o + 384:o + 448] = np.arange(kv0 + 64, kv0 + 128)
    src[o + 448:o + 512] = np.arange(kv0 + 192, kv0 + 256)
    src[o + 512:o + 576] = np.arange(kv0 + 320, kv0 + 384)
    src[o + 640:o + 652] = np.arange(1760, 1772)
    return src


_INPROJ_SRC = _inproj_cols()


def _prep_w_in(w_in):
    idx = jnp.asarray(np.maximum(_INPROJ_SRC, 0), jnp.int32)
    keep = jnp.asarray(_INPROJ_SRC >= 0)
    return jnp.where(keep[None, :], jnp.take(w_in, idx, axis=1), 0.0).astype(BF16)


CG_W = 3 * LANES


def _proj_prep(x3, g, w_pad, trig, pm, pn, ts=1024):
    b, s, _ = x3.shape
    heads = lambda: pl.BlockSpec((1, MLA_HEADS, ts, LANES), lambda i, j: (i, 0, j, 0))
    rows = lambda w: pl.BlockSpec((1, ts, w), lambda i, j: (i, j, 0))
    hshape = jax.ShapeDtypeStruct((b, MLA_HEADS, s, LANES), BF16)
    tok = lambda w, dt: jax.ShapeDtypeStruct((b, s, w), dt)
    return pl.pallas_call(
        _proj_prep_kernel,
        grid=(b, s // ts),
        in_specs=[rows(D_MODEL), _full((1, D_MODEL)), _full((D_MODEL, IN_W)), rows(LANES), rows(LANES),
                  _full((1, 2 * LANES)), _full((2 * LANES, MLA_HEADS * LANES)), _full((1, MLA_KV_RANK)),
                  _full((MLA_KV_RANK, MLA_HEADS * LANES)), _full((MLA_KV_RANK, MLA_HEADS * MLA_V)),
                  _full((1, LANES)), _full((1, LANES)),
                  _full((1, LANES)), _full((3, LANES))],
        out_specs=[heads(), heads(), heads(), rows(LRU_IN_W), rows(S5_IN_W),
                   heads(), rows(LANES), rows(LANES), rows(LANES), rows(LANES), rows(CG_W)],
        out_shape=[hshape, hshape, hshape, tok(LRU_IN_W, F32), tok(S5_IN_W, BF16),
                   hshape, tok(LANES, BF16), tok(LANES, BF16), tok(LANES, BF16), tok(LANES, BF16),
                   tok(CG_W, F32)],
        compiler_params=_cparams(("parallel", "parallel")),
        name="proj_prep",
    )(x3, g[None, :], w_pad, trig[0], trig[1], pm["g_cq"], pm["w_uq"], pm["g_ckv"], pm["w_k"], pm["w_v"],
      pm["g_q"], pm["g_k"], pn["g_q"], pn["g_k"])


def _mla_prep(xin, cs, sn, gcq_ref, wuq_ref, gckv_ref, wk_ref, wv_ref, gq_ref, gk_ref, q_out, k_out, v_out):
    cq = _rms_mxu(xin[:, 0:256], gcq_ref[...], MLA_Q_RANK)
    ckv = _rms_mxu(xin[:, 256:384], gckv_ref[...], MLA_KV_RANK)
    kpe = xin[:, 384:512]
    q = _dot(cq, wuq_ref[...])
    kn = _dot(ckv, wk_ref[...])
    v = _dot(ckv, wv_ref[...])
    cos, s_lo, s_hi = _rope_tables(cs, sn, 0, MLA_ROPE // 2)
    scale = MLA_QK ** -0.5 * LOG2E
    for h in range(MLA_HEADS):
        qh = _rms_mxu(q[:, LANES * h:LANES * (h + 1)], gq_ref[...], MLA_QK)
        q_out[0, h] = (_rope(qh, cos, s_lo, s_hi, MLA_ROPE // 2) * scale).astype(BF16)
        kh = _rms_mxu(kn[:, LANES * h:LANES * (h + 1)] + kpe, gk_ref[...], MLA_QK)
        k_out[0, h] = _rope(kh, cos, s_lo, s_hi, MLA_ROPE // 2).astype(BF16)
        v_out[0, h] = _with_ones(v[:, LANES * (h // 2):LANES * (h // 2 + 1)], h % 2 == 1)


def _attend(q, segs, v1, s_scr, p_scr, m_scr, nh, chunk):
    rows = q.shape[0]
    c0 = 0
    m = None
    for k, bias in segs:
        n = k.shape[0]
        sb = _dot_nt(q, k)
        if bias is not None:
            sb = (sb.reshape(nh, rows // nh, n) + bias[None]).reshape(rows, n)
        s_scr[0:rows, c0:c0 + n] = sb
        mx = jnp.max(sb, axis=-1, keepdims=True)
        m = mx if m is None else jnp.maximum(m, mx)
        c0 += n
    m_scr[0:rows, :] = jnp.broadcast_to(m, (rows, LANES))
    for r0 in range(0, rows, chunk):
        mb = m_scr[r0:r0 + chunk, :]
        for j0 in range(0, c0, LANES):
            p = jnp.exp2(s_scr[r0:r0 + chunk, j0:j0 + LANES] - mb)
            p_scr[r0:r0 + chunk, j0:j0 + LANES] = p.astype(BF16)
    o = jnp.dot(p_scr[0:rows, 0:c0], v1, preferred_element_type=F32)
    return (o * (1.0 / pltpu.roll(o, LANES // 2, axis=1)))[:, 0:LANES // 2]


def _mla_attn_kernel(q_ref, k_ref, v_ref, o_ref, s_scr, p_scr, m_scr, o_scr, bias_scr, *, tq, chunk):
    qi = pl.program_id(1)
    row = lax.broadcasted_iota(jnp.int32, (tq, tq), 0)
    col = lax.broadcasted_iota(jnp.int32, (tq, tq), 1)
    bias_scr[...] = jnp.where(col <= row, 0.0, NEG)
    nset = s_scr.shape[0]
    for c in range(k_ref.shape[2] // tq):

        @pl.when(qi == c)
        def _(c=c):
            n = (c + 1) * tq

            def heads(i, carry):
                for j in range(nset):
                    h = i * nset + j
                    segs = [(k_ref[0, h, n - tq:n, :], bias_scr[...])]
                    if c > 0:
                        segs = [(k_ref[0, h, 0:n - tq, :], None)] + segs
                    o_scr[h] = _attend(q_ref[0, h], segs, v_ref[0, h, 0:n, :], s_scr.at[j], p_scr.at[j],
                                       m_scr.at[j], 1, chunk)
                return carry

            lax.fori_loop(0, MLA_HEADS // nset, heads, 0)

    for h in range(MLA_HEADS):
        o_ref[0, :, MLA_V * h:MLA_V * (h + 1)] = o_scr[h].astype(o_ref.dtype)


def _mla(q, k, v, tq=512, chunk=128, nset=4):
    b, _, s, _ = q.shape
    return pl.pallas_call(
        functools.partial(_mla_attn_kernel, tq=tq, chunk=chunk),
        grid=(b, s // tq),
        in_specs=[pl.BlockSpec((1, MLA_HEADS, tq, LANES), lambda i, j: (i, 0, j, 0)),
                  pl.BlockSpec((1, MLA_HEADS, s, LANES), lambda i, j: (i, 0, 0, 0)),
                  pl.BlockSpec((1, MLA_HEADS, s, LANES), lambda i, j: (i, 0, 0, 0))],
        out_specs=pl.BlockSpec((1, tq, GROUP_W), lambda i, j: (i, j, 0)),
        out_shape=jax.ShapeDtypeStruct((b, s, GROUP_W), BF16),
        scratch_shapes=[pltpu.VMEM((nset, tq, s), F32), pltpu.VMEM((nset, tq, s), BF16),
                        pltpu.VMEM((nset, tq, LANES), F32),
                        pltpu.VMEM((MLA_HEADS, tq, MLA_V), F32), pltpu.VMEM((tq, tq), F32)],
        compiler_params=_cparams(("parallel", "arbitrary")),
        name="mla_attn",
    )(q, k, v)


def _prep_mla(g_cq, g_ckv, w_uq, w_ukv, g_q, g_k):
    wq = w_uq.reshape(MLA_Q_RANK, MLA_HEADS, MLA_QK)
    wq = _pad_last(wq, LANES).reshape(MLA_Q_RANK, MLA_HEADS * LANES)
    wq = jnp.pad(wq, ((0, 256 - MLA_Q_RANK), (0, 0)))
    wkv = w_ukv.reshape(MLA_KV_RANK, MLA_HEADS, MLA_NOPE + MLA_V)
    wk = jnp.pad(wkv[:, :, :MLA_NOPE], ((0, 0), (0, 0), (MLA_ROPE, LANES - MLA_QK)))
    wk = wk.reshape(MLA_KV_RANK, MLA_HEADS * LANES)
    wv = wkv[:, :, MLA_NOPE:].reshape(MLA_KV_RANK, MLA_HEADS * MLA_V)
    return dict(g_cq=_pad_last(g_cq, 256)[None, :], g_ckv=g_ckv[None, :], w_uq=wq.astype(BF16),
                w_k=wk.astype(BF16), w_v=wv.astype(BF16), g_q=_pad_last(g_q, LANES)[None, :],
                g_k=_pad_last(g_k, LANES)[None, :])


def _shift_rows(x, k, row, fill):
    return jnp.where(row >= k, pltpu.roll(x, k, axis=0), fill)


def _lru_kernel(in_ref, cw_ref, cb_ref, wa_ref, ba_ref, wi_ref, bi_ref, lam_ref, o_ref, a_scr, b_scr, *,
                chunk):
    xin = in_ref[0]
    s = xin.shape[0]
    xb = xin[:, :LRU_W]
    row = lax.broadcasted_iota(jnp.int32, (s, LRU_W), 0)
    u = cb_ref[...] + xb * cw_ref[CONV_W - 1:CONV_W, :]
    for j in range(CONV_W - 1):
        u = u + _shift_rows(xb, CONV_W - 1 - j, row, 0.0) * cw_ref[j:j + 1, :]
    r = jax.nn.sigmoid(_dot(u, wa_ref[...]) + ba_ref[...])
    gi = jax.nn.sigmoid(_dot(u, wi_ref[...]) + bi_ref[...])
    nlam = -lam_ref[...]
    softplus = jnp.maximum(nlam, 0.0) + jnp.log1p(jnp.exp(-jnp.abs(nlam)))
    log_a = (-LRU_C) * r * softplus
    a = jnp.exp(log_a)
    y = jnp.maximum(-jnp.tanh(log_a) * (a * a + 1.0), 0.0)
    mult = jnp.where(y > 0.0, y * lax.rsqrt(y), 0.0)
    mult = jnp.where(row == 0, 1.0, mult)
    bt = mult * gi * u
    rin = jnp.bitwise_and(row, chunk - 1)
    k = 1
    while k < chunk:
        bt = a * _shift_rows(bt, k, rin, 0.0) + bt
        a = a * _shift_rows(a, k, rin, 1.0)
        k *= 2
    a_scr[...] = a
    b_scr[...] = bt

    def carry_chunk(j, h_prev):
        r0 = pl.multiple_of(j * chunk, chunk)
        h = b_scr[pl.ds(r0, chunk), :] + a_scr[pl.ds(r0, chunk), :] * h_prev
        o_ref[0, pl.ds(r0, chunk), :] = (h * _gelu(in_ref[0, pl.ds(r0, chunk), LRU_W:])).astype(o_ref.dtype)
        return h[chunk - 1:chunk, :]

    lax.fori_loop(0, s // chunk, carry_chunk, jnp.zeros((1, LRU_W), F32))


def _block_diag(w):
    n, i, j = w.shape
    eye = jnp.eye(n, dtype=w.dtype)
    return (eye[:, None, :, None] * w[:, :, None, :]).reshape(n * i, n * j)


def _lru(lru_in, p, chunk=64):
    b, s, _ = lru_in.shape
    return pl.pallas_call(
        functools.partial(_lru_kernel, chunk=chunk),
        grid=(b,),
        in_specs=[pl.BlockSpec((1, s, LRU_IN_W), lambda i: (i, 0, 0)),
                  _full((CONV_W, LRU_W)), _full((1, LRU_W)), _full((LRU_W, LRU_W)), _full((1, LRU_W)),
                  _full((LRU_W, LRU_W)), _full((1, LRU_W)), _full((1, LRU_W))],
        out_specs=pl.BlockSpec((1, s, LRU_W), lambda i: (i, 0, 0)),
        out_shape=jax.ShapeDtypeStruct((b, s, LRU_W), BF16),
        scratch_shapes=[pltpu.VMEM((s, LRU_W), F32), pltpu.VMEM((s, LRU_W), F32)],
        compiler_params=_cparams(("parallel",)),
        name="rglru",
    )(lru_in, p["cw"], p["cb"], p["wa"], p["ba"], p["wi"], p["bi"], p["lam"])


def _prep_lru(conv_w, conv_b, w_a, b_a, w_i, b_i, lam):
    return dict(cw=conv_w, cb=conv_b[None, :], wa=_block_diag(w_a).astype(BF16),
                ba=b_a.reshape(1, LRU_W), wi=_block_diag(w_i).astype(BF16), bi=b_i.reshape(1, LRU_W),
                lam=lam[None, :])


def _s5_disc_kernel(are_ref, aim_ref, ldt_ref, arer_ref, aimr_ref, bre_ref, bim_ref,
                    abre_ref, abim_ref, bbre_ref, bbim_ref):
    dt = jnp.exp(ldt_ref[...])

    def disc(a_re, a_im):
        mag = jnp.exp(dt * a_re)
        ab_re = mag * jnp.cos(dt * a_im)
        ab_im = mag * jnp.sin(dt * a_im)
        den = a_re * a_re + a_im * a_im
        n_re = ab_re - 1.0
        g_re = (n_re * a_re + ab_im * a_im) / den
        g_im = (ab_im * a_re - n_re * a_im) / den
        return ab_re, ab_im, g_re, g_im

    ab_re, ab_im, _, _ = disc(are_ref[...], aim_ref[...])
    abre_ref[...] = ab_re
    abim_ref[...] = ab_im
    _, _, g_re, g_im = disc(arer_ref[...], aimr_ref[...])
    bbre_ref[...] = g_re * bre_ref[...] - g_im * bim_ref[...]
    bbim_ref[...] = g_re * bim_ref[...] + g_im * bre_ref[...]


def _s5_kernel(u_ref, are_ref, aim_ref, bre_ref, bim_ref, cre_ref, cim_ref, d_ref, wg_ref, bg_ref,
               o_ref, hre, him, st_re, st_im, *, tc, nb, cw):
    @pl.when(pl.program_id(0) == 0)
    def _():
        st_re[...] = jnp.zeros_like(st_re)
        st_im[...] = jnp.zeros_like(st_im)

    ub = u_ref[...].reshape(tc * nb, S5_W)
    u = ub.astype(F32)
    hre[...] = jnp.dot(ub, bre_ref[...], preferred_element_type=F32)
    him[...] = jnp.dot(ub, bim_ref[...], preferred_element_type=F32)
    for c in range(S5_STATE // cw):
        cs = slice(c * cw, (c + 1) * cw)
        ar = jnp.broadcast_to(are_ref[:, cs], (nb, cw))
        ai = jnp.broadcast_to(aim_ref[:, cs], (nb, cw))

        def body(t, carry, cs=cs, ar=ar, ai=ai):
            hr, hi = carry
            r0 = pl.multiple_of(t * nb, nb)
            nr = ar * hr - ai * hi + hre[pl.ds(r0, nb), cs]
            ni = ar * hi + ai * hr + him[pl.ds(r0, nb), cs]
            hre[pl.ds(r0, nb), cs] = nr
            him[pl.ds(r0, nb), cs] = ni
            return nr, ni

        hr, hi = lax.fori_loop(0, tc, body, (st_re[:, cs], st_im[:, cs]), unroll=4)
        st_re[:, cs] = hr
        st_im[:, cs] = hi
    y = (jnp.dot(hre[...].astype(BF16), cre_ref[...], preferred_element_type=F32)
         - jnp.dot(him[...].astype(BF16), cim_ref[...], preferred_element_type=F32))
    y = _gelu(y + d_ref[...] * u)
    z = _dot(y, wg_ref[...]) + bg_ref[...]
    o_ref[...] = (y * jax.nn.sigmoid(z)).astype(o_ref.dtype).reshape(tc, nb, S5_W)


def _prep_s5(a_re, a_im, log_dt, b_re, b_im, c_re, c_im, d, w_glu, b_glu):
    g, p, ch = S5_GROUPS, S5_P, S5_CH
    ab_re, ab_im, bb_re, bb_im = pl.pallas_call(
        _s5_disc_kernel,
        out_shape=[jax.ShapeDtypeStruct((g, p), F32), jax.ShapeDtypeStruct((g, p), F32),
                   jax.ShapeDtypeStruct((g, p * ch), F32), jax.ShapeDtypeStruct((g, p * ch), F32)],
        name="s5_discretize",
    )(a_re, a_im, log_dt[:, None], jnp.repeat(a_re, ch, axis=1), jnp.repeat(a_im, ch, axis=1),
      b_re.reshape(g, p * ch), b_im.reshape(g, p * ch))

    def b_dense(bb):
        return _block_diag(bb.reshape(g, p, ch).transpose(0, 2, 1)).astype(BF16)

    def c_dense(c):
        return _block_diag(c.transpose(0, 2, 1)).astype(BF16)

    return dict(a_re=ab_re.reshape(1, S5_STATE), a_im=ab_im.reshape(1, S5_STATE),
                b_re=b_dense(bb_re), b_im=b_dense(bb_im), c_re=c_dense(c_re), c_im=c_dense(c_im),
                d=d[None, :], w_glu=w_glu.astype(BF16), b_glu=b_glu[None, :])


def _s5(u_tm, p, tc=128, cw=512):
    s, nb, _ = u_tm.shape
    return pl.pallas_call(
        functools.partial(_s5_kernel, tc=tc, nb=nb, cw=cw),
        grid=(s // tc,),
        in_specs=[pl.BlockSpec((tc, nb, S5_W), lambda i: (i, 0, 0)),
                  _full((1, S5_STATE)), _full((1, S5_STATE)),
                  _full((S5_W, S5_STATE)), _full((S5_W, S5_STATE)),
                  _full((S5_STATE, S5_W)), _full((S5_STATE, S5_W)),
                  _full((1, S5_W)), _full((S5_W, S5_W)), _full((1, S5_W))],
        out_specs=pl.BlockSpec((tc, nb, S5_W), lambda i: (i, 0, 0)),
        out_shape=jax.ShapeDtypeStruct((s, nb, S5_W), BF16),
        scratch_shapes=[pltpu.VMEM((tc * nb, S5_STATE), F32), pltpu.VMEM((tc * nb, S5_STATE), F32),
                        pltpu.VMEM((nb, S5_STATE), F32), pltpu.VMEM((nb, S5_STATE), F32)],
        compiler_params=_cparams(("arbitrary",)),
        name="s5",
    )(u_tm, p["a_re"], p["a_im"], p["b_re"], p["b_im"], p["c_re"], p["c_im"], p["d"], p["w_glu"],
      p["b_glu"])


def _nsa_prep(nq, nkv, cs, sn, gq_ref, gk_ref, q_out, ks_out, kw_out, vs_out, vw_out):
    half = NSA_ROT // 2
    cos, s_lo, s_hi = _rope_tables(cs, sn, NSA_TRIG_LANE0, half)
    scale = NSA_DK ** -0.5 * LOG2E
    for h in range(NSA_HEADS):
        qh = _rms_mxu(nq[:, LANES * h:LANES * (h + 1)], gq_ref[...], NSA_DK)
        q_out[0, h] = (_rope(qh, cos, s_lo, s_hi, half) * scale).astype(BF16)
    ks = _rms_mxu(nkv[:, 128:256], gk_ref[1:2, :], NSA_DK)
    ks_out[0] = _rope(ks, cos, s_lo, s_hi, half).astype(BF16)
    kw = _rms_mxu(nkv[:, 256:384], gk_ref[2:3, :], NSA_DK)
    kw_out[0] = _rope(kw, cos, s_lo, s_hi, half).astype(BF16)
    vs_out[0] = _with_ones(nkv[:, 384:512], True)
    vw_out[0] = _with_ones(nkv[:, 512:640], False)


def _nsa_cmp_kernel(k_in, v_in, cs_ref, sn_ref, pek_ref, w1k_ref, w2k_ref, pev_ref, w1v_ref, w2v_ref,
                    g_ref, kc_out, vc_out):
    nc = kc_out.shape[1]

    def compress(x_ref, pe_ref, w1_ref, w2_ref):
        lo = hi = None
        for j in range(CMP_STRIDE):
            xj = x_ref[0, pl.ds(j, nc, stride=CMP_STRIDE), :]
            dl = _dot(xj + pe_ref[j:j + 1, :], w1_ref[j])
            dh = _dot(xj + pe_ref[CMP_STRIDE + j:CMP_STRIDE + j + 1, :], w1_ref[CMP_STRIDE + j])
            lo = dl if lo is None else lo + dl
            hi = dh if hi is None else hi + dh
        hid = lo + pltpu.roll(hi, nc - 1, axis=0)
        return _dot(_gelu(hid), w2_ref[...])

    kc = compress(k_in, pek_ref, w1k_ref, w2k_ref)
    cos, s_lo, s_hi = _rope_tables(cs_ref[0], sn_ref[0], NSA_TRIG_LANE0, NSA_ROT // 2)
    kc = _rope(_rms(kc, g_ref[0:1, :], NSA_DK), cos, s_lo, s_hi, NSA_ROT // 2)
    kc_out[0] = kc.astype(BF16)
    vc_out[0] = compress(v_in, pev_ref, w1v_ref, w2v_ref).astype(BF16)


def _nsa_cmpsel_kernel(q_ref, kc_ref, vc_ref, ov_ref, ocmp_ref, sel_ref, *, tc):
    qi = pl.program_id(1)
    nh = NSA_HEADS
    q = q_ref[0].reshape(nh * tc, LANES)
    lane = lax.broadcasted_iota(jnp.int32, (tc, LANES), 1)
    qpos = qi * tc + lax.broadcasted_iota(jnp.int32, (tc, LANES), 0)

    s = _dot_nt(q, kc_ref[0]).reshape(nh, tc, LANES)
    valid = (lane * CMP_STRIDE + (CMP_LEN - 1)) <= qpos
    s = jnp.where(valid, s, NEG)
    e = jnp.exp2(s - jnp.max(s, axis=-1, keepdims=True))
    p_c = jnp.where(valid, e / jnp.sum(e, axis=-1, keepdims=True), 0.0)
    ocmp_ref[0] = jnp.dot(p_c.astype(BF16).reshape(nh * tc, LANES), vc_ref[0],
                          preferred_element_type=F32).reshape(nh, tc, NSA_DK)

    imp = _dot_f32_by_exact(jnp.sum(p_c, axis=0), ov_ref[...])
    cur = qpos // SEL_LEN
    lane_f = lane.astype(F32)
    sel = (lane == 0) | (lane == cur) | (lane == cur - 1)
    cand = (lane < cur - 1) & (lane > 0)
    for _ in range(SEL_TOPK - 3):
        sc = jnp.where(cand, imp, -jnp.inf)
        best = jnp.max(sc, axis=-1, keepdims=True)
        pick = jnp.min(jnp.where(cand & (sc == best), lane_f, float(LANES)), axis=-1, keepdims=True)
        hit = lane_f == pick
        sel = sel | hit
        cand = cand & jnp.logical_not(hit)
    sel_ref[0] = jnp.where(sel, 1.0, 0.0).astype(BF16)


def _nsa_attn_kernel(q_ref, sel_ref, ocmp_ref, ks_ref, vs_ref, kw_ref, vw_ref, gate_ref, ex_ref,
                     o_ref, s_scr, p_scr, m_scr, obr_scr, *, tq, chunk, kchunk):
    qi = pl.program_id(1)
    nh = NSA_HEADS
    rows = nh * tq
    s_len = ks_ref.shape[1]

    def window():
        wk = WIN + tq
        start = pl.multiple_of(jnp.maximum(qi - WIN // tq, 0) * tq, tq)
        kpos = start + lax.broadcasted_iota(jnp.int32, (tq, wk), 1)
        qrow = qi * tq + lax.broadcasted_iota(jnp.int32, (tq, wk), 0)
        bias = jnp.where((kpos <= qrow) & (qrow - kpos < WIN), 0.0, NEG)
        obr_scr[1] = _attend(q_ref[0].reshape(rows, LANES), [(kw_ref[0, pl.ds(start, wk), :], bias)],
                             vw_ref[0, pl.ds(start, wk), :], s_scr.at[1], p_scr.at[1], m_scr.at[1],
                             nh, chunk)

    for c in range(s_len // kchunk):

        @pl.when(qi // (kchunk // tq) == c)
        def _(c=c):
            n = (c + 1) * kchunk
            em = jnp.dot(sel_ref[0], ex_ref[:, 0:n], preferred_element_type=F32)
            kpos = lax.broadcasted_iota(jnp.int32, (tq, n), 1)
            qrow = qi * tq + lax.broadcasted_iota(jnp.int32, (tq, n), 0)
            bias = jnp.where((em > 0.5) & (kpos <= qrow), 0.0, NEG)
            obr_scr[0] = _attend(q_ref[0].reshape(rows, LANES), [(ks_ref[0, 0:n, :], bias)],
                                 vs_ref[0, 0:n, :], s_scr.at[0], p_scr.at[0], m_scr.at[0], nh, chunk)
            window()

    g = jax.nn.sigmoid(gate_ref[0])
    for h in range(nh):
        r = slice(h * tq, (h + 1) * tq)
        o = (g[:, 3 * h:3 * h + 1] * ocmp_ref[0, h] + g[:, 3 * h + 1:3 * h + 2] * obr_scr[0, r, :]
             + g[:, 3 * h + 2:3 * h + 3] * obr_scr[1, r, :])
        o_ref[0, :, NSA_DK * h:NSA_DK * (h + 1)] = o.astype(o_ref.dtype)


def _nsa_tables(s, tq):
    nc = s // CMP_STRIDE
    nsb = s // SEL_LEN
    cs = np.arange(nc) * CMP_STRIDE
    ss = np.arange(nsb) * SEL_LEN
    ov = np.clip(np.minimum(cs[:, None] + CMP_LEN, ss[None, :] + SEL_LEN)
                 - np.maximum(cs[:, None], ss[None, :]), 0, None) / CMP_STRIDE
    ov[(s - CMP_LEN) // CMP_STRIDE + 1:] = 0.0
    ov_pad = np.zeros((nc, LANES), np.float32)
    ov_pad[:, :nsb] = ov
    ex = np.zeros((LANES, s), np.float32)
    ex[np.arange(s) // SEL_LEN, np.arange(s)] = 1.0
    return jnp.asarray(ov_pad, BF16), jnp.asarray(ex, BF16)


def _nsa(q, ks, kw, vs, vw, cg, trig, p, tq=256, tc=2048, chunk=128, kchunk=256):
    b, s, _ = ks.shape
    nc = s // CMP_STRIDE
    assert nc == LANES and s // SEL_LEN <= LANES and s >= WIN + tq and s % kchunk == 0

    last = np.minimum(np.arange(nc) * CMP_STRIDE + CMP_LEN - 1, s - 1)
    cspec = pl.BlockSpec((1, nc, LANES), lambda i: (i, 0, 0))
    kc, vc = pl.pallas_call(
        _nsa_cmp_kernel,
        grid=(b,),
        in_specs=[pl.BlockSpec((1, s, LANES), lambda i: (i, 0, 0)),
                  pl.BlockSpec((1, s, LANES), lambda i: (i, 0, 1)),
                  cspec, cspec,
                  _full((CMP_LEN, LANES)), _full((CMP_LEN, LANES, CMP_HID)), _full((CMP_HID, LANES)),
                  _full((CMP_LEN, LANES)), _full((CMP_LEN, LANES, CMP_HID)), _full((CMP_HID, NSA_DK)),
                  _full((3, LANES))],
        out_specs=[cspec, pl.BlockSpec((1, nc, NSA_DK), lambda i: (i, 0, 0))],
        out_shape=[jax.ShapeDtypeStruct((b, nc, LANES), BF16),
                   jax.ShapeDtypeStruct((b, nc, NSA_DK), BF16)],
        compiler_params=_cparams(("parallel",)),
        name="nsa_compress",
    )(cg, cg, trig[0][:, last, :], trig[1][:, last, :], p["pe_k"], p["w1_k"], p["w2_k"], p["pe_v"],
      p["w1_v"], p["w2_v"], p["g_k"])

    ov, ex = _nsa_tables(s, tq)
    o_cmp, sel = pl.pallas_call(
        functools.partial(_nsa_cmpsel_kernel, tc=tc),
        grid=(b, s // tc),
        in_specs=[pl.BlockSpec((1, NSA_HEADS, tc, LANES), lambda i, j: (i, 0, j, 0)),
                  pl.BlockSpec((1, nc, LANES), lambda i, j: (i, 0, 0)),
                  pl.BlockSpec((1, nc, NSA_DK), lambda i, j: (i, 0, 0)),
                  _full((nc, LANES))],
        out_specs=[pl.BlockSpec((1, NSA_HEADS, tc, NSA_DK), lambda i, j: (i, 0, j, 0)),
                   pl.BlockSpec((1, tc, LANES), lambda i, j: (i, j, 0))],
        out_shape=[jax.ShapeDtypeStruct((b, NSA_HEADS, s, NSA_DK), F32),
                   jax.ShapeDtypeStruct((b, s, LANES), BF16)],
        compiler_params=_cparams(("parallel", "parallel")),
        name="nsa_cmpsel",
    )(q, kc, vc, ov)

    rows = NSA_HEADS * tq
    kvspec = pl.BlockSpec((1, s, LANES), lambda i, j: (i, 0, 0))
    return pl.pallas_call(
        functools.partial(_nsa_attn_kernel, tq=tq, chunk=chunk, kchunk=kchunk),
        grid=(b, s // tq),
        in_specs=[pl.BlockSpec((1, NSA_HEADS, tq, LANES), lambda i, j: (i, 0, j, 0)),
                  pl.BlockSpec((1, tq, LANES), lambda i, j: (i, j, 0)),
                  pl.BlockSpec((1, NSA_HEADS, tq, NSA_DK), lambda i, j: (i, 0, j, 0)),
                  kvspec, kvspec, kvspec, kvspec,
                  pl.BlockSpec((1, tq, LANES), lambda i, j: (i, j, CG_W // LANES - 1)),
                  _full((LANES, s))],
        out_specs=pl.BlockSpec((1, tq, GROUP_W), lambda i, j: (i, j, 0)),
        out_shape=jax.ShapeDtypeStruct((b, s, GROUP_W), BF16),
        scratch_shapes=[pltpu.VMEM((2, rows, s), F32), pltpu.VMEM((2, rows, s), BF16),
                        pltpu.VMEM((2, rows, LANES), F32), pltpu.VMEM((2, rows, NSA_DK), F32)],
        compiler_params=_cparams(("parallel", "arbitrary")),
        name="nsa_attn",
    )(q, sel, o_cmp, ks, vs, kw, vw, cg, ex)


def _prep_nsa(g_q, g_k, pe_k, w1_k, w2_k, pe_v, w1_v, w2_v):
    def per_token(w1):
        w = w1.reshape(CMP_LEN, NSA_DK, CMP_HID)
        return jnp.pad(w, ((0, 0), (0, LANES - NSA_DK), (0, 0))).astype(BF16)

    return dict(g_q=_pad_last(g_q, LANES)[None, :], g_k=_pad_last(g_k, LANES),
                pe_k=_pad_last(pe_k, LANES), w1_k=per_token(w1_k),
                w2_k=_pad_last(w2_k, LANES).astype(BF16),
                pe_v=_pad_last(pe_v, LANES), w1_v=per_token(w1_v), w2_v=w2_v.astype(BF16))


MOE_BLOCK = 144
MOE_ALIGN = 16


def _route_t(lt):
    row = lambda r: lt[r:r + 1, :]
    lg = [row(i) for i in range(N_GROUPS)]
    gmax = functools.reduce(jnp.maximum, lg)
    pg_top = 1.0 / functools.reduce(lambda a, b: a + b, [jnp.exp(v - gmax) for v in lg])
    taken = jnp.zeros_like(gmax) > 1.0
    oh = []
    for v in lg:
        hit = (v == gmax) & jnp.logical_not(taken)
        oh.append(hit)
        taken = taken | hit
    le = []
    for k in range(EXP_PER_GROUP):
        v = row(N_GROUPS + k)
        for i in range(1, N_GROUPS):
            v = jnp.where(oh[i], row(N_GROUPS + EXP_PER_GROUP * i + k), v)
        le.append(v)
    m1 = functools.reduce(jnp.maximum, le)
    taken = jnp.zeros_like(m1) > 1.0
    first = []
    for v in le:
        hit = (v == m1) & jnp.logical_not(taken)
        first.append(hit)
        taken = taken | hit
    le2 = [jnp.where(f, -jnp.inf, v) for f, v in zip(first, le)]
    m2 = functools.reduce(jnp.maximum, le2)
    taken = jnp.zeros_like(m1) > 1.0
    second = []
    for v in le2:
        hit = (v == m2) & jnp.logical_not(taken)
        second.append(hit)
        taken = taken | hit
    v2 = jnp.exp(m2 - m1)
    w1 = pg_top / (1.0 + v2)
    w2 = pg_top * v2 / (1.0 + v2)
    comb = []
    for i in range(N_GROUPS):
        for k in range(EXP_PER_GROUP):
            w = jnp.where(first[k], w1, 0.0) + jnp.where(second[k], w2, 0.0)
            comb.append(jnp.where(oh[i], w, 0.0))
    return [jnp.where(o, 1.0, 0.0) for o in oh], comb


def _moe_kernel(ya_ref, yb_ref, yc_ref, yd_ref, x_ref, gout_ref, wout_ref, g_ref, wr_ref,
                br_ref, tri_ref, wg_ref, wu_ref, wd_ref, o_ref, hs, cs, ys, tok, meta):
    grp = pl.program_id(1)
    tm = x_ref.shape[0]
    rs = hs.shape[0]

    @pl.when(grp == 0)
    def _():
        x = x_ref[...]
        for i, y_ref in enumerate((ya_ref, yb_ref, yc_ref, yd_ref)):
            y = _rms(y_ref[...].astype(F32), gout_ref[i:i + 1, :], GROUP_W)
            x = x + _dot(y, wout_ref[GROUP_W * i:GROUP_W * (i + 1), :])
        o_ref[...] = x
        h = x * lax.rsqrt(jnp.mean(x * x, axis=-1, keepdims=True) + EPS) * g_ref[...]
        h_hi = h.astype(BF16)
        h_lo = (h - h_hi.astype(F32)).astype(BF16)
        lhl = jnp.dot(h_hi, wr_ref[...], preferred_element_type=F32)
        logits = (lhl[:, 0:LANES] + lhl[:, LANES:2 * LANES]
                  + jnp.dot(h_lo, wr_ref[:, 0:LANES], preferred_element_type=F32)) + br_ref[...]
        oh, comb = _route_t(logits.T)

        oh8 = jnp.concatenate(oh + [jnp.zeros((8 - N_GROUPS, tm), F32)], axis=0)
        cum = jnp.dot(oh8.astype(BF16), tri_ref[...], preferred_element_type=F32)
        cnt = jnp.sum(oh8, axis=1, keepdims=True)
        padded = jnp.floor((cnt + (MOE_BLOCK - 1)) * (1.0 / MOE_BLOCK)) * MOE_BLOCK
        starts, acc0 = [], jnp.zeros((1, 1), F32)
        for i in range(N_GROUPS):
            starts.append(acc0)
            acc0 = acc0 + padded[i:i + 1, :]
            meta[i] = starts[i][0, 0].astype(jnp.int32)
            meta[N_GROUPS + i] = (padded[i:i + 1, :][0, 0] * (1.0 / MOE_BLOCK)).astype(jnp.int32)
        dest = functools.reduce(lambda a, b: a + b,
                                [oh[i] * (starts[i] + cum[i:i + 1, :] - 1.0) for i in range(N_GROUPS)])
        perm = jnp.where(lax.broadcasted_iota(jnp.int32, (rs, tm), 0).astype(F32) == dest, 1.0, 0.0)
        perm = perm.astype(BF16)

        comb4 = [functools.reduce(lambda a, b: a + b, comb[k::EXP_PER_GROUP]) for k in range(EXP_PER_GROUP)]
        stack = jnp.concatenate(comb4 + [dest] + [jnp.zeros((LANES - EXP_PER_GROUP - 1, tm), F32)], axis=0)
        stack_t = stack.T
        tok[...] = stack_t
        c_hi = stack_t.astype(BF16)
        c_lo = (stack_t - c_hi.astype(F32)).astype(BF16)
        hs[...] = jnp.dot(perm, h_hi, preferred_element_type=F32).astype(BF16)
        cs[...] = jnp.dot(perm, jnp.concatenate([c_hi, c_lo], axis=1), preferred_element_type=F32)
        ys[...] = jnp.zeros_like(ys)

    def experts(r0, nrows):
        hb = hs[pl.ds(r0, nrows), :]
        cb = cs[pl.ds(r0, nrows), :]
        cb = cb[:, 0:LANES] + cb[:, LANES:2 * LANES]
        acc = None
        for k in range(EXP_PER_GROUP):
            a = jax.nn.silu(jnp.dot(hb, wg_ref[k], preferred_element_type=F32)) \
                * jnp.dot(hb, wu_ref[k], preferred_element_type=F32)
            d = jnp.dot((a * cb[:, k:k + 1]).astype(BF16), wd_ref[k], preferred_element_type=F32)
            acc = d if acc is None else acc + d
        ys[pl.ds(r0, nrows), :] = acc.astype(BF16)

    base = meta[grp]
    nblk = meta[N_GROUPS + grp]

    def pair(j, carry):
        experts(pl.multiple_of(base + j * (2 * MOE_BLOCK), MOE_ALIGN), 2 * MOE_BLOCK)
        return carry

    lax.fori_loop(0, nblk // 2, pair, 0)

    @pl.when(nblk % 2 == 1)
    def _():
        experts(pl.multiple_of(base + (nblk - 1) * MOE_BLOCK, MOE_ALIGN), MOE_BLOCK)

    @pl.when(grp == N_GROUPS - 1)
    def _():
        dest_t = tok[:, EXP_PER_GROUP:EXP_PER_GROUP + 1]
        unperm = jnp.where(lax.broadcasted_iota(jnp.int32, (tm, rs), 1).astype(F32) == dest_t, 1.0, 0.0)
        o_ref[...] += jnp.dot(unperm.astype(BF16), ys[...], preferred_element_type=F32)


def _sorted_rows(tm):
    low = -tm % MOE_BLOCK
    pad = low + MOE_BLOCK * ((N_GROUPS * (MOE_BLOCK - 1) - low) // MOE_BLOCK)
    return -(-(tm + pad) // LANES) * LANES


def _outproj_moe(ys, x2, g_out, w_out, p, experts, layer, tm=1024):
    t = x2.shape[0]
    rs = _sorted_rows(tm)
    yspec = pl.BlockSpec((tm, GROUP_W), lambda i, g: (i, 0))
    tri = jnp.asarray(np.triu(np.ones((tm, tm), np.float32)), BF16)

    def const(shape):
        nd = len(shape)
        return pl.BlockSpec(shape, lambda i, g: (0,) * nd)

    def resident(shape):
        nd = len(shape)
        return pl.BlockSpec(shape, lambda i, g: (0,) * nd, pipeline_mode=pl.Buffered(1))

    def of_group(shape):
        return pl.BlockSpec((None, EXP_PER_GROUP) + shape, lambda i, g: (layer, g, 0, 0))

    return pl.pallas_call(
        _moe_kernel,
        grid=(t // tm, N_GROUPS),
        in_specs=[yspec, yspec, yspec, yspec, pl.BlockSpec((tm, D_MODEL), lambda i, g: (i, 0)),
                  const((4, GROUP_W)), resident((D_MODEL, D_MODEL)),
                  const((1, D_MODEL)), const((D_MODEL, 2 * LANES)),
                  const((1, LANES)), resident((tm, tm)),
                  of_group((D_MODEL, D_EXPERT)), of_group((D_MODEL, D_EXPERT)),
                  of_group((D_EXPERT, D_MODEL))],
        out_specs=pl.BlockSpec((tm, D_MODEL), lambda i, g: (i, 0)),
        out_shape=jax.ShapeDtypeStruct((t, D_MODEL), F32),
        scratch_shapes=[pltpu.VMEM((rs, D_MODEL), BF16), pltpu.VMEM((rs, 2 * LANES), F32),
                        pltpu.VMEM((rs, D_MODEL), BF16), pltpu.VMEM((tm, LANES), F32),
                        pltpu.SMEM((2 * N_GROUPS,), jnp.int32)],
        compiler_params=pltpu.CompilerParams(dimension_semantics=("parallel", "arbitrary"),
                                             vmem_limit_bytes=MOE_VMEM_LIMIT),
        name="outproj_moe",
    )(*ys, x2, g_out, w_out, p["g"], p["wr"], p["br"], tri, *experts)


def _prep_moe(g, w_rg, b_rg, w_re, b_re):
    wr = _pad_last(jnp.concatenate([w_rg, w_re], axis=1), LANES)
    wr_hi = wr.astype(BF16)
    wr_lo = (wr - wr_hi.astype(F32)).astype(BF16)
    br = _pad_last(jnp.concatenate([b_rg, b_re]), LANES)[None, :]
    return dict(g=g[None, :], wr=jnp.concatenate([wr_hi, wr_lo], axis=1), br=br)


def kernel(x, positions, mix_norm, w_in, mla_g_cq, mla_g_ckv, mla_w_uq, mla_w_ukv, mla_g_q, mla_g_k, lru_conv_w, lru_conv_b, lru_w_a, lru_b_a, lru_w_i, lru_b_i, lru_lambda, s5_a_re, s5_a_im, s5_log_dt, s5_b_re, s5_b_im, s5_c_re, s5_c_im, s5_d, s5_w_glu, s5_b_glu, nsa_g_q, nsa_g_k, nsa_pe_k, nsa_w1_k, nsa_w2_k, nsa_pe_v, nsa_w1_v, nsa_w2_v, out_norm, w_out, ffn_norm, moe_w_rg, moe_b_rg, moe_w_re, moe_b_re, moe_w_gate, moe_w_up, moe_w_down):
    b, s, d = x.shape
    t = b * s
    trig = _trig(positions.astype(jnp.int32)[:, :, None])
    x2 = x.reshape(t, d)
    experts = (moe_w_gate.astype(BF16), moe_w_up.astype(BF16), moe_w_down.astype(BF16))
    for l in range(w_in.shape[0]):
        pn = _prep_nsa(nsa_g_q[l], nsa_g_k[l], nsa_pe_k[l], nsa_w1_k[l], nsa_w2_k[l], nsa_pe_v[l],
                       nsa_w1_v[l], nsa_w2_v[l])
        mq, mk, mv, o_lru, o_s5, nq, ks, kw, vs, vw, cg = _proj_prep(
            x2.reshape(b, s, d), mix_norm[l], _prep_w_in(w_in[l]), trig,
            _prep_mla(mla_g_cq[l], mla_g_ckv[l], mla_w_uq[l], mla_w_ukv[l], mla_g_q[l], mla_g_k[l]), pn)
        y_a = _mla(mq, mk, mv)
        y_b = _lru(o_lru, _prep_lru(lru_conv_w[l], lru_conv_b[l], lru_w_a[l], lru_b_a[l], lru_w_i[l],
                                    lru_b_i[l], lru_lambda[l]))
        y_c = _s5(o_s5.transpose(1, 0, 2),
                  _prep_s5(s5_a_re[l], s5_a_im[l], s5_log_dt[l], s5_b_re[l], s5_b_im[l], s5_c_re[l],
                           s5_c_im[l], s5_d[l], s5_w_glu[l], s5_b_glu[l])).transpose(1, 0, 2)
        y_d = _nsa(nq, ks, kw, vs, vw, cg, trig, pn)
        ys = [y.reshape(t, GROUP_W) for y in (y_a, y_b, y_c, y_d)]
        x2 = _outproj_moe(ys, x2, out_norm[l], w_out[l].astype(BF16),
                          _prep_moe(ffn_norm[l], moe_w_rg[l], moe_b_rg[l], moe_w_re[l], moe_b_re[l]),
                          experts, l)
    return x2.reshape(b, s, d)
```

```python
import functools
import math

import numpy as np
import jax
import jax.numpy as jnp
from jax import lax
from jax.experimental import pallas as pl
from jax.experimental.pallas import tpu as pltpu

F32 = jnp.float32
BF16 = jnp.bfloat16

D_MODEL = 1024
DEPTH = 2
GROUP_W = 256
EPS = 1e-6
ROPE_THETA = 500000.0
NEG = -1e30
LOG2E = math.log2(math.e)

MLA_HEADS = 4
MLA_ROPE = 32
MLA_NOPE = 64
MLA_V = 64
MLA_QK = 96
MLA_Q_RANK = 192
MLA_KV_RANK = 128

LRU_W = 256
LRU_BLOCKS = 4
LRU_BW = 64
CONV_W = 4
LRU_C = 8.0

S5_W = 256
S5_CH = 16
S5_GROUPS = 16
S5_P = 64
S5_STATE = S5_GROUPS * S5_P

NSA_HEADS = 4
NSA_DK = 64
NSA_ROT = 16
CMP_LEN = 32
CMP_STRIDE = 16
CMP_HID = 128
SEL_LEN = 64
SEL_TOPK = 5
WIN = 512

N_GROUPS = 4
EXP_PER_GROUP = 4
N_EXPERTS = 16
D_EXPERT = 256

D_IN = 1772

LANES = 128
VMEM_LIMIT = 48 * 1024 * 1024
MOE_VMEM_LIMIT = 56 * 1024 * 1024

MLA_IN_W = 512
LRU_IN_W = 512
S5_IN_W = 256
NQ_IN_W = NSA_HEADS * LANES
NKV_IN_W = 6 * LANES
IN_W = MLA_IN_W + LRU_IN_W + S5_IN_W + NQ_IN_W + NKV_IN_W


def _cparams(sem):
    return pltpu.CompilerParams(dimension_semantics=sem, vmem_limit_bytes=VMEM_LIMIT)


def _dot(a, b):
    return jnp.dot(a.astype(BF16), b.astype(BF16), preferred_element_type=F32)


def _dot_nt(a, b):
    return lax.dot_general(a.astype(BF16), b.astype(BF16), (((1,), (1,)), ((), ())),
                           preferred_element_type=F32)


def _split3(x):
    hi = x.astype(BF16)
    r = x - hi.astype(F32)
    mid = r.astype(BF16)
    lo = (r - mid.astype(F32)).astype(BF16)
    return hi, mid, lo


def _dot_f32_by_exact(x, w_bf16):
    hi, mid, lo = _split3(x)
    return (jnp.dot(hi, w_bf16, preferred_element_type=F32)
            + jnp.dot(mid, w_bf16, preferred_element_type=F32)
            + jnp.dot(lo, w_bf16, preferred_element_type=F32))


def _rms(x, g, n):
    return x * lax.rsqrt(jnp.sum(x * x, axis=-1, keepdims=True) * (1.0 / n) + EPS) * g


def _lane_sum(x):
    hi = x.astype(BF16)
    lo = (x - hi.astype(F32)).astype(BF16)
    ones = jnp.ones((x.shape[-1], LANES), BF16)
    return (jnp.dot(hi, ones, preferred_element_type=F32) + jnp.dot(lo, ones, preferred_element_type=F32))


def _rms_mxu(x, g, n):
    inv = lax.rsqrt(_lane_sum(x * x) * (1.0 / n) + EPS)
    if x.shape[-1] > LANES:
        inv = jnp.concatenate([inv] * (x.shape[-1] // LANES), axis=-1)
    return x * inv * g


def _gelu(x):
    return 0.5 * x * (1.0 + jnp.tanh(math.sqrt(2.0 / math.pi) * (x + 0.044715 * (x * x * x))))


def _rope(x, cos, sin_lo, sin_hi, half):
    return (x * cos + pltpu.roll(x, LANES - half, axis=1) * sin_lo
            + pltpu.roll(x, half, axis=1) * sin_hi)


def _with_ones(x, upper):
    if upper:
        x = pltpu.roll(x, LANES // 2, axis=1)
    lane = lax.broadcasted_iota(jnp.int32, x.shape, 1)
    return jnp.where(lane < LANES // 2, x, 1.0).astype(BF16)


NSA_TRIG_LANE0 = MLA_ROPE


def _trig_kernel(pos_ref, inv_ref, cos_ref, sin_ref):
    ang = pos_ref[0].astype(F32) * inv_ref[...]
    cos_ref[0] = jnp.cos(ang)
    sin_ref[0] = jnp.sin(ang)


def _trig(pos3, ts=512):
    def inv(rot):
        v = ROPE_THETA ** (-jnp.arange(rot // 2, dtype=F32) * 2.0 / rot)
        return jnp.concatenate([v, v])
    inv_l = _pad_last(jnp.concatenate([inv(MLA_ROPE), inv(NSA_ROT)]), LANES)[None, :]
    b, s, _ = pos3.shape
    spec = pl.BlockSpec((1, ts, LANES), lambda i, j: (i, j, 0))
    return pl.pallas_call(
        _trig_kernel,
        grid=(b, s // ts),
        in_specs=[pl.BlockSpec((1, ts, 1), lambda i, j: (i, j, 0)), _full((1, LANES))],
        out_specs=[spec, spec],
        out_shape=[jax.ShapeDtypeStruct((b, s, LANES), F32)] * 2,
        compiler_params=_cparams(("parallel", "parallel")),
        name="rope_trig",
    )(pos3, inv_l)


def _rope_tables(cs, sn, lane0, half):
    if lane0:
        cs = pltpu.roll(cs, LANES - lane0, axis=1)
        sn = pltpu.roll(sn, LANES - lane0, axis=1)
    lane = lax.broadcasted_iota(jnp.int32, cs.shape, 1)
    cos = jnp.where(lane < 2 * half, cs, 1.0)
    s_lo = jnp.where(lane < half, -sn, 0.0)
    s_hi = jnp.where((lane >= half) & (lane < 2 * half), sn, 0.0)
    return cos, s_lo, s_hi


def _pad_last(a, n):
    return jnp.pad(a, [(0, 0)] * (a.ndim - 1) + [(0, n - a.shape[-1])])


def _full(shape):
    nd = len(shape)
    return pl.BlockSpec(shape, lambda *_: (0,) * nd)


def _proj_prep_kernel(x_ref, g_ref, w_ref, cs_ref, sn_ref, gcq_ref, wuq_ref, gckv_ref, wk_ref, wv_ref,
                      gqm_ref, gkm_ref, gqn_ref, gkn_ref,
                      mq_out, mk_out, mv_out, lru_out, s5_out, nq_out, ks_out, kw_out, vs_out, vw_out,
                      cg_out):
    x = x_ref[0]
    h = x * lax.rsqrt(jnp.mean(x * x, axis=-1, keepdims=True) + EPS) * g_ref[...]
    y = jnp.dot(h.astype(BF16), w_ref[...], preferred_element_type=F32)
    cs, sn = cs_ref[0], sn_ref[0]
    c0 = 0
    _mla_prep(y[:, c0:c0 + MLA_IN_W], cs, sn, gcq_ref, wuq_ref, gckv_ref, wk_ref, wv_ref, gqm_ref, gkm_ref,
              mq_out, mk_out, mv_out)
    c0 += MLA_IN_W
    lru_out[0] = y[:, c0:c0 + LRU_IN_W]
    c0 += LRU_IN_W
    s5_out[0] = y[:, c0:c0 + S5_IN_W].astype(BF16)
    c0 += S5_IN_W
    nkv = y[:, c0 + NQ_IN_W:c0 + NQ_IN_W + NKV_IN_W]
    _nsa_prep(y[:, c0:c0 + NQ_IN_W], nkv, cs, sn, gqn_ref, gkn_ref, nq_out, ks_out, kw_out, vs_out, vw_out)
    for i, slot in enumerate((0, 3, 5)):
        cg_out[0, :, LANES * i:LANES * (i + 1)] = nkv[:, LANES * slot:LANES * (slot + 1)]


def _inproj_cols():
    src = -np.ones((IN_W,), np.int64)
    o = 0
    src[o:o + 192] = np.arange(0, 192)
    src[o + 256:o + 384] = np.arange(192, 320)
    src[o + 384:o + 416] = np.arange(320, 352)
    o += MLA_IN_W
    src[o:o + 512] = np.arange(352, 864)
    o += LRU_IN_W
    src[o:o + 256] = np.arange(864, 1120)
    o += S5_IN_W
    for h in range(NSA_HEADS):
        src[o + LANES * h:o + LANES * h + 64] = np.arange(1120 + 64 * h, 1120 + 64 * h + 64)
    o += NQ_IN_W
    kv0 = 1376
    src[o:o + 64] = np.arange(kv0, kv0 + 64)
    src[o + 128:o + 192] = np.arange(kv0 + 128, kv0 + 192)
    src[o + 256:o + 320] = np.arange(kv0 + 256, kv0 + 320)
    src[o + 384:o + 448] = np.arange(kv0 + 64, kv0 + 128)
    src[o + 448:o + 512] = np.arange(kv0 + 192, kv0 + 256)
    src[o + 512:o + 576] = np.arange(kv0 + 320, kv0 + 384)
    src[o + 640:o + 652] = np.arange(1760, 1772)
    return src


_INPROJ_SRC = _inproj_cols()


def _prep_w_in(w_in):
    idx = jnp.asarray(np.maximum(_INPROJ_SRC, 0), jnp.int32)
    keep = jnp.asarray(_INPROJ_SRC >= 0)
    return jnp.where(keep[None, :], jnp.take(w_in, idx, axis=1), 0.0).astype(BF16)


CG_W = 3 * LANES


def _proj_prep(x3, g, w_pad, trig, pm, pn, ts=1024):
    b, s, _ = x3.shape
    heads = lambda: pl.BlockSpec((1, MLA_HEADS, ts, LANES), lambda i, j: (i, 0, j, 0))
    rows = lambda w: pl.BlockSpec((1, ts, w), lambda i, j: (i, j, 0))
    hshape = jax.ShapeDtypeStruct((b, MLA_HEADS, s, LANES), BF16)
    tok = lambda w, dt: jax.ShapeDtypeStruct((b, s, w), dt)
    return pl.pallas_call(
        _proj_prep_kernel,
        grid=(b, s // ts),
        in_specs=[rows(D_MODEL), _full((1, D_MODEL)), _full((D_MODEL, IN_W)), rows(LANES), rows(LANES),
                  _full((1, 2 * LANES)), _full((2 * LANES, MLA_HEADS * LANES)), _full((1, MLA_KV_RANK)),
                  _full((MLA_KV_RANK, MLA_HEADS * LANES)), _full((MLA_KV_RANK, MLA_HEADS * MLA_V)),
                  _full((1, LANES)), _full((1, LANES)),
                  _full((1, LANES)), _full((3, LANES))],
        out_specs=[heads(), heads(), heads(), rows(LRU_IN_W), rows(S5_IN_W),
                   heads(), rows(LANES), rows(LANES), rows(LANES), rows(LANES), rows(CG_W)],
        out_shape=[hshape, hshape, hshape, tok(LRU_IN_W, F32), tok(S5_IN_W, BF16),
                   hshape, tok(LANES, BF16), tok(LANES, BF16), tok(LANES, BF16), tok(LANES, BF16),
                   tok(CG_W, F32)],
        compiler_params=_cparams(("parallel", "parallel")),
        name="proj_prep",
    )(x3, g[None, :], w_pad, trig[0], trig[1], pm["g_cq"], pm["w_uq"], pm["g_ckv"], pm["w_k"], pm["w_v"],
      pm["g_q"], pm["g_k"], pn["g_q"], pn["g_k"])


def _mla_prep(xin, cs, sn, gcq_ref, wuq_ref, gckv_ref, wk_ref, wv_ref, gq_ref, gk_ref, q_out, k_out, v_out):
    cq = _rms_mxu(xin[:, 0:256], gcq_ref[...], MLA_Q_RANK)
    ckv = _rms_mxu(xin[:, 256:384], gckv_ref[...], MLA_KV_RANK)
    kpe = xin[:, 384:512]
    q = _dot(cq, wuq_ref[...])
    kn = _dot(ckv, wk_ref[...])
    v = _dot(ckv, wv_ref[...])
    cos, s_lo, s_hi = _rope_tables(cs, sn, 0, MLA_ROPE // 2)
    scale = MLA_QK ** -0.5 * LOG2E
    for h in range(MLA_HEADS):
        qh = _rms_mxu(q[:, LANES * h:LANES * (h + 1)], gq_ref[...], MLA_QK)
        q_out[0, h] = (_rope(qh, cos, s_lo, s_hi, MLA_ROPE // 2) * scale).astype(BF16)
        kh = _rms_mxu(kn[:, LANES * h:LANES * (h + 1)] + kpe, gk_ref[...], MLA_QK)
        k_out[0, h] = _rope(kh, cos, s_lo, s_hi, MLA_ROPE // 2).astype(BF16)
        v_out[0, h] = _with_ones(v[:, LANES * (h // 2):LANES * (h // 2 + 1)], h % 2 == 1)


def _attend(q, segs, v1, s_scr, p_scr, m_scr, nh, chunk):
    rows = q.shape[0]
    c0 = 0
    m = None
    for k, bias in segs:
        n = k.shape[0]
        sb = _dot_nt(q, k)
        if bias is not None:
            sb = (sb.reshape(nh, rows // nh, n) + bias[None]).reshape(rows, n)
        s_scr[0:rows, c0:c0 + n] = sb
        mx = jnp.max(sb, axis=-1, keepdims=True)
        m = mx if m is None else jnp.maximum(m, mx)
        c0 += n
    m_scr[0:rows, :] = jnp.broadcast_to(m, (rows, LANES))
    for r0 in range(0, rows, chunk):
        mb = m_scr[r0:r0 + chunk, :]
        for j0 in range(0, c0, LANES):
            p = jnp.exp2(s_scr[r0:r0 + chunk, j0:j0 + LANES] - mb)
            p_scr[r0:r0 + chunk, j0:j0 + LANES] = p.astype(BF16)
    o = jnp.dot(p_scr[0:rows, 0:c0], v1, preferred_element_type=F32)
    return (o * (1.0 / pltpu.roll(o, LANES // 2, axis=1)))[:, 0:LANES // 2]


def _mla_attn_kernel(q_ref, k_ref, v_ref, o_ref, s_scr, p_scr, m_scr, o_scr, bias_scr, *, tq, chunk):
    qi = pl.program_id(1)
    row = lax.broadcasted_iota(jnp.int32, (tq, tq), 0)
    col = lax.broadcasted_iota(jnp.int32, (tq, tq), 1)
    bias_scr[...] = jnp.where(col <= row, 0.0, NEG)
    nset = s_scr.shape[0]
    for c in range(k_ref.shape[2] // tq):

        @pl.when(qi == c)
        def _(c=c):
            n = (c + 1) * tq

            def heads(i, carry):
                for j in range(nset):
                    h = i * nset + j
                    segs = [(k_ref[0, h, n - tq:n, :], bias_scr[...])]
                    if c > 0:
                        segs = [(k_ref[0, h, 0:n - tq, :], None)] + segs
                    o_scr[h] = _attend(q_ref[0, h], segs, v_ref[0, h, 0:n, :], s_scr.at[j], p_scr.at[j],
                                       m_scr.at[j], 1, chunk)
                return carry

            lax.fori_loop(0, MLA_HEADS // nset, heads, 0)

    for h in range(MLA_HEADS):
        o_ref[0, :, MLA_V * h:MLA_V * (h + 1)] = o_scr[h].astype(o_ref.dtype)


def _mla(q, k, v, tq=512, chunk=128, nset=4):
    b, _, s, _ = q.shape
    return pl.pallas_call(
        functools.partial(_mla_attn_kernel, tq=tq, chunk=chunk),
        grid=(b, s // tq),
        in_specs=[pl.BlockSpec((1, MLA_HEADS, tq, LANES), lambda i, j: (i, 0, j, 0)),
                  pl.BlockSpec((1, MLA_HEADS, s, LANES), lambda i, j: (i, 0, 0, 0)),
                  pl.BlockSpec((1, MLA_HEADS, s, LANES), lambda i, j: (i, 0, 0, 0))],
        out_specs=pl.BlockSpec((1, tq, GROUP_W), lambda i, j: (i, j, 0)),
        out_shape=jax.ShapeDtypeStruct((b, s, GROUP_W), BF16),
        scratch_shapes=[pltpu.VMEM((nset, tq, s), F32), pltpu.VMEM((nset, tq, s), BF16),
                        pltpu.VMEM((nset, tq, LANES), F32),
                        pltpu.VMEM((MLA_HEADS, tq, MLA_V), F32), pltpu.VMEM((tq, tq), F32)],
        compiler_params=_cparams(("parallel", "arbitrary")),
        name="mla_attn",
    )(q, k, v)


def _prep_mla(g_cq, g_ckv, w_uq, w_ukv, g_q, g_k):
    wq = w_uq.reshape(MLA_Q_RANK, MLA_HEADS, MLA_QK)
    wq = _pad_last(wq, LANES).reshape(MLA_Q_RANK, MLA_HEADS * LANES)
    wq = jnp.pad(wq, ((0, 256 - MLA_Q_RANK), (0, 0)))
    wkv = w_ukv.reshape(MLA_KV_RANK, MLA_HEADS, MLA_NOPE + MLA_V)
    wk = jnp.pad(wkv[:, :, :MLA_NOPE], ((0, 0), (0, 0), (MLA_ROPE, LANES - MLA_QK)))
    wk = wk.reshape(MLA_KV_RANK, MLA_HEADS * LANES)
    wv = wkv[:, :, MLA_NOPE:].reshape(MLA_KV_RANK, MLA_HEADS * MLA_V)
    return dict(g_cq=_pad_last(g_cq, 256)[None, :], g_ckv=g_ckv[None, :], w_uq=wq.astype(BF16),
                w_k=wk.astype(BF16), w_v=wv.astype(BF16), g_q=_pad_last(g_q, LANES)[None, :],
                g_k=_pad_last(g_k, LANES)[None, :])


def _shift_rows(x, k, row, fill):
    return jnp.where(row >= k, pltpu.roll(x, k, axis=0), fill)


def _lru_kernel(in_ref, cw_ref, cb_ref, wa_ref, ba_ref, wi_ref, bi_ref, lam_ref, o_ref, a_scr, b_scr, *,
                chunk):
    xin = in_ref[0]
    s = xin.shape[0]
    xb = xin[:, :LRU_W]
    row = lax.broadcasted_iota(jnp.int32, (s, LRU_W), 0)
    u = cb_ref[...] + xb * cw_ref[CONV_W - 1:CONV_W, :]
    for j in range(CONV_W - 1):
        u = u + _shift_rows(xb, CONV_W - 1 - j, row, 0.0) * cw_ref[j:j + 1, :]
    r = jax.nn.sigmoid(_dot(u, wa_ref[...]) + ba_ref[...])
    gi = jax.nn.sigmoid(_dot(u, wi_ref[...]) + bi_ref[...])
    nlam = -lam_ref[...]
    softplus = jnp.maximum(nlam, 0.0) + jnp.log1p(jnp.exp(-jnp.abs(nlam)))
    log_a = (-LRU_C) * r * softplus
    a = jnp.exp(log_a)
    y = jnp.maximum(-jnp.tanh(log_a) * (a * a + 1.0), 0.0)
    mult = jnp.where(y > 0.0, y * lax.rsqrt(y), 0.0)
    mult = jnp.where(row == 0, 1.0, mult)
    bt = mult * gi * u
    rin = jnp.bitwise_and(row, chunk - 1)
    k = 1
    while k < chunk:
        bt = a * _shift_rows(bt, k, rin, 0.0) + bt
        a = a * _shift_rows(a, k, rin, 1.0)
        k *= 2
    a_scr[...] = a
    b_scr[...] = bt

    def carry_chunk(j, h_prev):
        r0 = pl.multiple_of(j * chunk, chunk)
        h = b_scr[pl.ds(r0, chunk), :] + a_scr[pl.ds(r0, chunk), :] * h_prev
        o_ref[0, pl.ds(r0, chunk), :] = (h * _gelu(in_ref[0, pl.ds(r0, chunk), LRU_W:])).astype(o_ref.dtype)
        return h[chunk - 1:chunk, :]

    lax.fori_loop(0, s // chunk, carry_chunk, jnp.zeros((1, LRU_W), F32))


def _block_diag(w):
    n, i, j = w.shape
    eye = jnp.eye(n, dtype=w.dtype)
    return (eye[:, None, :, None] * w[:, :, None, :]).reshape(n * i, n * j)


def _lru(lru_in, p, chunk=64):
    b, s, _ = lru_in.shape
    return pl.pallas_call(
        functools.partial(_lru_kernel, chunk=chunk),
        grid=(b,),
        in_specs=[pl.BlockSpec((1, s, LRU_IN_W), lambda i: (i, 0, 0)),
                  _full((CONV_W, LRU_W)), _full((1, LRU_W)), _full((LRU_W, LRU_W)), _full((1, LRU_W)),
                  _full((LRU_W, LRU_W)), _full((1, LRU_W)), _full((1, LRU_W))],
        out_specs=pl.BlockSpec((1, s, LRU_W), lambda i: (i, 0, 0)),
        out_shape=jax.ShapeDtypeStruct((b, s, LRU_W), BF16),
        scratch_shapes=[pltpu.VMEM((s, LRU_W), F32), pltpu.VMEM((s, LRU_W), F32)],
        compiler_params=_cparams(("parallel",)),
        name="rglru",
    )(lru_in, p["cw"], p["cb"], p["wa"], p["ba"], p["wi"], p["bi"], p["lam"])


def _prep_lru(conv_w, conv_b, w_a, b_a, w_i, b_i, lam):
    return dict(cw=conv_w, cb=conv_b[None, :], wa=_block_diag(w_a).astype(BF16),
                ba=b_a.reshape(1, LRU_W), wi=_block_diag(w_i).astype(BF16), bi=b_i.reshape(1, LRU_W),
                lam=lam[None, :])


def _s5_disc_kernel(are_ref, aim_ref, ldt_ref, arer_ref, aimr_ref, bre_ref, bim_ref,
                    abre_ref, abim_ref, bbre_ref, bbim_ref):
    dt = jnp.exp(ldt_ref[...])

    def disc(a_re, a_im):
        mag = jnp.exp(dt * a_re)
        ab_re = mag * jnp.cos(dt * a_im)
        ab_im = mag * jnp.sin(dt * a_im)
        den = a_re * a_re + a_im * a_im
        n_re = ab_re - 1.0
        g_re = (n_re * a_re + ab_im * a_im) / den
        g_im = (ab_im * a_re - n_re * a_im) / den
        return ab_re, ab_im, g_re, g_im

    ab_re, ab_im, _, _ = disc(are_ref[...], aim_ref[...])
    abre_ref[...] = ab_re
    abim_ref[...] = ab_im
    _, _, g_re, g_im = disc(arer_ref[...], aimr_ref[...])
    bbre_ref[...] = g_re * bre_ref[...] - g_im * bim_ref[...]
    bbim_ref[...] = g_re * bim_ref[...] + g_im * bre_ref[...]


def _s5_kernel(u_ref, are_ref, aim_ref, bre_ref, bim_ref, cre_ref, cim_ref, d_ref, wg_ref, bg_ref,
               o_ref, hre, him, st_re, st_im, *, tc, nb, cw):
    @pl.when(pl.program_id(0) == 0)
    def _():
        st_re[...] = jnp.zeros_like(st_re)
        st_im[...] = jnp.zeros_like(st_im)

    ub = u_ref[...].reshape(tc * nb, S5_W)
    u = ub.astype(F32)
    hre[...] = jnp.dot(ub, bre_ref[...], preferred_element_type=F32)
    him[...] = jnp.dot(ub, bim_ref[...], preferred_element_type=F32)
    for c in range(S5_STATE // cw):
        cs = slice(c * cw, (c + 1) * cw)
        ar = jnp.broadcast_to(are_ref[:, cs], (nb, cw))
        ai = jnp.broadcast_to(aim_ref[:, cs], (nb, cw))

        def body(t, carry, cs=cs, ar=ar, ai=ai):
            hr, hi = carry
            r0 = pl.multiple_of(t * nb, nb)
            nr = ar * hr - ai * hi + hre[pl.ds(r0, nb), cs]
            ni = ar * hi + ai * hr + him[pl.ds(r0, nb), cs]
            hre[pl.ds(r0, nb), cs] = nr
            him[pl.ds(r0, nb), cs] = ni
            return nr, ni

        hr, hi = lax.fori_loop(0, tc, body, (st_re[:, cs], st_im[:, cs]), unroll=4)
        st_re[:, cs] = hr
        st_im[:, cs] = hi
    y = (jnp.dot(hre[...].astype(BF16), cre_ref[...], preferred_element_type=F32)
         - jnp.dot(him[...].astype(BF16), cim_ref[...], preferred_element_type=F32))
    y = _gelu(y + d_ref[...] * u)
    z = _dot(y, wg_ref[...]) + bg_ref[...]
    o_ref[...] = (y * jax.nn.sigmoid(z)).astype(o_ref.dtype).reshape(tc, nb, S5_W)


def _prep_s5(a_re, a_im, log_dt, b_re, b_im, c_re, c_im, d, w_glu, b_glu):
    g, p, ch = S5_GROUPS, S5_P, S5_CH
    ab_re, ab_im, bb_re, bb_im = pl.pallas_call(
        _s5_disc_kernel,
        out_shape=[jax.ShapeDtypeStruct((g, p), F32), jax.ShapeDtypeStruct((g, p), F32),
                   jax.ShapeDtypeStruct((g, p * ch), F32), jax.ShapeDtypeStruct((g, p * ch), F32)],
        name="s5_discretize",
    )(a_re, a_im, log_dt[:, None], jnp.repeat(a_re, ch, axis=1), jnp.repeat(a_im, ch, axis=1),
      b_re.reshape(g, p * ch), b_im.reshape(g, p * ch))

    def b_dense(bb):
        return _block_diag(bb.reshape(g, p, ch).transpose(0, 2, 1)).astype(BF16)

    def c_dense(c):
        return _block_diag(c.transpose(0, 2, 1)).astype(BF16)

    return dict(a_re=ab_re.reshape(1, S5_STATE), a_im=ab_im.reshape(1, S5_STATE),
                b_re=b_dense(bb_re), b_im=b_dense(bb_im), c_re=c_dense(c_re), c_im=c_dense(c_im),
                d=d[None, :], w_glu=w_glu.astype(BF16), b_glu=b_glu[None, :])


def _s5(u_tm, p, tc=128, cw=512):
    s, nb, _ = u_tm.shape
    return pl.pallas_call(
        functools.partial(_s5_kernel, tc=tc, nb=nb, cw=cw),
        grid=(s // tc,),
        in_specs=[pl.BlockSpec((tc, nb, S5_W), lambda i: (i, 0, 0)),
                  _full((1, S5_STATE)), _full((1, S5_STATE)),
                  _full((S5_W, S5_STATE)), _full((S5_W, S5_STATE)),
                  _full((S5_STATE, S5_W)), _full((S5_STATE, S5_W)),
                  _full((1, S5_W)), _full((S5_W, S5_W)), _full((1, S5_W))],
        out_specs=pl.BlockSpec((tc, nb, S5_W), lambda i: (i, 0, 0)),
        out_shape=jax.ShapeDtypeStruct((s, nb, S5_W), BF16),
        scratch_shapes=[pltpu.VMEM((tc * nb, S5_STATE), F32), pltpu.VMEM((tc * nb, S5_STATE), F32),
                        pltpu.VMEM((nb, S5_STATE), F32), pltpu.VMEM((nb, S5_STATE), F32)],
        compiler_params=_cparams(("arbitrary",)),
        name="s5",
    )(u_tm, p["a_re"], p["a_im"], p["b_re"], p["b_im"], p["c_re"], p["c_im"], p["d"], p["w_glu"],
      p["b_glu"])


def _nsa_prep(nq, nkv, cs, sn, gq_ref, gk_ref, q_out, ks_out, kw_out, vs_out, vw_out):
    half = NSA_ROT // 2
    cos, s_lo, s_hi = _rope_tables(cs, sn, NSA_TRIG_LANE0, half)
    scale = NSA_DK ** -0.5 * LOG2E
    for h in range(NSA_HEADS):
        qh = _rms_mxu(nq[:, LANES * h:LANES * (h + 1)], gq_ref[...], NSA_DK)
        q_out[0, h] = (_rope(qh, cos, s_lo, s_hi, half) * scale).astype(BF16)
    ks = _rms_mxu(nkv[:, 128:256], gk_ref[1:2, :], NSA_DK)
    ks_out[0] = _rope(ks, cos, s_lo, s_hi, half).astype(BF16)
    kw = _rms_mxu(nkv[:, 256:384], gk_ref[2:3, :], NSA_DK)
    kw_out[0] = _rope(kw, cos, s_lo, s_hi, half).astype(BF16)
    vs_out[0] = _with_ones(nkv[:, 384:512], True)
    vw_out[0] = _with_ones(nkv[:, 512:640], False)


def _nsa_cmp_kernel(k_in, v_in, cs_ref, sn_ref, pek_ref, w1k_ref, w2k_ref, pev_ref, w1v_ref, w2v_ref,
                    g_ref, kc_out, vc_out):
    nc = kc_out.shape[1]

    def compress(x_ref, pe_ref, w1_ref, w2_ref):
        lo = hi = None
        for j in range(CMP_STRIDE):
            xj = x_ref[0, pl.ds(j, nc, stride=CMP_STRIDE), :]
            dl = _dot(xj + pe_ref[j:j + 1, :], w1_ref[j])
            dh = _dot(xj + pe_ref[CMP_STRIDE + j:CMP_STRIDE + j + 1, :], w1_ref[CMP_STRIDE + j])
            lo = dl if lo is None else lo + dl
            hi = dh if hi is None else hi + dh
        hid = lo + pltpu.roll(hi, nc - 1, axis=0)
        return _dot(_gelu(hid), w2_ref[...])

    kc = compress(k_in, pek_ref, w1k_ref, w2k_ref)
    cos, s_lo, s_hi = _rope_tables(cs_ref[0], sn_ref[0], NSA_TRIG_LANE0, NSA_ROT // 2)
    kc = _rope(_rms(kc, g_ref[0:1, :], NSA_DK), cos, s_lo, s_hi, NSA_ROT // 2)
    kc_out[0] = kc.astype(BF16)
    vc_out[0] = compress(v_in, pev_ref, w1v_ref, w2v_ref).astype(BF16)


def _nsa_cmpsel_kernel(q_ref, kc_ref, vc_ref, ov_ref, ocmp_ref, sel_ref, *, tc):
    qi = pl.program_id(1)
    nh = NSA_HEADS
    q = q_ref[0].reshape(nh * tc, LANES)
    lane = lax.broadcasted_iota(jnp.int32, (tc, LANES), 1)
    qpos = qi * tc + lax.broadcasted_iota(jnp.int32, (tc, LANES), 0)

    s = _dot_nt(q, kc_ref[0]).reshape(nh, tc, LANES)
    valid = (lane * CMP_STRIDE + (CMP_LEN - 1)) <= qpos
    s = jnp.where(valid, s, NEG)
    e = jnp.exp2(s - jnp.max(s, axis=-1, keepdims=True))
    p_c = jnp.where(valid, e / jnp.sum(e, axis=-1, keepdims=True), 0.0)
    ocmp_ref[0] = jnp.dot(p_c.astype(BF16).reshape(nh * tc, LANES), vc_ref[0],
                          preferred_element_type=F32).reshape(nh, tc, NSA_DK)

    imp = _dot_f32_by_exact(jnp.sum(p_c, axis=0), ov_ref[...])
    cur = qpos // SEL_LEN
    lane_f = lane.astype(F32)
    sel = (lane == 0) | (lane == cur) | (lane == cur - 1)
    cand = (lane < cur - 1) & (lane > 0)
    for _ in range(SEL_TOPK - 3):
        sc = jnp.where(cand, imp, -jnp.inf)
        best = jnp.max(sc, axis=-1, keepdims=True)
        pick = jnp.min(jnp.where(cand & (sc == best), lane_f, float(LANES)), axis=-1, keepdims=True)
        hit = lane_f == pick
        sel = sel | hit
        cand = cand & jnp.logical_not(hit)
    sel_ref[0] = jnp.where(sel, 1.0, 0.0).astype(BF16)


def _nsa_attn_kernel(q_ref, sel_ref, ocmp_ref, ks_ref, vs_ref, kw_ref, vw_ref, gate_ref, ex_ref,
                     o_ref, s_scr, p_scr, m_scr, obr_scr, *, tq, chunk, kchunk):
    qi = pl.program_id(1)
    nh = NSA_HEADS
    rows = nh * tq
    s_len = ks_ref.shape[1]

    def window():
        wk = WIN + tq
        start = pl.multiple_of(jnp.maximum(qi - WIN // tq, 0) * tq, tq)
        kpos = start + lax.broadcasted_iota(jnp.int32, (tq, wk), 1)
        qrow = qi * tq + lax.broadcasted_iota(jnp.int32, (tq, wk), 0)
        bias = jnp.where((kpos <= qrow) & (qrow - kpos < WIN), 0.0, NEG)
        obr_scr[1] = _attend(q_ref[0].reshape(rows, LANES), [(kw_ref[0, pl.ds(start, wk), :], bias)],
                             vw_ref[0, pl.ds(start, wk), :], s_scr.at[1], p_scr.at[1], m_scr.at[1],
                             nh, chunk)

    for c in range(s_len // kchunk):

        @pl.when(qi // (kchunk // tq) == c)
        def _(c=c):
            n = (c + 1) * kchunk
            em = jnp.dot(sel_ref[0], ex_ref[:, 0:n], preferred_element_type=F32)
            kpos = lax.broadcasted_iota(jnp.int32, (tq, n), 1)
            qrow = qi * tq + lax.broadcasted_iota(jnp.int32, (tq, n), 0)
            bias = jnp.where((em > 0.5) & (kpos <= qrow), 0.0, NEG)
            obr_scr[0] = _attend(q_ref[0].reshape(rows, LANES), [(ks_ref[0, 0:n, :], bias)],
                                 vs_ref[0, 0:n, :], s_scr.at[0], p_scr.at[0], m_scr.at[0], nh, chunk)
            window()

    g = jax.nn.sigmoid(gate_ref[0])
    for h in range(nh):
        r = slice(h * tq, (h + 1) * tq)
        o = (g[:, 3 * h:3 * h + 1] * ocmp_ref[0, h] + g[:, 3 * h + 1:3 * h + 2] * obr_scr[0, r, :]
             + g[:, 3 * h + 2:3 * h + 3] * obr_scr[1, r, :])
        o_ref[0, :, NSA_DK * h:NSA_DK * (h + 1)] = o.astype(o_ref.dtype)


def _nsa_tables(s, tq):
    nc = s // CMP_STRIDE
    nsb = s // SEL_LEN
    cs = np.arange(nc) * CMP_STRIDE
    ss = np.arange(nsb) * SEL_LEN
    ov = np.clip(np.minimum(cs[:, None] + CMP_LEN, ss[None, :] + SEL_LEN)
                 - np.maximum(cs[:, None], ss[None, :]), 0, None) / CMP_STRIDE
    ov[(s - CMP_LEN) // CMP_STRIDE + 1:] = 0.0
    ov_pad = np.zeros((nc, LANES), np.float32)
    ov_pad[:, :nsb] = ov
    ex = np.zeros((LANES, s), np.float32)
    ex[np.arange(s) // SEL_LEN, np.arange(s)] = 1.0
    return jnp.asarray(ov_pad, BF16), jnp.asarray(ex, BF16)


def _nsa(q, ks, kw, vs, vw, cg, trig, p, tq=256, tc=2048, chunk=128, kchunk=256):
    b, s, _ = ks.shape
    nc = s // CMP_STRIDE
    assert nc == LANES and s // SEL_LEN <= LANES and s >= WIN + tq and s % kchunk == 0

    last = np.minimum(np.arange(nc) * CMP_STRIDE + CMP_LEN - 1, s - 1)
    cspec = pl.BlockSpec((1, nc, LANES), lambda i: (i, 0, 0))
    kc, vc = pl.pallas_call(
        _nsa_cmp_kernel,
        grid=(b,),
        in_specs=[pl.BlockSpec((1, s, LANES), lambda i: (i, 0, 0)),
                  pl.BlockSpec((1, s, LANES), lambda i: (i, 0, 1)),
                  cspec, cspec,
                  _full((CMP_LEN, LANES)), _full((CMP_LEN, LANES, CMP_HID)), _full((CMP_HID, LANES)),
                  _full((CMP_LEN, LANES)), _full((CMP_LEN, LANES, CMP_HID)), _full((CMP_HID, NSA_DK)),
                  _full((3, LANES))],
        out_specs=[cspec, pl.BlockSpec((1, nc, NSA_DK), lambda i: (i, 0, 0))],
        out_shape=[jax.ShapeDtypeStruct((b, nc, LANES), BF16),
                   jax.ShapeDtypeStruct((b, nc, NSA_DK), BF16)],
        compiler_params=_cparams(("parallel",)),
        name="nsa_compress",
    )(cg, cg, trig[0][:, last, :], trig[1][:, last, :], p["pe_k"], p["w1_k"], p["w2_k"], p["pe_v"],
      p["w1_v"], p["w2_v"], p["g_k"])

    ov, ex = _nsa_tables(s, tq)
    o_cmp, sel = pl.pallas_call(
        functools.partial(_nsa_cmpsel_kernel, tc=tc),
        grid=(b, s // tc),
        in_specs=[pl.BlockSpec((1, NSA_HEADS, tc, LANES), lambda i, j: (i, 0, j, 0)),
                  pl.BlockSpec((1, nc, LANES), lambda i, j: (i, 0, 0)),
                  pl.BlockSpec((1, nc, NSA_DK), lambda i, j: (i, 0, 0)),
                  _full((nc, LANES))],
        out_specs=[pl.BlockSpec((1, NSA_HEADS, tc, NSA_DK), lambda i, j: (i, 0, j, 0)),
                   pl.BlockSpec((1, tc, LANES), lambda i, j: (i, j, 0))],
        out_shape=[jax.ShapeDtypeStruct((b, NSA_HEADS, s, NSA_DK), F32),
                   jax.ShapeDtypeStruct((b, s, LANES), BF16)],
        compiler_params=_cparams(("parallel", "parallel")),
        name="nsa_cmpsel",
    )(q, kc, vc, ov)

    rows = NSA_HEADS * tq
    kvspec = pl.BlockSpec((1, s, LANES), lambda i, j: (i, 0, 0))
    return pl.pallas_call(
        functools.partial(_nsa_attn_kernel, tq=tq, chunk=chunk, kchunk=kchunk),
        grid=(b, s // tq),
        in_specs=[pl.BlockSpec((1, NSA_HEADS, tq, LANES), lambda i, j: (i, 0, j, 0)),
                  pl.BlockSpec((1, tq, LANES), lambda i, j: (i, j, 0)),
                  pl.BlockSpec((1, NSA_HEADS, tq, NSA_DK), lambda i, j: (i, 0, j, 0)),
                  kvspec, kvspec, kvspec, kvspec,
                  pl.BlockSpec((1, tq, LANES), lambda i, j: (i, j, CG_W // LANES - 1)),
                  _full((LANES, s))],
        out_specs=pl.BlockSpec((1, tq, GROUP_W), lambda i, j: (i, j, 0)),
        out_shape=jax.ShapeDtypeStruct((b, s, GROUP_W), BF16),
        scratch_shapes=[pltpu.VMEM((2, rows, s), F32), pltpu.VMEM((2, rows, s), BF16),
                        pltpu.VMEM((2, rows, LANES), F32), pltpu.VMEM((2, rows, NSA_DK), F32)],
        compiler_params=_cparams(("parallel", "arbitrary")),
        name="nsa_attn",
    )(q, sel, o_cmp, ks, vs, kw, vw, cg, ex)


def _prep_nsa(g_q, g_k, pe_k, w1_k, w2_k, pe_v, w1_v, w2_v):
    def per_token(w1):
        w = w1.reshape(CMP_LEN, NSA_DK, CMP_HID)
        return jnp.pad(w, ((0, 0), (0, LANES - NSA_DK), (0, 0))).astype(BF16)

    return dict(g_q=_pad_last(g_q, LANES)[None, :], g_k=_pad_last(g_k, LANES),
                pe_k=_pad_last(pe_k, LANES), w1_k=per_token(w1_k),
                w2_k=_pad_last(w2_k, LANES).astype(BF16),
                pe_v=_pad_last(pe_v, LANES), w1_v=per_token(w1_v), w2_v=w2_v.astype(BF16))


MOE_ROWS = 288
MOE_ALIGN = 16


def _route_t(lt):
    row = lambda r: lt[r:r + 1, :]
    lg = [row(i) for i in range(N_GROUPS)]
    gmax = functools.reduce(jnp.maximum, lg)
    pg_top = 1.0 / functools.reduce(lambda a, b: a + b, [jnp.exp(v - gmax) for v in lg])
    taken = jnp.zeros_like(gmax) > 1.0
    oh = []
    for v in lg:
        hit = (v == gmax) & jnp.logical_not(taken)
        oh.append(hit)
        taken = taken | hit
    le = []
    for k in range(EXP_PER_GROUP):
        v = row(N_GROUPS + k)
        for i in range(1, N_GROUPS):
            v = jnp.where(oh[i], row(N_GROUPS + EXP_PER_GROUP * i + k), v)
        le.append(v)
    m1 = functools.reduce(jnp.maximum, le)
    taken = jnp.zeros_like(m1) > 1.0
    first = []
    for v in le:
        hit = (v == m1) & jnp.logical_not(taken)
        first.append(hit)
        taken = taken | hit
    le2 = [jnp.where(f, -jnp.inf, v) for f, v in zip(first, le)]
    m2 = functools.reduce(jnp.maximum, le2)
    taken = jnp.zeros_like(m1) > 1.0
    second = []
    for v in le2:
        hit = (v == m2) & jnp.logical_not(taken)
        second.append(hit)
        taken = taken | hit
    v2 = jnp.exp(m2 - m1)
    w1 = pg_top / (1.0 + v2)
    w2 = pg_top * v2 / (1.0 + v2)
    comb = []
    for i in range(N_GROUPS):
        for k in range(EXP_PER_GROUP):
            w = jnp.where(first[k], w1, 0.0) + jnp.where(second[k], w2, 0.0)
            comb.append(jnp.where(oh[i], w, 0.0))
    return [jnp.where(o, 1.0, 0.0) for o in oh], comb


def _moe_kernel(ya_ref, yb_ref, yc_ref, yd_ref, x_ref, gout_ref, wout_ref, g_ref, wr_ref,
                br_ref, tri_ref, wg_ref, wu_ref, wd_ref, o_ref, hs, cs, ys, tok, meta):
    grp = pl.program_id(1)
    tm = x_ref.shape[0]

    @pl.when(grp == 0)
    def _():
        x = x_ref[...]
        for i, y_ref in enumerate((ya_ref, yb_ref, yc_ref, yd_ref)):
            y = _rms(y_ref[...].astype(F32), gout_ref[i:i + 1, :], GROUP_W)
            x = x + _dot(y, wout_ref[GROUP_W * i:GROUP_W * (i + 1), :])
        o_ref[...] = x
        h = x * lax.rsqrt(jnp.mean(x * x, axis=-1, keepdims=True) + EPS) * g_ref[...]
        h_hi = h.astype(BF16)
        h_lo = (h - h_hi.astype(F32)).astype(BF16)
        lhl = jnp.dot(h_hi, wr_ref[...], preferred_element_type=F32)
        logits = (lhl[:, 0:LANES] + lhl[:, LANES:2 * LANES]
                  + jnp.dot(h_lo, wr_ref[:, 0:LANES], preferred_element_type=F32)) + br_ref[...]
        oh, comb = _route_t(logits.T)

        oh8 = jnp.concatenate(oh + [jnp.zeros((8 - N_GROUPS, tm), F32)], axis=0)
        cum = jnp.dot(oh8.astype(BF16), tri_ref[...], preferred_element_type=F32)
        cnt = jnp.sum(oh8, axis=1, keepdims=True)
        starts, acc0 = [], jnp.zeros((1, 1), F32)
        for i in range(N_GROUPS):
            starts.append(acc0)
            acc0 = acc0 + cnt[i:i + 1, :]
            meta[i] = starts[i][0, 0].astype(jnp.int32)
            meta[N_GROUPS + i] = cnt[i:i + 1, :][0, 0].astype(jnp.int32)
        dest = functools.reduce(lambda a, b: a + b,
                                [oh[i] * (starts[i] + cum[i:i + 1, :] - 1.0) for i in range(N_GROUPS)])
        perm = jnp.where(lax.broadcasted_iota(jnp.int32, (tm, tm), 0).astype(F32) == dest, 1.0, 0.0)
        perm = perm.astype(BF16)

        comb4 = [functools.reduce(lambda a, b: a + b, comb[k::EXP_PER_GROUP]) for k in range(EXP_PER_GROUP)]
        stack = jnp.concatenate(comb4 + [dest] + [jnp.zeros((LANES - EXP_PER_GROUP - 1, tm), F32)], axis=0)
        stack_t = stack.T
        tok[...] = stack_t
        c_hi = stack_t.astype(BF16)
        c_lo = (stack_t - c_hi.astype(F32)).astype(BF16)
        hs[...] = jnp.dot(perm, h_hi, preferred_element_type=F32).astype(BF16)
        cs[...] = jnp.dot(perm, jnp.concatenate([c_hi, c_lo], axis=1), preferred_element_type=F32)
        ys[...] = jnp.zeros_like(ys)

    start = meta[grp]
    end = start + meta[N_GROUPS + grp]
    first = (start // MOE_ALIGN) * MOE_ALIGN
    nwin = jnp.where(end > start, (end - first + MOE_ROWS - 1) // MOE_ROWS, 0)

    def window(j, carry):
        want = first + j * MOE_ROWS
        r0 = pl.multiple_of(jnp.minimum(want, tm - MOE_ROWS), MOE_ALIGN)
        lo = jnp.maximum(start, want)
        hi = jnp.minimum(end, want + MOE_ROWS)
        row = r0 + lax.broadcasted_iota(jnp.int32, (MOE_ROWS, LANES), 0)
        hb = hs[pl.ds(r0, MOE_ROWS), :]
        cb = cs[pl.ds(r0, MOE_ROWS), :]
        cb = jnp.where((row >= lo) & (row < hi), cb[:, 0:LANES] + cb[:, LANES:2 * LANES], 0.0)
        acc = None
        for k in range(EXP_PER_GROUP):
            a = jax.nn.silu(jnp.dot(hb, wg_ref[k], preferred_element_type=F32)) \
                * jnp.dot(hb, wu_ref[k], preferred_element_type=F32)
            d = jnp.dot((a * cb[:, k:k + 1]).astype(BF16), wd_ref[k], preferred_element_type=F32)
            acc = d if acc is None else acc + d
        ys[pl.ds(r0, MOE_ROWS), :] = (ys[pl.ds(r0, MOE_ROWS), :].astype(F32) + acc).astype(BF16)
        return carry

    lax.fori_loop(0, nwin, window, 0)

    @pl.when(grp == N_GROUPS - 1)
    def _():
        dest_t = tok[:, EXP_PER_GROUP:EXP_PER_GROUP + 1]
        unperm = jnp.where(lax.broadcasted_iota(jnp.int32, (tm, tm), 1).astype(F32) == dest_t, 1.0, 0.0)
        o_ref[...] += jnp.dot(unperm.astype(BF16), ys[...], preferred_element_type=F32)


def _outproj_moe(ys, x2, g_out, w_out, p, experts, layer, tm=1024):
    t = x2.shape[0]
    assert tm >= MOE_ROWS and tm % MOE_ALIGN == 0
    yspec = pl.BlockSpec((tm, GROUP_W), lambda i, g: (i, 0))
    tri = jnp.asarray(np.triu(np.ones((tm, tm), np.float32)), BF16)

    def const(shape):
        nd = len(shape)
        return pl.BlockSpec(shape, lambda i, g: (0,) * nd)

    def resident(shape):
        nd = len(shape)
        return pl.BlockSpec(shape, lambda i, g: (0,) * nd, pipeline_mode=pl.Buffered(1))

    def of_group(shape):
        return pl.BlockSpec((None, EXP_PER_GROUP) + shape, lambda i, g: (layer, g, 0, 0))

    return pl.pallas_call(
        _moe_kernel,
        grid=(t // tm, N_GROUPS),
        in_specs=[yspec, yspec, yspec, yspec, pl.BlockSpec((tm, D_MODEL), lambda i, g: (i, 0)),
                  const((4, GROUP_W)), resident((D_MODEL, D_MODEL)),
                  const((1, D_MODEL)), const((D_MODEL, 2 * LANES)),
                  const((1, LANES)), resident((tm, tm)),
                  of_group((D_MODEL, D_EXPERT)), of_group((D_MODEL, D_EXPERT)),
                  of_group((D_EXPERT, D_MODEL))],
        out_specs=pl.BlockSpec((tm, D_MODEL), lambda i, g: (i, 0)),
        out_shape=jax.ShapeDtypeStruct((t, D_MODEL), F32),
        scratch_shapes=[pltpu.VMEM((tm, D_MODEL), BF16), pltpu.VMEM((tm, 2 * LANES), F32),
                        pltpu.VMEM((tm, D_MODEL), BF16), pltpu.VMEM((tm, LANES), F32),
                        pltpu.SMEM((2 * N_GROUPS,), jnp.int32)],
        compiler_params=pltpu.CompilerParams(dimension_semantics=("parallel", "arbitrary"),
                                             vmem_limit_bytes=MOE_VMEM_LIMIT),
        name="outproj_moe",
    )(*ys, x2, g_out, w_out, p["g"], p["wr"], p["br"], tri, *experts)


def _prep_moe(g, w_rg, b_rg, w_re, b_re):
    wr = _pad_last(jnp.concatenate([w_rg, w_re], axis=1), LANES)
    wr_hi = wr.astype(BF16)
    wr_lo = (wr - wr_hi.astype(F32)).astype(BF16)
    br = _pad_last(jnp.concatenate([b_rg, b_re]), LANES)[None, :]
    return dict(g=g[None, :], wr=jnp.concatenate([wr_hi, wr_lo], axis=1), br=br)


def kernel(x, positions, mix_norm, w_in, mla_g_cq, mla_g_ckv, mla_w_uq, mla_w_ukv, mla_g_q, mla_g_k, lru_conv_w, lru_conv_b, lru_w_a, lru_b_a, lru_w_i, lru_b_i, lru_lambda, s5_a_re, s5_a_im, s5_log_dt, s5_b_re, s5_b_im, s5_c_re, s5_c_im, s5_d, s5_w_glu, s5_b_glu, nsa_g_q, nsa_g_k, nsa_pe_k, nsa_w1_k, nsa_w2_k, nsa_pe_v, nsa_w1_v, nsa_w2_v, out_norm, w_out, ffn_norm, moe_w_rg, moe_b_rg, moe_w_re, moe_b_re, moe_w_gate, moe_w_up, moe_w_down):
    b, s, d = x.shape
    t = b * s
    trig = _trig(positions.astype(jnp.int32)[:, :, None])
    x2 = x.reshape(t, d)
    experts = (moe_w_gate.astype(BF16), moe_w_up.astype(BF16), moe_w_down.astype(BF16))
    for l in range(w_in.shape[0]):
        pn = _prep_nsa(nsa_g_q[l], nsa_g_k[l], nsa_pe_k[l], nsa_w1_k[l], nsa_w2_k[l], nsa_pe_v[l],
                       nsa_w1_v[l], nsa_w2_v[l])
        mq, mk, mv, o_lru, o_s5, nq, ks, kw, vs, vw, cg = _proj_prep(
            x2.reshape(b, s, d), mix_norm[l], _prep_w_in(w_in[l]), trig,
            _prep_mla(mla_g_cq[l], mla_g_ckv[l], mla_w_uq[l], mla_w_ukv[l], mla_g_q[l], mla_g_k[l]), pn)
        y_a = _mla(mq, mk, mv)
        y_b = _lru(o_lru, _prep_lru(lru_conv_w[l], lru_conv_b[l], lru_w_a[l], lru_b_a[l], lru_w_i[l],
                                    lru_b_i[l], lru_lambda[l]))
        y_c = _s5(o_s5.transpose(1, 0, 2),
                  _prep_s5(s5_a_re[l], s5_a_im[l], s5_log_dt[l], s5_b_re[l], s5_b_im[l], s5_c_re[l],
                           s5_c_im[l], s5_d[l], s5_w_glu[l], s5_b_glu[l])).transpose(1, 0, 2)
        y_d = _nsa(nq, ks, kw, vs, vw, cg, trig, pn)
        ys = [y.reshape(t, GROUP_W) for y in (y_a, y_b, y_c, y_d)]
        x2 = _outproj_moe(ys, x2, out_norm[l], w_out[l].astype(BF16),
                          _prep_moe(ffn_norm[l], moe_w_rg[l], moe_b_rg[l], moe_w_re[l], moe_b_re[l]),
                          experts, l)
    return x2.reshape(b, s, d)
```
